```python
import math
import jax, jax.numpy as jnp
from jax import lax
import numpy as np

D_MODEL = 1024
BATCH = 2
SEQ = 16384
DEPTH = 1

HEAD_DIM = 64
DIL_WIDTH = D_MODEL // 2
N_HEADS_DIL = DIL_WIDTH // HEAD_DIM
DIL_PATTERNS = ((128, 1), (512, 4), (2048, 16))
DIFF_WIDTH = D_MODEL - DIL_WIDTH
N_HEADS_DIFF = DIFF_WIDTH // (2 * HEAD_DIM)
MIX_WIDTH = DIL_WIDTH + DIFF_WIDTH
Q_BLOCK = 128
N_EXPERTS = 32
TOP_K = 4
D_FF = D_MODEL
SWIGLU_ALPHA = 1.702
SWIGLU_LIMIT = 7.0
MOE_BLOCK = 256
DEEPNORM_ALPHA = (2.0 * DEPTH) ** 0.25
DEEPNORM_BETA = (8.0 * DEPTH) ** -0.25
LN_EPS = 1e-5
NEG_INF = -1e30

kernel_name = "hybrid_dilated_diffattn_moe_deepnorm"


def alibi_slopes(n):
    return jnp.asarray([2.0 ** (-8.0 * (i + 1) / n) for i in range(n)], jnp.float32)


def layer_norm(x, g, b):
    xf = x.astype(jnp.float32)
    mu = xf.mean(-1, keepdims=True)
    var = jnp.square(xf - mu).mean(-1, keepdims=True)
    return ((xf - mu) * lax.rsqrt(var + LN_EPS) * g + b).astype(x.dtype)


def dilated_pattern(q, k, v, window, dilation, slopes):
    B, H, S, Dh = q.shape
    n_side = (window // 2) // dilation
    L = S // dilation
    blk = n_side
    nb = -(-L // blk)
    Lp = nb * blk
    span = blk + 2 * n_side

    def to_res(a):
        return a.reshape(B, H, L, dilation, Dh).transpose(0, 1, 3, 2, 4)

    qr = jnp.pad(to_res(q), ((0, 0), (0, 0), (0, 0), (0, Lp - L), (0, 0)))
    kr = jnp.pad(to_res(k), ((0, 0), (0, 0), (0, 0), (n_side, n_side + Lp - L), (0, 0)))
    vr = jnp.pad(to_res(v), ((0, 0), (0, 0), (0, 0), (n_side, n_side + Lp - L), (0, 0)))
    kidx = (jnp.arange(nb) * blk)[:, None] + jnp.arange(span)[None, :]
    kb = kr[:, :, :, kidx]
    vb = vr[:, :, :, kidx].astype(jnp.float32)
    qb = qr.reshape(B, H, dilation, nb, blk, Dh)
    s = jnp.einsum('bhrnqd,bhrnkd->bhrnqk', qb, kb,
                   preferred_element_type=jnp.float32) * (HEAD_DIM ** -0.5)
    rel = jnp.arange(span)[None, :] - n_side - jnp.arange(blk)[:, None]
    key_m = kidx - n_side
    valid = (jnp.abs(rel) <= n_side)[None] & ((key_m >= 0) & (key_m < L))[:, None, :]
    bias = -(slopes * dilation)[:, None, None] * jnp.abs(rel).astype(jnp.float32)
    s = jnp.where(valid[None, None, None], s + bias[None, :, None, None], NEG_INF)
    m = s.max(-1)
    p = jnp.exp(s - m[..., None])
    l = p.sum(-1)
    acc = jnp.einsum('bhrnqk,bhrnkd->bhrnqd', p, vb)

    def from_res(a):
        tail = a.shape[5:]
        a = a.reshape(B, H, dilation, Lp, *tail)[:, :, :, :L]
        a = jnp.moveaxis(a, 2, 3)
        return a.reshape(B, H, S, *tail)

    return from_res(m), from_res(l), from_res(acc)


def dilated_attention(q, k, v, slopes):
    results = [dilated_pattern(q, k, v, w, d, slopes) for (w, d) in DIL_PATTERNS]
    ms = jnp.stack([r[0] for r in results])
    ls = jnp.stack([r[1] for r in results])
    accs = jnp.stack([r[2] for r in results])
    wts = jnp.exp(ms - ms.max(0, keepdims=True))
    num = (wts[..., None] * accs).sum(0)
    den = (wts * ls).sum(0)
    return num / den[..., None]


def diff_attention(q, k, v, slopes, lam):
    B, H, _, S, Dh = q.shape
    nq = S // Q_BLOCK
    qb = q.reshape(B, H, 2, nq, Q_BLOCK, Dh).transpose(3, 0, 1, 2, 4, 5)
    vf = v.astype(jnp.float32)
    key_pos = jnp.arange(S)

    def block(args):
        qblk, i = args
        s = jnp.einsum('bhcqd,bhckd->bhcqk', qblk, k,
                       preferred_element_type=jnp.float32) * (HEAD_DIM ** -0.5)
        qpos = i * Q_BLOCK + jnp.arange(Q_BLOCK)
        dist = jnp.abs(qpos[:, None] - key_pos[None, :]).astype(jnp.float32)
        s = s - (slopes[:, None, None] * dist)[None, :, None]
        p = jax.nn.softmax(s, axis=-1)
        a = p[:, :, 0] - lam * p[:, :, 1]
        return jnp.einsum('bhqk,bhkd->bhqd', a, vf)

    o = lax.map(block, (qb, jnp.arange(nq)))
    return o.transpose(1, 2, 0, 3, 4).reshape(B, H, S, 2 * Dh)


def token_mixers(h, w_in, w_out, lq1, lk1, lq2, lk2, g_sub, layer_idx):
    B, S, _ = h.shape
    proj = h @ w_in
    cuts = np.cumsum([DIL_WIDTH, DIL_WIDTH, DIL_WIDTH, DIFF_WIDTH, DIFF_WIDTH])
    qa, ka, va, qd, kd, vd = jnp.split(proj, [int(c) for c in cuts], axis=-1)

    def heads_a(t):
        return t.reshape(B, S, N_HEADS_DIL, HEAD_DIM).transpose(0, 2, 1, 3)

    oa = dilated_attention(heads_a(qa), heads_a(ka), heads_a(va), alibi_slopes(N_HEADS_DIL))
    oa = oa.transpose(0, 2, 1, 3).reshape(B, S, DIL_WIDTH)

    def heads_qk(t):
        return t.reshape(B, S, N_HEADS_DIFF, 2, HEAD_DIM).transpose(0, 2, 3, 1, 4)

    lam_init = 0.8 - 0.6 * math.exp(-0.3 * layer_idx)
    lam = (jnp.exp(jnp.sum(lq1.astype(jnp.float32) * lk1.astype(jnp.float32)))
           - jnp.exp(jnp.sum(lq2.astype(jnp.float32) * lk2.astype(jnp.float32))) + lam_init)
    vdh = vd.reshape(B, S, N_HEADS_DIFF, 2 * HEAD_DIM).transpose(0, 2, 1, 3)
    od = diff_attention(heads_qk(qd), heads_qk(kd), vdh, alibi_slopes(N_HEADS_DIFF), lam)
    od = od * lax.rsqrt(jnp.mean(od * od, -1, keepdims=True) + LN_EPS) * g_sub.astype(jnp.float32)
    od = (od * (1.0 - lam_init)).transpose(0, 2, 1, 3).reshape(B, S, DIFF_WIDTH)

    mixed = jnp.concatenate([oa, od], axis=-1).astype(h.dtype)
    return mixed @ w_out


def moe(h, w_router, b_router, w_up, b_up, w_down, b_down):
    B, S, D = h.shape
    T = B * S
    xf = h.reshape(T, D)
    logits = (xf @ w_router).astype(jnp.float32) + b_router.astype(jnp.float32)
    top_vals, top_idx = lax.top_k(logits, TOP_K)
    gates = jax.nn.softmax(top_vals, axis=-1)

    TK = T * TOP_K
    flat_e = top_idx.reshape(-1).astype(jnp.int32)
    flat_tok = jnp.arange(TK, dtype=jnp.int32) // TOP_K
    flat_gate = gates.reshape(-1)
    order = jnp.argsort(flat_e, stable=True)
    sorted_e = flat_e[order]
    counts = jnp.bincount(flat_e, length=N_EXPERTS)
    starts = jnp.cumsum(counts) - counts
    pcounts = ((counts + MOE_BLOCK - 1) // MOE_BLOCK) * MOE_BLOCK
    pends = jnp.cumsum(pcounts)
    pstarts = pends - pcounts
    dest = pstarts[sorted_e] + (jnp.arange(TK) - starts[sorted_e])

    n_rows = (-(-TK // MOE_BLOCK)) * MOE_BLOCK + N_EXPERTS * MOE_BLOCK
    nb = n_rows // MOE_BLOCK
    row_tok = jnp.full((n_rows,), T, jnp.int32).at[dest].set(flat_tok[order])
    row_gate = jnp.zeros((n_rows,), jnp.float32).at[dest].set(flat_gate[order])
    block_e = jnp.clip(jnp.searchsorted(pends, jnp.arange(nb) * MOE_BLOCK, side='right'),
                       0, N_EXPERTS - 1)
    x_pad = jnp.concatenate([xf, jnp.zeros((1, D), xf.dtype)], axis=0)

    def expert_block(args):
        tok, e = args
        hu = x_pad[tok] @ w_up[e] + b_up[e]
        g, u = hu[:, :D_FF], hu[:, D_FF:]
        g = jnp.minimum(g, SWIGLU_LIMIT)
        u = jnp.clip(u, -SWIGLU_LIMIT, SWIGLU_LIMIT)
        act = g * jax.nn.sigmoid(SWIGLU_ALPHA * g) * (u + 1.0)
        return act @ w_down[e] + b_down[e]

    y = lax.map(expert_block, (row_tok.reshape(nb, MOE_BLOCK), block_e)).reshape(n_rows, D)
    out = jnp.zeros((T + 1, D), jnp.float32).at[row_tok].add(y.astype(jnp.float32) * row_gate[:, None])
    return out[:T].reshape(B, S, D).astype(h.dtype)


def setup_inputs(seed: int = 0) -> dict:
    key = jax.random.key(seed)
    ks = jax.random.split(key, 20)
    n = lambda k, shape: jax.random.normal(k, shape, jnp.float32)
    return {
        "x": n(ks[0], (BATCH, SEQ, D_MODEL)),
        "w_in": n(ks[1], (DEPTH, D_MODEL, 3 * MIX_WIDTH)) * D_MODEL ** -0.5,
        "w_out": n(ks[2], (DEPTH, MIX_WIDTH, D_MODEL)) * MIX_WIDTH ** -0.5 * DEEPNORM_BETA,
        "lambda_q1": n(ks[3], (DEPTH, HEAD_DIM)) * 0.1,
        "lambda_k1": n(ks[4], (DEPTH, HEAD_DIM)) * 0.1,
        "lambda_q2": n(ks[5], (DEPTH, HEAD_DIM)) * 0.1,
        "lambda_k2": n(ks[6], (DEPTH, HEAD_DIM)) * 0.1,
        "diff_norm_g": 1.0 + 0.02 * n(ks[7], (DEPTH, 2 * HEAD_DIM)),
        "ln1_g": 1.0 + 0.02 * n(ks[8], (DEPTH, D_MODEL)),
        "ln1_b": 0.02 * n(ks[9], (DEPTH, D_MODEL)),
        "w_router": n(ks[10], (DEPTH, D_MODEL, N_EXPERTS)) * D_MODEL ** -0.5,
        "b_router": 0.01 * n(ks[11], (DEPTH, N_EXPERTS)),
        "w_up": n(ks[12], (DEPTH, N_EXPERTS, D_MODEL, 2 * D_FF)) * D_MODEL ** -0.5,
        "b_up": 0.01 * n(ks[13], (DEPTH, N_EXPERTS, 2 * D_FF)),
        "w_down": n(ks[14], (DEPTH, N_EXPERTS, D_FF, D_MODEL)) * D_FF ** -0.5 * DEEPNORM_BETA,
        "b_down": 0.01 * n(ks[15], (DEPTH, N_EXPERTS, D_MODEL)),
        "ln2_g": 1.0 + 0.02 * n(ks[16], (DEPTH, D_MODEL)),
        "ln2_b": 0.02 * n(ks[17], (DEPTH, D_MODEL)),
    }


def reference(x, w_in, w_out, lambda_q1, lambda_k1, lambda_q2, lambda_k2, diff_norm_g,
              ln1_g, ln1_b, w_router, b_router, w_up, b_up, w_down, b_down, ln2_g, ln2_b):
    for l in range(DEPTH):
        mix = token_mixers(x, w_in[l], w_out[l], lambda_q1[l], lambda_k1[l], lambda_q2[l],
                           lambda_k2[l], diff_norm_g[l], l)
        x = layer_norm(DEEPNORM_ALPHA * x + mix, ln1_g[l], ln1_b[l])
        ffn = moe(x, w_router[l], b_router[l], w_up[l], b_up[l], w_down[l], b_down[l])
        x = layer_norm(DEEPNORM_ALPHA * x + ffn, ln2_g[l], ln2_b[l])
    return x
```

```python
import functools
import math

import jax
import jax.numpy as jnp
import numpy as np
from jax import lax
from jax.experimental import pallas as pl
from jax.experimental.pallas import tpu as pltpu

D_MODEL = 1024
HEAD_DIM = 64
DIL_WIDTH = 512
N_HEADS_DIL = 8
DIL_PATTERNS = ((128, 1), (512, 4), (2048, 16))
DIL_SIDE = 64
DIFF_WIDTH = 512
N_HEADS_DIFF = 4
DIFF_VDIM = 2 * HEAD_DIM
N_EXPERTS = 32
TOP_K = 4
D_FF = D_MODEL
SWIGLU_ALPHA = 1.702
SWIGLU_LIMIT = 7.0
DEEPNORM_ALPHA = 2.0 ** 0.25
LN_EPS = 1e-5
NEG_INF = -1e30
LAM_INIT = 0.8 - 0.6 * math.exp(-0.3 * 0)

LANES = 128
VMEM_LIMIT = 56 * 1024 * 1024

ROW_TILE = 512
DIFF_TQ = 512
DIFF_TK = 512
DIL_TQ = 128
MOE_TILE = 512
POS_SPLIT = 16


def _cparams(sem):
    return pltpu.CompilerParams(dimension_semantics=sem, vmem_limit_bytes=VMEM_LIMIT)


def _dot_nt(a, b, **kw):
    return lax.dot_general(a, b, (((1,), (1,)), ((), ())), preferred_element_type=jnp.float32, **kw)


def _dot(a, b, **kw):
    return jnp.dot(a, b, preferred_element_type=jnp.float32, **kw)


def _layer_norm(y, g, b):
    mu = jnp.mean(y, axis=-1, keepdims=True)
    yc = y - mu
    var = jnp.mean(yc * yc, axis=-1, keepdims=True)
    return yc * lax.rsqrt(var + LN_EPS) * g + b


def _in_proj_kernel(x_ref, w_ref, kpos_ref, pa_ref, qd_ref, ka_ref, vd_ref):
    xb = x_ref[0].astype(jnp.bfloat16)
    na = 3 * DIL_WIDTH
    pa_ref[0] = _dot(xb, w_ref[:, :na]).astype(jnp.bfloat16)
    q = _dot(xb, w_ref[:, na:na + DIFF_WIDTH]).astype(jnp.bfloat16)
    k = _dot(xb, w_ref[:, na + DIFF_WIDTH:na + 2 * DIFF_WIDTH]).astype(jnp.bfloat16)
    v = _dot(xb, w_ref[:, na + 2 * DIFF_WIDTH:]).astype(jnp.bfloat16)
    for h in range(N_HEADS_DIFF):
        sl = slice(h * LANES, (h + 1) * LANES)
        qd_ref[0, h] = q[:, sl]
        ka_ref[0, h, :, :LANES] = k[:, sl]
        ka_ref[0, h, :, LANES:] = kpos_ref[...]
        vd_ref[0, h] = v[:, sl]


def _in_proj(x, w_bf, kpos):
    B, S, D = x.shape
    tm = ROW_TILE
    na = 3 * DIL_WIDTH
    return pl.pallas_call(
        _in_proj_kernel,
        grid=(B, S // tm),
        in_specs=[
            pl.BlockSpec((1, tm, D), lambda b, i: (b, i, 0)),
            pl.BlockSpec(w_bf.shape, lambda b, i: (0, 0)),
            pl.BlockSpec(kpos.shape, lambda b, i: (0, 0)),
        ],
        out_specs=[
            pl.BlockSpec((1, tm, na), lambda b, i: (b, i, 0)),
            pl.BlockSpec((1, N_HEADS_DIFF, tm, LANES), lambda b, i: (b, 0, i, 0)),
            pl.BlockSpec((1, N_HEADS_DIFF, tm, 2 * LANES), lambda b, i: (b, 0, i, 0)),
            pl.BlockSpec((1, N_HEADS_DIFF, tm, LANES), lambda b, i: (b, 0, i, 0)),
        ],
        out_shape=[
            jax.ShapeDtypeStruct((B, S, na), jnp.bfloat16),
            jax.ShapeDtypeStruct((B, N_HEADS_DIFF, S, LANES), jnp.bfloat16),
            jax.ShapeDtypeStruct((B, N_HEADS_DIFF, S, 2 * LANES), jnp.bfloat16),
            jax.ShapeDtypeStruct((B, N_HEADS_DIFF, S, LANES), jnp.bfloat16),
        ],
        compiler_params=_cparams(("parallel", "parallel")),
        name="in_proj",
    )(x, w_bf, kpos)


def _dilated_kernel(*refs, n_blocks, seq_len, has_prev, final):
    (q_ref, kp_ref, kc_ref, kn_ref, vp_ref, vc_ref, vn_ref, bias_ref), refs = refs[:8], refs[8:]
    if has_prev:
        (acc0_ref, m0_ref, l0_ref), refs = refs[:3], refs[3:]
    if final:
        (o_ref,) = refs
    else:
        acc_ref, m_ref, l_ref = refs
    tq = q_ref.shape[1]
    i = pl.program_id(2)
    q = q_ref[0]
    kcat = jnp.concatenate([kp_ref[0], kc_ref[0], kn_ref[0]], axis=0)
    vcat = jnp.concatenate([vp_ref[0], vc_ref[0], vn_ref[0]], axis=0)
    key_pos = (i - 1) * tq + lax.broadcasted_iota(jnp.int32, (1, 3 * tq), 1)
    edge = jnp.where((key_pos >= 0) & (key_pos < seq_len), 0.0, NEG_INF)
    low_half = lax.broadcasted_iota(jnp.int32, (tq, LANES), 1) < HEAD_DIM
    for pair in range(N_HEADS_DIL // 2):
        sl = slice(pair * LANES, (pair + 1) * LANES)
        qp, kp, vp = q[:, sl], kcat[:, sl], vcat[:, sl]
        res = []
        for hh in range(2):
            qm = jnp.where(low_half == (hh == 0), qp, jnp.zeros_like(qp))
            s = _dot_nt(qm, kp) + bias_ref[2 * pair + hh] + edge
            m = jnp.max(s, axis=-1, keepdims=True)
            p = jnp.exp(s - m)
            l = jnp.sum(p, axis=-1, keepdims=True)
            a = _dot(p.astype(jnp.bfloat16), vp)
            res.append((a, m, l))
        acc = jnp.where(low_half, res[0][0], res[1][0])
        m = jnp.where(low_half, res[0][1], res[1][1])
        l = jnp.where(low_half, res[0][2], res[1][2])
        if has_prev:
            m0 = m0_ref[0, :, sl]
            mm = jnp.maximum(m0, m)
            w0 = jnp.exp(m0 - mm)
            w1 = jnp.exp(m - mm)
            acc = acc0_ref[0, :, sl] * w0 + acc * w1
            l = l0_ref[0, :, sl] * w0 + l * w1
            m = mm
        if final:
            o_ref[0, :, sl] = (acc / l).astype(o_ref.dtype)
        else:
            acc_ref[0, :, sl] = acc
            m_ref[0, :, sl] = m
            l_ref[0, :, sl] = l


def _dilated_stage(pa, bias, dilation, prev, final):
    B, S, _ = pa.shape
    d = dilation
    L = S // d
    tq = DIL_TQ
    nb = L // tq
    w = DIL_WIDTH
    pav = pa.reshape(B, L, d * 3 * w)

    def spec(col, shift):
        def imap(b, r, i):
            return (b, jnp.clip(i + shift, 0, nb - 1), r * 3 + col)
        return pl.BlockSpec((1, tq, w), imap)

    stat_spec = pl.BlockSpec((1, tq, w), lambda b, r, i: (b, i, r))
    in_specs = [spec(0, 0), spec(1, -1), spec(1, 0), spec(1, 1), spec(2, -1), spec(2, 0), spec(2, 1),
                pl.BlockSpec(bias.shape, lambda b, r, i: (0, 0, 0))]
    args = [pav, pav, pav, pav, pav, pav, pav, bias]
    if prev is not None:
        in_specs += [stat_spec] * 3
        args += [a.reshape(B, L, d * w) for a in prev]
    if final:
        out_specs = stat_spec
        out_shape = jax.ShapeDtypeStruct((B, L, d * w), jnp.bfloat16)
    else:
        out_specs = [stat_spec] * 3
        out_shape = [jax.ShapeDtypeStruct((B, L, d * w), jnp.float32)] * 3
    out = pl.pallas_call(
        functools.partial(_dilated_kernel, n_blocks=nb, seq_len=L, has_prev=prev is not None, final=final),
        grid=(B, d, nb),
        in_specs=in_specs,
        out_specs=out_specs,
        out_shape=out_shape,
        compiler_params=_cparams(("parallel", "parallel", "parallel")),
        name=f"dilated_d{d}",
    )(*args)
    if final:
        return out.reshape(B, S, w)
    return tuple(o.reshape(B, S, w) for o in out)


def _dilated_bias(dilation):
    tq = DIL_TQ
    slopes = np.asarray([2.0 ** (-8.0 * (i + 1) / N_HEADS_DIL) for i in range(N_HEADS_DIL)], np.float32)
    rel = (np.arange(3 * tq)[None, :] - tq) - np.arange(tq)[:, None]
    band = np.abs(rel) <= DIL_SIDE
    pen = -(slopes * dilation)[:, None, None] * np.abs(rel).astype(np.float32)[None]
    return jnp.asarray(np.where(band[None], pen, np.float32(NEG_INF)), jnp.float32)


def _diff_kernel(q_ref, ka_ref, v_ref, qpos_ref, dabs_ref, lq1_ref, lk1_ref, lq2_ref, lk2_ref, g_ref,
                 o_ref, qa_scr, m_scr, l_scr, acc_scr, *, slopes):
    h = pl.program_id(1)
    qi = pl.program_id(2)
    ki = pl.program_id(3)
    nk = pl.num_programs(3)
    tq = q_ref.shape[2]
    tk = ka_ref.shape[2]

    slope = jnp.float32(slopes[0])
    for hh in range(1, N_HEADS_DIFF):
        slope = jnp.where(h == hh, jnp.float32(slopes[hh]), slope)

    @pl.when(ki == 0)
    def _():
        q = q_ref[0, 0]
        low_half = lax.broadcasted_iota(jnp.int32, (tq, LANES), 1) < HEAD_DIM
        zero = jnp.zeros_like(q)
        qpos = qpos_ref[...] * slope.astype(jnp.bfloat16)
        for c in range(2):
            qc = jnp.where(low_half == (c == 0), q, zero)
            for var, pos in enumerate((qpos, -qpos, jnp.zeros_like(qpos))):
                qa_scr[c, var, :, :LANES] = qc
                qa_scr[c, var, :, LANES:] = pos
        m_scr[...] = jnp.full(m_scr.shape, NEG_INF, jnp.float32)
        l_scr[...] = jnp.zeros(l_scr.shape, jnp.float32)
        acc_scr[...] = jnp.zeros(acc_scr.shape, jnp.float32)

    var = jnp.where(ki < qi, 0, jnp.where(ki > qi, 1, 2))
    off = -slope * jnp.abs(qi * tq - ki * tk).astype(jnp.float32)
    ka = ka_ref[0, 0]
    v = v_ref[0, 0]

    def step(diag):
        for c in range(2):
            s = _dot_nt(qa_scr[c, var], ka)
            if diag:
                s = s + dabs_ref[...] * slope
            m_old = m_scr[c]
            m_new = jnp.maximum(m_old, jnp.max(s, axis=-1, keepdims=True) + off)
            alpha = jnp.exp(m_old - m_new)
            p = jnp.exp(s - (m_new - off))
            l_scr[c] = alpha * l_scr[c] + jnp.sum(p, axis=-1, keepdims=True)
            acc_scr[c] = alpha * acc_scr[c] + _dot(p.astype(jnp.bfloat16), v)
            m_scr[c] = m_new

    @pl.when(ki == qi)
    def _():
        step(True)

    @pl.when(ki != qi)
    def _():
        step(False)

    @pl.when(ki == nk - 1)
    def _():
        lam = (jnp.exp(jnp.sum(lq1_ref[...] * lk1_ref[...], axis=-1, keepdims=True))
               - jnp.exp(jnp.sum(lq2_ref[...] * lk2_ref[...], axis=-1, keepdims=True)) + LAM_INIT)
        o = acc_scr[0] / l_scr[0] - lam * (acc_scr[1] / l_scr[1])
        o = o * lax.rsqrt(jnp.mean(o * o, axis=-1, keepdims=True) + LN_EPS) * g_ref[...]
        o_ref[0] = (o * (1.0 - LAM_INIT)).astype(o_ref.dtype)


def _diff_attention(qd, ka, vd, qpos, dabs, lq1, lk1, lq2, lk2, g_sub):
    B, H, S, _ = qd.shape
    tq, tk = DIFF_TQ, DIFF_TK
    slopes = tuple(2.0 ** (-8.0 * (i + 1) / H) for i in range(H))
    small = lambda a: pl.BlockSpec(a.shape, lambda b, h, qi, ki: (0, 0))
    return pl.pallas_call(
        functools.partial(_diff_kernel, slopes=slopes),
        grid=(B, H, S // tq, S // tk),
        in_specs=[
            pl.BlockSpec((1, 1, tq, LANES), lambda b, h, qi, ki: (b, h, qi, 0)),
            pl.BlockSpec((1, 1, tk, 2 * LANES), lambda b, h, qi, ki: (b, h, ki, 0)),
            pl.BlockSpec((1, 1, tk, LANES), lambda b, h, qi, ki: (b, h, ki, 0)),
            small(qpos), small(dabs), small(lq1), small(lk1), small(lq2), small(lk2), small(g_sub),
        ],
        out_specs=pl.BlockSpec((1, tq, LANES), lambda b, h, qi, ki: (b, qi, h)),
        out_shape=jax.ShapeDtypeStruct((B, S, H * LANES), jnp.bfloat16),
        scratch_shapes=[
            pltpu.VMEM((2, 3, tq, 2 * LANES), jnp.bfloat16),
            pltpu.VMEM((2, tq, 1), jnp.float32),
            pltpu.VMEM((2, tq, 1), jnp.float32),
            pltpu.VMEM((2, tq, LANES), jnp.float32),
        ],
        compiler_params=_cparams(("parallel", "parallel", "parallel", "arbitrary")),
        name="diff_attention",
    )(qd, ka, vd, qpos, dabs, lq1, lk1, lq2, lk2, g_sub)


def _position_columns():
    def cols(n, as_key):
        p = np.arange(n)
        hi, lo = (p // POS_SPLIT) * POS_SPLIT, p % POS_SPLIT
        one = np.ones(n)
        c = np.zeros((n, LANES), np.float32)
        if as_key:
            c[:, 0], c[:, 1], c[:, 2], c[:, 3] = hi, lo, one, one
        else:
            c[:, 0], c[:, 1], c[:, 2], c[:, 3] = one, one, -hi, -lo
        return jnp.asarray(c, jnp.bfloat16)

    di = np.arange(DIFF_TQ)[:, None]
    dj = np.arange(DIFF_TK)[None, :]
    dabs = jnp.asarray(-np.abs(di - dj), jnp.float32)
    return cols(DIFF_TK, True), cols(DIFF_TQ, False), dabs


def _post_attn_kernel(oa_ref, od_ref, x_ref, wo_ref, g_ref, b_ref, wr_ref, br_ref, tri_ref,
                      x1_ref, x1b_ref, idx_ref, gate_ref, rank_ref, cnt_ref, carry_scr):
    step = pl.program_id(0)

    @pl.when(step == 0)
    def _():
        carry_scr[...] = jnp.zeros(carry_scr.shape, jnp.float32)

    mix = _dot(oa_ref[...], wo_ref[:DIL_WIDTH, :]) + _dot(od_ref[...], wo_ref[DIL_WIDTH:, :])
    x1 = _layer_norm(DEEPNORM_ALPHA * x_ref[...] + mix, g_ref[...], b_ref[...])
    x1_ref[...] = x1
    x1b_ref[...] = x1.astype(jnp.bfloat16)

    tm = x1.shape[0]
    lg = _dot_nt(wr_ref[...], x1, precision=lax.Precision.HIGHEST) + br_ref[...]
    eidx = lax.broadcasted_iota(jnp.int32, (N_EXPERTS, tm), 0)
    vals, sels = [], []
    for k in range(TOP_K):
        mx = jnp.max(lg, axis=0, keepdims=True)
        idx = jnp.min(jnp.where(lg == mx, eidx, N_EXPERTS), axis=0, keepdims=True)
        sel = eidx == idx
        vals.append(mx)
        sels.append(sel)
        idx_ref[k:k + 1, :] = idx
        lg = jnp.where(sel, -jnp.inf, lg)
    ex = [jnp.exp(v - vals[0]) for v in vals]
    den = ex[0] + ex[1] + ex[2] + ex[3]
    for k in range(TOP_K):
        gate_ref[k:k + 1, :] = ex[k] / den

    chosen = (sels[0] | sels[1] | sels[2] | sels[3])
    onehot = jnp.where(chosen, 1.0, 0.0)
    before = _dot(onehot.astype(jnp.bfloat16), tri_ref[...]) + carry_scr[...]
    for k in range(TOP_K):
        rank_ref[k:k + 1, :] = jnp.sum(jnp.where(sels[k], before, 0.0), axis=0,
                                       keepdims=True).astype(jnp.int32)
    carry_scr[...] = carry_scr[...] + jnp.sum(onehot, axis=1, keepdims=True)
    pad = jnp.zeros((8 - TOP_K, tm), jnp.int32)
    idx_ref[TOP_K:, :] = pad
    rank_ref[TOP_K:, :] = pad
    gate_ref[TOP_K:, :] = pad.astype(jnp.float32)
    cnt_ref[...] = jnp.broadcast_to(carry_scr[...], cnt_ref.shape).astype(jnp.int32)


def _post_attn(oa, od, x, wo_bf, g, b, wr_t, br, tri):
    T, D = x.shape
    tm = ROW_TILE
    row = lambda w: pl.BlockSpec((tm, w), lambda i: (i, 0))
    full = lambda a: pl.BlockSpec(a.shape, lambda i: (0, 0))
    col = pl.BlockSpec((8, tm), lambda i: (0, i))
    return pl.pallas_call(
        _post_attn_kernel,
        grid=(T // tm,),
        in_specs=[row(DIL_WIDTH), row(DIFF_WIDTH), row(D), full(wo_bf), full(g), full(b), full(wr_t),
                  full(br), full(tri)],
        out_specs=[row(D), row(D), col, col, col, pl.BlockSpec((N_EXPERTS, LANES), lambda i: (0, 0))],
        out_shape=[
            jax.ShapeDtypeStruct((T, D), jnp.float32),
            jax.ShapeDtypeStruct((T, D), jnp.bfloat16),
            jax.ShapeDtypeStruct((8, T), jnp.int32),
            jax.ShapeDtypeStruct((8, T), jnp.float32),
            jax.ShapeDtypeStruct((8, T), jnp.int32),
            jax.ShapeDtypeStruct((N_EXPERTS, LANES), jnp.int32),
        ],
        scratch_shapes=[pltpu.VMEM((N_EXPERTS, 1), jnp.float32)],
        compiler_params=_cparams(("arbitrary",)),
        name="post_attn_router",
    )(oa, od, x, wo_bf, g, b, wr_t, br, tri)


def _expert_kernel(tile_e_ref, n_used_ref, xs_ref, wu_ref, bu_ref, wd_ref, bd_ref, y_ref):
    i = pl.program_id(0)

    @pl.when(i < n_used_ref[0])
    def _():
        hu = _dot(xs_ref[...], wu_ref[0]) + bu_ref[0]
        g = jnp.minimum(hu[:, :D_FF], SWIGLU_LIMIT)
        u = jnp.clip(hu[:, D_FF:], -SWIGLU_LIMIT, SWIGLU_LIMIT)
        act = g * (1.0 / (1.0 + jnp.exp(-SWIGLU_ALPHA * g))) * (u + 1.0)
        y_ref[...] = _dot(act.astype(jnp.bfloat16), wd_ref[0]) + bd_ref[0]

    @pl.when(i >= n_used_ref[0])
    def _():
        y_ref[...] = jnp.zeros(y_ref.shape, y_ref.dtype)


def _experts(tile_e, n_used, xs, wu_bf, bu, wd_bf, bd):
    n_rows, D = xs.shape
    tm = MOE_TILE
    grid_spec = pltpu.PrefetchScalarGridSpec(
        num_scalar_prefetch=2,
        grid=(n_rows // tm,),
        in_specs=[
            pl.BlockSpec((tm, D), lambda i, te, nu: (i, 0)),
            pl.BlockSpec((1, D, 2 * D_FF), lambda i, te, nu: (te[i], 0, 0)),
            pl.BlockSpec((1, 1, 2 * D_FF), lambda i, te, nu: (te[i], 0, 0)),
            pl.BlockSpec((1, D_FF, D), lambda i, te, nu: (te[i], 0, 0)),
            pl.BlockSpec((1, 1, D), lambda i, te, nu: (te[i], 0, 0)),
        ],
        out_specs=pl.BlockSpec((tm, D), lambda i, te, nu: (i, 0)),
    )
    return pl.pallas_call(
        _expert_kernel,
        grid_spec=grid_spec,
        out_shape=jax.ShapeDtypeStruct((n_rows, D), jnp.float32),
        compiler_params=_cparams(("arbitrary",)),
        name="experts",
    )(tile_e, n_used, xs, wu_bf, bu, wd_bf, bd)


def _combine_kernel(x1_ref, yg_ref, gate_ref, g_ref, b_ref, o_ref):
    ffn = yg_ref[0] * gate_ref[:, 0:1]
    for k in range(1, TOP_K):
        ffn = ffn + yg_ref[k] * gate_ref[:, k:k + 1]
    o_ref[...] = _layer_norm(DEEPNORM_ALPHA * x1_ref[...] + ffn, g_ref[...], b_ref[...])


def _combine(x1, yg, gates_t, g, b):
    T, D = x1.shape
    tm = ROW_TILE
    full = lambda a: pl.BlockSpec(a.shape, lambda i: (0, 0))
    return pl.pallas_call(
        _combine_kernel,
        grid=(T // tm,),
        in_specs=[
            pl.BlockSpec((tm, D), lambda i: (i, 0)),
            pl.BlockSpec((TOP_K, tm, D), lambda i: (0, i, 0)),
            pl.BlockSpec((tm, 8), lambda i: (i, 0)),
            full(g), full(b),
        ],
        out_specs=pl.BlockSpec((tm, D), lambda i: (i, 0)),
        out_shape=jax.ShapeDtypeStruct((T, D), jnp.float32),
        compiler_params=_cparams(("parallel",)),
        name="combine_ln2",
    )(x1, yg, gates_t, g, b)


def kernel(x, w_in, w_out, lambda_q1, lambda_k1, lambda_q2, lambda_k2, diff_norm_g, ln1_g, ln1_b,
           w_router, b_router, w_up, b_up, w_down, b_down, ln2_g, ln2_b):
    B, S, D = x.shape
    T = B * S
    assert w_in.shape[0] == 1, "single layer"
    assert S % (DIFF_TQ * 1) == 0 and DIFF_TK == ROW_TILE and S % (16 * DIL_TQ) == 0

    na = 3 * DIL_WIDTH
    colscale = np.ones((3 * D,), np.float32)
    colscale[:DIL_WIDTH] = HEAD_DIM ** -0.5
    colscale[na:na + DIFF_WIDTH] = HEAD_DIM ** -0.5
    w_in_bf = (w_in[0] * colscale).astype(jnp.bfloat16)

    kpos, qpos, dabs = _position_columns()
    pa, qd, ka, vd = _in_proj(x, w_in_bf, kpos)

    stats = None
    for n, (_, dil) in enumerate(DIL_PATTERNS):
        final = n == len(DIL_PATTERNS) - 1
        stats = _dilated_stage(pa, _dilated_bias(dil), dil, stats, final)
    oa = stats

    r2 = lambda a: a.reshape(1, -1).astype(jnp.float32)
    od = _diff_attention(qd, ka, vd, qpos, dabs, r2(lambda_q1[0]), r2(lambda_k1[0]), r2(lambda_q2[0]),
                         r2(lambda_k2[0]), r2(diff_norm_g[0]))

    tri = jnp.asarray(np.triu(np.ones((ROW_TILE, ROW_TILE), np.float32), k=1), jnp.bfloat16)
    x1, x1b, top_idx, gates, rank, counts = _post_attn(
        oa.reshape(T, DIL_WIDTH), od.reshape(T, DIFF_WIDTH), x.reshape(T, D), w_out[0].astype(jnp.bfloat16),
        r2(ln1_g[0]), r2(ln1_b[0]), w_router[0].T, b_router[0].reshape(N_EXPERTS, 1), tri)

    counts = counts[:, 0]
    pcounts = ((counts + MOE_TILE - 1) // MOE_TILE) * MOE_TILE
    pends = jnp.cumsum(pcounts)
    pstarts = pends - pcounts
    n_rows = T * TOP_K + N_EXPERTS * MOE_TILE
    n_tiles = n_rows // MOE_TILE
    dest = pstarts[top_idx[:TOP_K]] + rank[:TOP_K]
    tile_e = jnp.clip(jnp.searchsorted(pends, jnp.arange(n_tiles) * MOE_TILE, side='right'),
                      0, N_EXPERTS - 1).astype(jnp.int32)
    n_used = (pends[-1] // MOE_TILE).astype(jnp.int32).reshape(1)

    row_tok = jnp.zeros((n_rows,), jnp.int32).at[dest.reshape(-1)].set(
        jnp.tile(jnp.arange(T, dtype=jnp.int32), TOP_K))
    xs = x1b[row_tok]

    y = _experts(tile_e, n_used, xs, w_up[0].astype(jnp.bfloat16), b_up[0][:, None, :],
                 w_down[0].astype(jnp.bfloat16), b_down[0][:, None, :])
    yg = y[dest]

    out = _combine(x1, yg, gates.T, r2(ln2_g[0]), r2(ln2_b[0]))
    return out.reshape(B, S, D)
```

```python
import functools
import math

import jax
import jax.numpy as jnp
import numpy as np
from jax import lax
from jax.experimental import pallas as pl
from jax.experimental.pallas import tpu as pltpu

D_MODEL = 1024
HEAD_DIM = 64
DIL_WIDTH = 512
N_HEADS_DIL = 8
DIL_PATTERNS = ((128, 1), (512, 4), (2048, 16))
DIL_SIDE = 64
DIFF_WIDTH = 512
N_HEADS_DIFF = 4
DIFF_VDIM = 2 * HEAD_DIM
N_EXPERTS = 32
TOP_K = 4
D_FF = D_MODEL
SWIGLU_ALPHA = 1.702
SWIGLU_LIMIT = 7.0
DEEPNORM_ALPHA = 2.0 ** 0.25
LN_EPS = 1e-5
NEG_INF = -1e30
LAM_INIT = 0.8 - 0.6 * math.exp(-0.3 * 0)

LANES = 128
VMEM_LIMIT = 56 * 1024 * 1024

ROW_TILE = 512
DIFF_TQ = 1024
DIFF_TK = 1024
V_ROWS = DIFF_VDIM + 16
DIL_TQ = 128
MOE_TILE = 512
POS_SPLIT = 16


def _cparams(sem):
    return pltpu.CompilerParams(dimension_semantics=sem, vmem_limit_bytes=VMEM_LIMIT)


def _dot_nt(a, b, **kw):
    return lax.dot_general(a, b, (((1,), (1,)), ((), ())), preferred_element_type=jnp.float32, **kw)


def _dot(a, b, **kw):
    return jnp.dot(a, b, preferred_element_type=jnp.float32, **kw)


def _layer_norm(y, g, b):
    mu = jnp.mean(y, axis=-1, keepdims=True)
    yc = y - mu
    var = jnp.mean(yc * yc, axis=-1, keepdims=True)
    return yc * lax.rsqrt(var + LN_EPS) * g + b


def _in_proj_kernel(x_ref, w_ref, wvt_ref, kpos_ref, pa_ref, qd_ref, ka_ref, vt_ref):
    xb = x_ref[0].astype(jnp.bfloat16)
    na = 3 * DIL_WIDTH
    tm = xb.shape[0]
    pa_ref[0] = _dot(xb, w_ref[:, :na]).astype(jnp.bfloat16)
    q = _dot(xb, w_ref[:, na:na + DIFF_WIDTH]).astype(jnp.bfloat16)
    k = _dot(xb, w_ref[:, na + DIFF_WIDTH:]).astype(jnp.bfloat16)
    vt = _dot_nt(wvt_ref[...], xb).astype(jnp.bfloat16)
    ones = jnp.ones((V_ROWS - DIFF_VDIM, tm), jnp.bfloat16)
    for h in range(N_HEADS_DIFF):
        sl = slice(h * LANES, (h + 1) * LANES)
        qd_ref[0, h] = q[:, sl]
        ka_ref[0, h, :, :LANES] = k[:, sl]
        ka_ref[0, h, :, LANES:] = kpos_ref[...]
        vt_ref[0, h, :DIFF_VDIM, :] = vt[sl, :]
        vt_ref[0, h, DIFF_VDIM:, :] = ones


def _in_proj(x, w_bf, wvt_bf, kpos):
    B, S, D = x.shape
    tm = ROW_TILE
    na = 3 * DIL_WIDTH
    return pl.pallas_call(
        _in_proj_kernel,
        grid=(B, S // tm),
        in_specs=[
            pl.BlockSpec((1, tm, D), lambda b, i: (b, i, 0)),
            pl.BlockSpec(w_bf.shape, lambda b, i: (0, 0)),
            pl.BlockSpec(wvt_bf.shape, lambda b, i: (0, 0)),
            pl.BlockSpec((tm, LANES), lambda b, i: (i % (DIFF_TK // tm), 0)),
        ],
        out_specs=[
            pl.BlockSpec((1, tm, na), lambda b, i: (b, i, 0)),
            pl.BlockSpec((1, N_HEADS_DIFF, tm, LANES), lambda b, i: (b, 0, i, 0)),
            pl.BlockSpec((1, N_HEADS_DIFF, tm, 2 * LANES), lambda b, i: (b, 0, i, 0)),
            pl.BlockSpec((1, N_HEADS_DIFF, V_ROWS, tm), lambda b, i: (b, 0, 0, i)),
        ],
        out_shape=[
            jax.ShapeDtypeStruct((B, S, na), jnp.bfloat16),
            jax.ShapeDtypeStruct((B, N_HEADS_DIFF, S, LANES), jnp.bfloat16),
            jax.ShapeDtypeStruct((B, N_HEADS_DIFF, S, 2 * LANES), jnp.bfloat16),
            jax.ShapeDtypeStruct((B, N_HEADS_DIFF, V_ROWS, S), jnp.bfloat16),
        ],
        compiler_params=_cparams(("parallel", "parallel")),
        name="in_proj",
    )(x, w_bf, wvt_bf, kpos)


def _dilated_kernel(*refs, n_blocks, seq_len, has_prev, final):
    (q_ref, kp_ref, kc_ref, kn_ref, vp_ref, vc_ref, vn_ref, bias_ref), refs = refs[:8], refs[8:]
    if has_prev:
        (acc0_ref, m0_ref, l0_ref), refs = refs[:3], refs[3:]
    if final:
        (o_ref,) = refs
    else:
        acc_ref, m_ref, l_ref = refs
    tq = q_ref.shape[1]
    i = pl.program_id(2)
    q = q_ref[0]
    kcat = jnp.concatenate([kp_ref[0], kc_ref[0], kn_ref[0]], axis=0)
    vcat = jnp.concatenate([vp_ref[0], vc_ref[0], vn_ref[0]], axis=0)
    key_pos = (i - 1) * tq + lax.broadcasted_iota(jnp.int32, (1, 3 * tq), 1)
    edge = jnp.where((key_pos >= 0) & (key_pos < seq_len), 0.0, NEG_INF)
    low_half = lax.broadcasted_iota(jnp.int32, (tq, LANES), 1) < HEAD_DIM
    for pair in range(N_HEADS_DIL // 2):
        sl = slice(pair * LANES, (pair + 1) * LANES)
        qp, kp, vp = q[:, sl], kcat[:, sl], vcat[:, sl]
        res = []
        for hh in range(2):
            qm = jnp.where(low_half == (hh == 0), qp, jnp.zeros_like(qp))
            s = _dot_nt(qm, kp) + bias_ref[2 * pair + hh] + edge
            m = jnp.max(s, axis=-1, keepdims=True)
            p = jnp.exp(s - m)
            l = jnp.sum(p, axis=-1, keepdims=True)
            a = _dot(p.astype(jnp.bfloat16), vp)
            res.append((a, m, l))
        acc = jnp.where(low_half, res[0][0], res[1][0])
        m = jnp.where(low_half, res[0][1], res[1][1])
        l = jnp.where(low_half, res[0][2], res[1][2])
        if has_prev:
            m0 = m0_ref[0, :, sl]
            mm = jnp.maximum(m0, m)
            w0 = jnp.exp(m0 - mm)
            w1 = jnp.exp(m - mm)
            acc = acc0_ref[0, :, sl] * w0 + acc * w1
            l = l0_ref[0, :, sl] * w0 + l * w1
            m = mm
        if final:
            o_ref[0, :, sl] = (acc / l).astype(o_ref.dtype)
        else:
            acc_ref[0, :, sl] = acc
            m_ref[0, :, sl] = m
            l_ref[0, :, sl] = l


def _dilated_stage(pa, bias, dilation, prev, final):
    B, S, _ = pa.shape
    d = dilation
    L = S // d
    tq = DIL_TQ
    nb = L // tq
    w = DIL_WIDTH
    pav = pa.reshape(B, L, d * 3 * w)

    def spec(col, shift):
        def imap(b, r, i):
            return (b, jnp.clip(i + shift, 0, nb - 1), r * 3 + col)
        return pl.BlockSpec((1, tq, w), imap)

    stat_spec = pl.BlockSpec((1, tq, w), lambda b, r, i: (b, i, r))
    in_specs = [spec(0, 0), spec(1, -1), spec(1, 0), spec(1, 1), spec(2, -1), spec(2, 0), spec(2, 1),
                pl.BlockSpec(bias.shape, lambda b, r, i: (0, 0, 0))]
    args = [pav, pav, pav, pav, pav, pav, pav, bias]
    if prev is not None:
        in_specs += [stat_spec] * 3
        args += [a.reshape(B, L, d * w) for a in prev]
    if final:
        out_specs = stat_spec
        out_shape = jax.ShapeDtypeStruct((B, L, d * w), jnp.bfloat16)
    else:
        out_specs = [stat_spec] * 3
        out_shape = [jax.ShapeDtypeStruct((B, L, d * w), jnp.float32)] * 3
    out = pl.pallas_call(
        functools.partial(_dilated_kernel, n_blocks=nb, seq_len=L, has_prev=prev is not None, final=final),
        grid=(B, d, nb),
        in_specs=in_specs,
        out_specs=out_specs,
        out_shape=out_shape,
        compiler_params=_cparams(("parallel", "parallel", "parallel")),
        name=f"dilated_d{d}",
    )(*args)
    if final:
        return out.reshape(B, S, w)
    return tuple(o.reshape(B, S, w) for o in out)


def _dilated_bias(dilation):
    tq = DIL_TQ
    slopes = np.asarray([2.0 ** (-8.0 * (i + 1) / N_HEADS_DIL) for i in range(N_HEADS_DIL)], np.float32)
    rel = (np.arange(3 * tq)[None, :] - tq) - np.arange(tq)[:, None]
    band = np.abs(rel) <= DIL_SIDE
    pen = -(slopes * dilation)[:, None, None] * np.abs(rel).astype(np.float32)[None]
    return jnp.asarray(np.where(band[None], pen, np.float32(NEG_INF)), jnp.float32)


def _diff_kernel(q_ref, ka_ref, vt_ref, qpos_ref, dabs_ref, lq1_ref, lk1_ref, lq2_ref, lk2_ref, g_ref,
                 o_ref, qa_scr, m_scr, acc_scr, *, slopes):
    h = pl.program_id(1)
    qi = pl.program_id(2)
    ki = pl.program_id(3)
    nk = pl.num_programs(3)
    tq = q_ref.shape[2]
    tk = ka_ref.shape[2]

    slope = jnp.float32(slopes[0])
    for hh in range(1, N_HEADS_DIFF):
        slope = jnp.where(h == hh, jnp.float32(slopes[hh]), slope)

    @pl.when(ki == 0)
    def _():
        q = q_ref[0, 0]
        low_half = lax.broadcasted_iota(jnp.int32, (tq, LANES), 1) < HEAD_DIM
        zero = jnp.zeros_like(q)
        qpos = qpos_ref[...] * slope.astype(jnp.bfloat16)
        for c in range(2):
            qc = jnp.where(low_half == (c == 0), q, zero)
            for var, pos in enumerate((qpos, -qpos, jnp.zeros_like(qpos))):
                qa_scr[c, var, :, :LANES] = qc
                qa_scr[c, var, :, LANES:] = pos
        m_scr[...] = jnp.full(m_scr.shape, NEG_INF, jnp.float32)
        acc_scr[...] = jnp.zeros(acc_scr.shape, jnp.float32)

    var = jnp.where(ki < qi, 0, jnp.where(ki > qi, 1, 2))
    off = -slope * jnp.abs(qi * tq - ki * tk).astype(jnp.float32)
    ka = ka_ref[0, 0]
    vt = vt_ref[0, 0]

    def step(diag):
        for c in range(2):
            s = _dot_nt(ka, qa_scr[c, var])
            if diag:
                s = s + dabs_ref[...] * slope
            m_old = m_scr[c]
            m_new = jnp.maximum(m_old, jnp.max(s, axis=0, keepdims=True) + off)
            alpha = jnp.exp(m_old - m_new)
            p = jnp.exp(s - (m_new - off)).astype(jnp.bfloat16)
            acc_scr[c] = alpha * acc_scr[c] + _dot(vt, p)
            m_scr[c] = m_new

    @pl.when(ki == qi)
    def _():
        step(True)

    @pl.when(ki != qi)
    def _():
        step(False)

    @pl.when(ki == nk - 1)
    def _():
        lam = (jnp.exp(jnp.sum(lq1_ref[...] * lk1_ref[...], axis=-1, keepdims=True))
               - jnp.exp(jnp.sum(lq2_ref[...] * lk2_ref[...], axis=-1, keepdims=True)) + LAM_INIT)
        a1, a2 = acc_scr[0], acc_scr[1]
        o = (a1[:DIFF_VDIM] / a1[DIFF_VDIM:DIFF_VDIM + 1]
             - lam * (a2[:DIFF_VDIM] / a2[DIFF_VDIM:DIFF_VDIM + 1]))
        o = o * lax.rsqrt(jnp.mean(o * o, axis=0, keepdims=True) + LN_EPS) * g_ref[...]
        o_ref[0] = (o * (1.0 - LAM_INIT)).T.astype(o_ref.dtype)


def _diff_attention(qd, ka, vt, qpos, dabs, lq1, lk1, lq2, lk2, g_col):
    B, H, S, _ = qd.shape
    tq, tk = DIFF_TQ, DIFF_TK
    slopes = tuple(2.0 ** (-8.0 * (i + 1) / H) for i in range(H))
    small = lambda a: pl.BlockSpec(a.shape, lambda b, h, qi, ki: (0, 0))
    return pl.pallas_call(
        functools.partial(_diff_kernel, slopes=slopes),
        grid=(B, H, S // tq, S // tk),
        in_specs=[
            pl.BlockSpec((1, 1, tq, LANES), lambda b, h, qi, ki: (b, h, qi, 0)),
            pl.BlockSpec((1, 1, tk, 2 * LANES), lambda b, h, qi, ki: (b, h, ki, 0)),
            pl.BlockSpec((1, 1, V_ROWS, tk), lambda b, h, qi, ki: (b, h, 0, ki)),
            small(qpos), small(dabs), small(lq1), small(lk1), small(lq2), small(lk2), small(g_col),
        ],
        out_specs=pl.BlockSpec((1, tq, LANES), lambda b, h, qi, ki: (b, qi, h)),
        out_shape=jax.ShapeDtypeStruct((B, S, H * LANES), jnp.bfloat16),
        scratch_shapes=[
            pltpu.VMEM((2, 3, tq, 2 * LANES), jnp.bfloat16),
            pltpu.VMEM((2, 1, tq), jnp.float32),
            pltpu.VMEM((2, V_ROWS, tq), jnp.float32),
        ],
        compiler_params=_cparams(("parallel", "parallel", "parallel", "arbitrary")),
        name="diff_attention",
    )(qd, ka, vt, qpos, dabs, lq1, lk1, lq2, lk2, g_col)


def _position_columns():
    def cols(n, as_key):
        p = np.arange(n)
        hi, lo = (p // POS_SPLIT) * POS_SPLIT, p % POS_SPLIT
        one = np.ones(n)
        c = np.zeros((n, LANES), np.float32)
        if as_key:
            c[:, 0], c[:, 1], c[:, 2], c[:, 3] = hi, lo, one, one
        else:
            c[:, 0], c[:, 1], c[:, 2], c[:, 3] = one, one, -hi, -lo
        return jnp.asarray(c, jnp.bfloat16)

    di = np.arange(DIFF_TQ)[:, None]
    dj = np.arange(DIFF_TK)[None, :]
    dabs = jnp.asarray(-np.abs(di - dj), jnp.float32)
    return cols(DIFF_TK, True), cols(DIFF_TQ, False), dabs


def _post_attn_kernel(oa_ref, od_ref, x_ref, wo_ref, g_ref, b_ref, wr_ref, br_ref, tri_ref,
                      x1_ref, x1b_ref, idx_ref, gate_ref, rank_ref, cnt_ref, carry_scr):
    step = pl.program_id(0)

    @pl.when(step == 0)
    def _():
        carry_scr[...] = jnp.zeros(carry_scr.shape, jnp.float32)

    mix = _dot(oa_ref[...], wo_ref[:DIL_WIDTH, :]) + _dot(od_ref[...], wo_ref[DIL_WIDTH:, :])
    x1 = _layer_norm(DEEPNORM_ALPHA * x_ref[...] + mix, g_ref[...], b_ref[...])
    x1_ref[...] = x1
    x1b_ref[...] = x1.astype(jnp.bfloat16)

    tm = x1.shape[0]
    lg = _dot_nt(wr_ref[...], x1, precision=lax.Precision.HIGHEST) + br_ref[...]
    eidx = lax.broadcasted_iota(jnp.int32, (N_EXPERTS, tm), 0)
    vals, sels = [], []
    for k in range(TOP_K):
        mx = jnp.max(lg, axis=0, keepdims=True)
        idx = jnp.min(jnp.where(lg == mx, eidx, N_EXPERTS), axis=0, keepdims=True)
        sel = eidx == idx
        vals.append(mx)
        sels.append(sel)
        idx_ref[k:k + 1, :] = idx
        lg = jnp.where(sel, -jnp.inf, lg)
    ex = [jnp.exp(v - vals[0]) for v in vals]
    den = ex[0] + ex[1] + ex[2] + ex[3]
    for k in range(TOP_K):
        gate_ref[k:k + 1, :] = ex[k] / den

    chosen = (sels[0] | sels[1] | sels[2] | sels[3])
    onehot = jnp.where(chosen, 1.0, 0.0)
    before = _dot(onehot.astype(jnp.bfloat16), tri_ref[...]) + carry_scr[...]
    for k in range(TOP_K):
        rank_ref[k:k + 1, :] = jnp.sum(jnp.where(sels[k], before, 0.0), axis=0,
                                       keepdims=True).astype(jnp.int32)
    carry_scr[...] = carry_scr[...] + jnp.sum(onehot, axis=1, keepdims=True)
    pad = jnp.zeros((8 - TOP_K, tm), jnp.int32)
    idx_ref[TOP_K:, :] = pad
    rank_ref[TOP_K:, :] = pad
    gate_ref[TOP_K:, :] = pad.astype(jnp.float32)
    cnt_ref[...] = jnp.broadcast_to(carry_scr[...], cnt_ref.shape).astype(jnp.int32)


def _post_attn(oa, od, x, wo_bf, g, b, wr_t, br, tri):
    T, D = x.shape
    tm = ROW_TILE
    row = lambda w: pl.BlockSpec((tm, w), lambda i: (i, 0))
    full = lambda a: pl.BlockSpec(a.shape, lambda i: (0, 0))
    col = pl.BlockSpec((8, tm), lambda i: (0, i))
    return pl.pallas_call(
        _post_attn_kernel,
        grid=(T // tm,),
        in_specs=[row(DIL_WIDTH), row(DIFF_WIDTH), row(D), full(wo_bf), full(g), full(b), full(wr_t),
                  full(br), full(tri)],
        out_specs=[row(D), row(D), col, col, col, pl.BlockSpec((N_EXPERTS, LANES), lambda i: (0, 0))],
        out_shape=[
            jax.ShapeDtypeStruct((T, D), jnp.float32),
            jax.ShapeDtypeStruct((T, D), jnp.bfloat16),
            jax.ShapeDtypeStruct((8, T), jnp.int32),
            jax.ShapeDtypeStruct((8, T), jnp.float32),
            jax.ShapeDtypeStruct((8, T), jnp.int32),
            jax.ShapeDtypeStruct((N_EXPERTS, LANES), jnp.int32),
        ],
        scratch_shapes=[pltpu.VMEM((N_EXPERTS, 1), jnp.float32)],
        compiler_params=_cparams(("arbitrary",)),
        name="post_attn_router",
    )(oa, od, x, wo_bf, g, b, wr_t, br, tri)


def _expert_kernel(tile_e_ref, n_used_ref, xs_ref, wu_ref, bu_ref, wd_ref, bd_ref, y_ref):
    i = pl.program_id(0)

    @pl.when(i < n_used_ref[0])
    def _():
        hu = _dot(xs_ref[...], wu_ref[0]) + bu_ref[0]
        g = jnp.minimum(hu[:, :D_FF], SWIGLU_LIMIT)
        u = jnp.clip(hu[:, D_FF:], -SWIGLU_LIMIT, SWIGLU_LIMIT)
        act = g * (1.0 / (1.0 + jnp.exp(-SWIGLU_ALPHA * g))) * (u + 1.0)
        y_ref[...] = _dot(act.astype(jnp.bfloat16), wd_ref[0]) + bd_ref[0]

    @pl.when(i >= n_used_ref[0])
    def _():
        y_ref[...] = jnp.zeros(y_ref.shape, y_ref.dtype)


def _experts(tile_e, n_used, xs, wu_bf, bu, wd_bf, bd):
    n_rows, D = xs.shape
    tm = MOE_TILE
    grid_spec = pltpu.PrefetchScalarGridSpec(
        num_scalar_prefetch=2,
        grid=(n_rows // tm,),
        in_specs=[
            pl.BlockSpec((tm, D), lambda i, te, nu: (i, 0)),
            pl.BlockSpec((1, D, 2 * D_FF), lambda i, te, nu: (te[i], 0, 0)),
            pl.BlockSpec((1, 1, 2 * D_FF), lambda i, te, nu: (te[i], 0, 0)),
            pl.BlockSpec((1, D_FF, D), lambda i, te, nu: (te[i], 0, 0)),
            pl.BlockSpec((1, 1, D), lambda i, te, nu: (te[i], 0, 0)),
        ],
        out_specs=pl.BlockSpec((tm, D), lambda i, te, nu: (i, 0)),
    )
    return pl.pallas_call(
        _expert_kernel,
        grid_spec=grid_spec,
        out_shape=jax.ShapeDtypeStruct((n_rows, D), jnp.float32),
        compiler_params=_cparams(("arbitrary",)),
        name="experts",
    )(tile_e, n_used, xs, wu_bf, bu, wd_bf, bd)


def _combine_kernel(x1_ref, yg_ref, gate_ref, g_ref, b_ref, o_ref):
    ffn = yg_ref[0] * gate_ref[:, 0:1]
    for k in range(1, TOP_K):
        ffn = ffn + yg_ref[k] * gate_ref[:, k:k + 1]
    o_ref[...] = _layer_norm(DEEPNORM_ALPHA * x1_ref[...] + ffn, g_ref[...], b_ref[...])


def _combine(x1, yg, gates_t, g, b):
    T, D = x1.shape
    tm = ROW_TILE
    full = lambda a: pl.BlockSpec(a.shape, lambda i: (0, 0))
    return pl.pallas_call(
        _combine_kernel,
        grid=(T // tm,),
        in_specs=[
            pl.BlockSpec((tm, D), lambda i: (i, 0)),
            pl.BlockSpec((TOP_K, tm, D), lambda i: (0, i, 0)),
            pl.BlockSpec((tm, 8), lambda i: (i, 0)),
            full(g), full(b),
        ],
        out_specs=pl.BlockSpec((tm, D), lambda i: (i, 0)),
        out_shape=jax.ShapeDtypeStruct((T, D), jnp.float32),
        compiler_params=_cparams(("parallel",)),
        name="combine_ln2",
    )(x1, yg, gates_t, g, b)


def kernel(x, w_in, w_out, lambda_q1, lambda_k1, lambda_q2, lambda_k2, diff_norm_g, ln1_g, ln1_b,
           w_router, b_router, w_up, b_up, w_down, b_down, ln2_g, ln2_b):
    B, S, D = x.shape
    T = B * S
    assert w_in.shape[0] == 1, "single layer"
    assert S % DIFF_TQ == 0 and DIFF_TK == DIFF_TQ and DIFF_TK % ROW_TILE == 0 and S % (16 * DIL_TQ) == 0

    na = 3 * DIL_WIDTH
    colscale = np.ones((3 * D,), np.float32)
    colscale[:DIL_WIDTH] = HEAD_DIM ** -0.5
    colscale[na:na + DIFF_WIDTH] = HEAD_DIM ** -0.5
    w_in_bf = (w_in[0] * colscale).astype(jnp.bfloat16)
    nv = na + 2 * DIFF_WIDTH

    kpos, qpos, dabs = _position_columns()
    pa, qd, ka, vt = _in_proj(x, w_in_bf[:, :nv], w_in_bf[:, nv:].T, kpos)

    stats = None
    for n, (_, dil) in enumerate(DIL_PATTERNS):
        final = n == len(DIL_PATTERNS) - 1
        stats = _dilated_stage(pa, _dilated_bias(dil), dil, stats, final)
    oa = stats

    r2 = lambda a: a.reshape(1, -1).astype(jnp.float32)
    od = _diff_attention(qd, ka, vt, qpos, dabs, r2(lambda_q1[0]), r2(lambda_k1[0]), r2(lambda_q2[0]),
                         r2(lambda_k2[0]), diff_norm_g[0].reshape(-1, 1).astype(jnp.float32))

    tri = jnp.asarray(np.triu(np.ones((ROW_TILE, ROW_TILE), np.float32), k=1), jnp.bfloat16)
    x1, x1b, top_idx, gates, rank, counts = _post_attn(
        oa.reshape(T, DIL_WIDTH), od.reshape(T, DIFF_WIDTH), x.reshape(T, D), w_out[0].astype(jnp.bfloat16),
        r2(ln1_g[0]), r2(ln1_b[0]), w_router[0].T, b_router[0].reshape(N_EXPERTS, 1), tri)

    counts = counts[:, 0]
    pcounts = ((counts + MOE_TILE - 1) // MOE_TILE) * MOE_TILE
    pends = jnp.cumsum(pcounts)
    pstarts = pends - pcounts
    n_rows = T * TOP_K + N_EXPERTS * MOE_TILE
    n_tiles = n_rows // MOE_TILE
    dest = pstarts[top_idx[:TOP_K]] + rank[:TOP_K]
    tile_e = jnp.clip(jnp.searchsorted(pends, jnp.arange(n_tiles) * MOE_TILE, side='right'),
                      0, N_EXPERTS - 1).astype(jnp.int32)
    n_used = (pends[-1] // MOE_TILE).astype(jnp.int32).reshape(1)

    row_tok = jnp.zeros((n_rows,), jnp.int32).at[dest.reshape(-1)].set(
        jnp.tile(jnp.arange(T, dtype=jnp.int32), TOP_K))
    xs = x1b[row_tok]

    y = _experts(tile_e, n_used, xs, w_up[0].astype(jnp.bfloat16), b_up[0][:, None, :],
                 w_down[0].astype(jnp.bfloat16), b_down[0][:, None, :])
    yg = y[dest]

    out = _combine(x1, yg, gates.T, r2(ln2_g[0]), r2(ln2_b[0]))
    return out.reshape(B, S, D)
```

```python
import functools
import math

import jax
import jax.numpy as jnp
import numpy as np
from jax import lax
from jax.experimental import pallas as pl
from jax.experimental.pallas import tpu as pltpu

D_MODEL = 1024
HEAD_DIM = 64
DIL_WIDTH = 512
N_HEADS_DIL = 8
DIL_PATTERNS = ((128, 1), (512, 4), (2048, 16))
DIL_SIDE = 64
DIFF_WIDTH = 512
N_HEADS_DIFF = 4
DIFF_VDIM = 2 * HEAD_DIM
N_EXPERTS = 32
TOP_K = 4
D_FF = D_MODEL
SWIGLU_ALPHA = 1.702
SWIGLU_LIMIT = 7.0
DEEPNORM_ALPHA = 2.0 ** 0.25
LN_EPS = 1e-5
NEG_INF = -1e30
LAM_INIT = 0.8 - 0.6 * math.exp(-0.3 * 0)
LOG2E = math.log2(math.e)

LANES = 128
VMEM_LIMIT = 56 * 1024 * 1024

ROW_TILE = 512
DIFF_TQ = 1024
DIFF_TK = 1024
V_ROWS = DIFF_VDIM + 16
EXP2_HEADROOM = 4.0
DIL_TQ = 128
MOE_TILE = 512
POS_SPLIT = 16


def _cparams(sem):
    return pltpu.CompilerParams(dimension_semantics=sem, vmem_limit_bytes=VMEM_LIMIT)


def _dot_nt(a, b, **kw):
    return lax.dot_general(a, b, (((1,), (1,)), ((), ())), preferred_element_type=jnp.float32, **kw)


def _dot(a, b, **kw):
    return jnp.dot(a, b, preferred_element_type=jnp.float32, **kw)


def _layer_norm(y, g, b):
    mu = jnp.mean(y, axis=-1, keepdims=True)
    yc = y - mu
    var = jnp.mean(yc * yc, axis=-1, keepdims=True)
    return yc * lax.rsqrt(var + LN_EPS) * g + b


def _in_proj_kernel(x_ref, w_ref, wvt_ref, kpos_ref, pa_ref, qd_ref, ka_ref, vt_ref):
    xb = x_ref[0].astype(jnp.bfloat16)
    na = 3 * DIL_WIDTH
    tm = xb.shape[0]
    pa_ref[0] = _dot(xb, w_ref[:, :na]).astype(jnp.bfloat16)
    q = _dot(xb, w_ref[:, na:na + DIFF_WIDTH]).astype(jnp.bfloat16)
    k = _dot(xb, w_ref[:, na + DIFF_WIDTH:]).astype(jnp.bfloat16)
    vt = _dot_nt(wvt_ref[...], xb).astype(jnp.bfloat16)
    ones = jnp.ones((V_ROWS - DIFF_VDIM, tm), jnp.bfloat16)
    for h in range(N_HEADS_DIFF):
        sl = slice(h * LANES, (h + 1) * LANES)
        qd_ref[0, h] = q[:, sl]
        ka_ref[0, h, :, :LANES] = k[:, sl]
        ka_ref[0, h, :, LANES:] = kpos_ref[h]
        vt_ref[0, h, :DIFF_VDIM, :] = vt[sl, :]
        vt_ref[0, h, DIFF_VDIM:, :] = ones


def _in_proj(x, w_bf, wvt_bf, kpos):
    B, S, D = x.shape
    tm = ROW_TILE
    na = 3 * DIL_WIDTH
    return pl.pallas_call(
        _in_proj_kernel,
        grid=(B, S // tm),
        in_specs=[
            pl.BlockSpec((1, tm, D), lambda b, i: (b, i, 0)),
            pl.BlockSpec(w_bf.shape, lambda b, i: (0, 0)),
            pl.BlockSpec(wvt_bf.shape, lambda b, i: (0, 0)),
            pl.BlockSpec((N_HEADS_DIFF, tm, LANES), lambda b, i: (0, i % (DIFF_TK // tm), 0)),
        ],
        out_specs=[
            pl.BlockSpec((1, tm, na), lambda b, i: (b, i, 0)),
            pl.BlockSpec((1, N_HEADS_DIFF, tm, LANES), lambda b, i: (b, 0, i, 0)),
            pl.BlockSpec((1, N_HEADS_DIFF, tm, 2 * LANES), lambda b, i: (b, 0, i, 0)),
            pl.BlockSpec((1, N_HEADS_DIFF, V_ROWS, tm), lambda b, i: (b, 0, 0, i)),
        ],
        out_shape=[
            jax.ShapeDtypeStruct((B, S, na), jnp.bfloat16),
            jax.ShapeDtypeStruct((B, N_HEADS_DIFF, S, LANES), jnp.bfloat16),
            jax.ShapeDtypeStruct((B, N_HEADS_DIFF, S, 2 * LANES), jnp.bfloat16),
            jax.ShapeDtypeStruct((B, N_HEADS_DIFF, V_ROWS, S), jnp.bfloat16),
        ],
        compiler_params=_cparams(("parallel", "parallel")),
        name="in_proj",
    )(x, w_bf, wvt_bf, kpos)


def _dilated_kernel(*refs, n_blocks, seq_len, has_prev, final):
    (q_ref, kp_ref, kc_ref, kn_ref, vp_ref, vc_ref, vn_ref, bias_ref), refs = refs[:8], refs[8:]
    if has_prev:
        (acc0_ref, m0_ref, l0_ref), refs = refs[:3], refs[3:]
    if final:
        (o_ref,) = refs
    else:
        acc_ref, m_ref, l_ref = refs
    tq = q_ref.shape[1]
    i = pl.program_id(2)
    q = q_ref[0]
    kcat = jnp.concatenate([kp_ref[0], kc_ref[0], kn_ref[0]], axis=0)
    vcat = jnp.concatenate([vp_ref[0], vc_ref[0], vn_ref[0]], axis=0)
    key_pos = (i - 1) * tq + lax.broadcasted_iota(jnp.int32, (1, 3 * tq), 1)
    edge = jnp.where((key_pos >= 0) & (key_pos < seq_len), 0.0, NEG_INF)
    low_half = lax.broadcasted_iota(jnp.int32, (tq, LANES), 1) < HEAD_DIM
    for pair in range(N_HEADS_DIL // 2):
        sl = slice(pair * LANES, (pair + 1) * LANES)
        qp, kp, vp = q[:, sl], kcat[:, sl], vcat[:, sl]
        res = []
        for hh in range(2):
            qm = jnp.where(low_half == (hh == 0), qp, jnp.zeros_like(qp))
            s = _dot_nt(qm, kp) + bias_ref[2 * pair + hh] + edge
            m = jnp.max(s, axis=-1, keepdims=True)
            p = jnp.exp(s - m)
            l = jnp.sum(p, axis=-1, keepdims=True)
            a = _dot(p.astype(jnp.bfloat16), vp)
            res.append((a, m, l))
        acc = jnp.where(low_half, res[0][0], res[1][0])
        m = jnp.where(low_half, res[0][1], res[1][1])
        l = jnp.where(low_half, res[0][2], res[1][2])
        if has_prev:
            m0 = m0_ref[0, :, sl]
            mm = jnp.maximum(m0, m)
            w0 = jnp.exp(m0 - mm)
            w1 = jnp.exp(m - mm)
            acc = acc0_ref[0, :, sl] * w0 + acc * w1
            l = l0_ref[0, :, sl] * w0 + l * w1
            m = mm
        if final:
            o_ref[0, :, sl] = (acc / l).astype(o_ref.dtype)
        else:
            acc_ref[0, :, sl] = acc
            m_ref[0, :, sl] = m
            l_ref[0, :, sl] = l


def _dilated_stage(pa, bias, dilation, prev, final):
    B, S, _ = pa.shape
    d = dilation
    L = S // d
    tq = DIL_TQ
    nb = L // tq
    w = DIL_WIDTH
    pav = pa.reshape(B, L, d * 3 * w)

    def spec(col, shift):
        def imap(b, r, i):
            return (b, jnp.clip(i + shift, 0, nb - 1), r * 3 + col)
        return pl.BlockSpec((1, tq, w), imap)

    stat_spec = pl.BlockSpec((1, tq, w), lambda b, r, i: (b, i, r))
    in_specs = [spec(0, 0), spec(1, -1), spec(1, 0), spec(1, 1), spec(2, -1), spec(2, 0), spec(2, 1),
                pl.BlockSpec(bias.shape, lambda b, r, i: (0, 0, 0))]
    args = [pav, pav, pav, pav, pav, pav, pav, bias]
    if prev is not None:
        in_specs += [stat_spec] * 3
        args += [a.reshape(B, L, d * w) for a in prev]
    if final:
        out_specs = stat_spec
        out_shape = jax.ShapeDtypeStruct((B, L, d * w), jnp.bfloat16)
    else:
        out_specs = [stat_spec] * 3
        out_shape = [jax.ShapeDtypeStruct((B, L, d * w), jnp.float32)] * 3
    out = pl.pallas_call(
        functools.partial(_dilated_kernel, n_blocks=nb, seq_len=L, has_prev=prev is not None, final=final),
        grid=(B, d, nb),
        in_specs=in_specs,
        out_specs=out_specs,
        out_shape=out_shape,
        compiler_params=_cparams(("parallel", "parallel", "parallel")),
        name=f"dilated_d{d}",
    )(*args)
    if final:
        return out.reshape(B, S, w)
    return tuple(o.reshape(B, S, w) for o in out)


def _dilated_bias(dilation):
    tq = DIL_TQ
    slopes = np.asarray([2.0 ** (-8.0 * (i + 1) / N_HEADS_DIL) for i in range(N_HEADS_DIL)], np.float32)
    rel = (np.arange(3 * tq)[None, :] - tq) - np.arange(tq)[:, None]
    band = np.abs(rel) <= DIL_SIDE
    pen = -(slopes * dilation)[:, None, None] * np.abs(rel).astype(np.float32)[None]
    return jnp.asarray(np.where(band[None], pen, np.float32(NEG_INF)), jnp.float32)


def _diff_kernel(q_ref, ka_ref, vt_ref, qpos_ref, dabs_ref, lq1_ref, lk1_ref, lq2_ref, lk2_ref, g_ref,
                 o_ref, qa_scr, m_scr, acc_scr, tmp_scr, *, slopes):
    h = pl.program_id(1)
    qi = pl.program_id(2)
    ki = pl.program_id(3)
    nk = pl.num_programs(3)
    tq = q_ref.shape[2]
    tk = ka_ref.shape[2]

    slope = jnp.float32(slopes[0])
    for hh in range(1, N_HEADS_DIFF):
        slope = jnp.where(h == hh, jnp.float32(slopes[hh]), slope)

    @pl.when(ki == 0)
    def _():
        q = q_ref[0, 0]
        low_half = lax.broadcasted_iota(jnp.int32, (tq, LANES), 1) < HEAD_DIM
        zero = jnp.zeros_like(q)
        qpos = qpos_ref[0]
        for c in range(2):
            qc = jnp.where(low_half == (c == 0), q, zero)
            for var, pos in enumerate((qpos, -qpos, jnp.zeros_like(qpos))):
                qa_scr[c, var, :, :LANES] = qc
                qa_scr[c, var, :, LANES:] = pos
        m_scr[...] = jnp.full(m_scr.shape, NEG_INF, jnp.float32)
        acc_scr[...] = jnp.zeros(acc_scr.shape, jnp.float32)

    kt = lax.rem(qi + ki, nk)
    var = jnp.where(kt < qi, 0, jnp.where(kt > qi, 1, 2))
    off = -slope * jnp.abs(qi * tq - kt * tk).astype(jnp.float32)
    ka = ka_ref[0, 0]
    vt = vt_ref[0, 0]


    def restabilise(diag):
        ss = []
        for c in range(2):
            s = _dot_nt(ka, qa_scr[c, var])
            if diag:
                s = s + dabs_ref[...] * slope
            ss.append(s)
        for c in range(2):
            s = ss[c]
            m_old = m_scr[c]
            m_new = jnp.maximum(m_old, jnp.max(s, axis=0, keepdims=True) + off)
            alpha = jnp.exp2(m_old - m_new)
            p = jnp.exp2((s - (m_new - off)).astype(jnp.bfloat16))
            acc_scr[c] = alpha * acc_scr[c] + _dot(vt, p)
            m_scr[c] = m_new

    @pl.when(ki == 0)
    def _():
        restabilise(True)

    @pl.when(ki != 0)
    def _():
        excess = None
        for c in range(2):
            s = _dot_nt(ka, qa_scr[c, var])
            m_eff = m_scr[c] - off
            over = jnp.max(jnp.max(s, axis=0, keepdims=True) - m_eff)
            excess = over if excess is None else jnp.maximum(excess, over)
            p = jnp.exp2((s - m_eff).astype(jnp.bfloat16))
            tmp_scr[c] = acc_scr[c] + _dot(vt, p)

        @pl.when(excess <= EXP2_HEADROOM)
        def _():
            acc_scr[...] = tmp_scr[...]

        @pl.when(excess > EXP2_HEADROOM)
        def _():
            restabilise(False)

    @pl.when(ki == nk - 1)
    def _():
        lam = (jnp.exp(jnp.sum(lq1_ref[...] * lk1_ref[...], axis=-1, keepdims=True))
               - jnp.exp(jnp.sum(lq2_ref[...] * lk2_ref[...], axis=-1, keepdims=True)) + LAM_INIT)
        a1, a2 = acc_scr[0], acc_scr[1]
        o = (a1[:DIFF_VDIM] / a1[DIFF_VDIM:DIFF_VDIM + 1]
             - lam * (a2[:DIFF_VDIM] / a2[DIFF_VDIM:DIFF_VDIM + 1]))
        o = o * lax.rsqrt(jnp.mean(o * o, axis=0, keepdims=True) + LN_EPS) * g_ref[...]
        o_ref[0] = (o * (1.0 - LAM_INIT)).T.astype(o_ref.dtype)


def _diff_attention(qd, ka, vt, qpos, dabs, lq1, lk1, lq2, lk2, g_col):
    B, H, S, _ = qd.shape
    tq, tk = DIFF_TQ, DIFF_TK
    slopes = tuple(float(s) for s in _diff_slopes_log2())
    nk = S // tk
    small = lambda a: pl.BlockSpec(a.shape, lambda b, h, qi, ki: (0, 0))
    return pl.pallas_call(
        functools.partial(_diff_kernel, slopes=slopes),
        grid=(B, H, S // tq, S // tk),
        in_specs=[
            pl.BlockSpec((1, 1, tq, LANES), lambda b, h, qi, ki: (b, h, qi, 0)),
            pl.BlockSpec((1, 1, tk, 2 * LANES), lambda b, h, qi, ki: (b, h, (qi + ki) % nk, 0)),
            pl.BlockSpec((1, 1, V_ROWS, tk), lambda b, h, qi, ki: (b, h, 0, (qi + ki) % nk)),
            pl.BlockSpec((1, tq, LANES), lambda b, h, qi, ki: (h, 0, 0)),
            small(dabs), small(lq1), small(lk1), small(lq2), small(lk2), small(g_col),
        ],
        out_specs=pl.BlockSpec((1, tq, LANES), lambda b, h, qi, ki: (b, qi, h)),
        out_shape=jax.ShapeDtypeStruct((B, S, H * LANES), jnp.bfloat16),
        scratch_shapes=[
            pltpu.VMEM((2, 3, tq, 2 * LANES), jnp.bfloat16),
            pltpu.VMEM((2, 1, tq), jnp.float32),
            pltpu.VMEM((2, V_ROWS, tq), jnp.float32),
            pltpu.VMEM((2, V_ROWS, tq), jnp.float32),
        ],
        compiler_params=_cparams(("parallel", "parallel", "parallel", "arbitrary")),
        name="diff_attention",
    )(qd, ka, vt, qpos, dabs, lq1, lk1, lq2, lk2, g_col)


def _diff_slopes_log2():
    return np.asarray([2.0 ** (-8.0 * (i + 1) / N_HEADS_DIFF) for i in range(N_HEADS_DIFF)],
                      np.float32) * np.float32(LOG2E)


def _position_columns():
    def bf16_round(v):
        u = np.asarray(v, np.float32).view(np.uint32)
        return ((u + 0x7FFF + ((u >> 16) & 1)) & np.uint32(0xFFFF0000)).view(np.float32)

    kpos = np.zeros((N_HEADS_DIFF, DIFF_TK, LANES), np.float32)
    qpos = np.zeros((N_HEADS_DIFF, DIFF_TQ, LANES), np.float32)
    pk, pq = np.arange(DIFF_TK), np.arange(DIFF_TQ)
    for h, a in enumerate(_diff_slopes_log2()):
        rest = np.float32(a)
        for n in range(3):
            a_n = np.float32(bf16_round(rest))
            rest = np.float32(rest - a_n)
            c = 4 * n
            kpos[h, :, c + 0] = (pk // POS_SPLIT) * POS_SPLIT
            kpos[h, :, c + 1] = pk % POS_SPLIT
            kpos[h, :, c + 2] = a_n
            kpos[h, :, c + 3] = a_n
            qpos[h, :, c + 0] = a_n
            qpos[h, :, c + 1] = a_n
            qpos[h, :, c + 2] = -((pq // POS_SPLIT) * POS_SPLIT)
            qpos[h, :, c + 3] = -(pq % POS_SPLIT)
    dabs = -np.abs(pq[:, None] - pk[None, :]).astype(np.float32)
    return jnp.asarray(kpos, jnp.bfloat16), jnp.asarray(qpos, jnp.bfloat16), jnp.asarray(dabs)


def _post_attn_kernel(oa_ref, od_ref, x_ref, wo_ref, g_ref, b_ref, wr_ref, br_ref, tri_ref,
                      x1_ref, x1b_ref, idx_ref, gate_ref, rank_ref, cnt_ref, carry_scr):
    step = pl.program_id(0)

    @pl.when(step == 0)
    def _():
        carry_scr[...] = jnp.zeros(carry_scr.shape, jnp.float32)

    mix = _dot(oa_ref[...], wo_ref[:DIL_WIDTH, :]) + _dot(od_ref[...], wo_ref[DIL_WIDTH:, :])
    x1 = _layer_norm(DEEPNORM_ALPHA * x_ref[...] + mix, g_ref[...], b_ref[...])
    x1_ref[...] = x1
    x1b_ref[...] = x1.astype(jnp.bfloat16)

    tm = x1.shape[0]
    lg = _dot_nt(wr_ref[...], x1, precision=lax.Precision.HIGHEST) + br_ref[...]
    eidx = lax.broadcasted_iota(jnp.int32, (N_EXPERTS, tm), 0)
    vals, sels = [], []
    for k in range(TOP_K):
        mx = jnp.max(lg, axis=0, keepdims=True)
        idx = jnp.min(jnp.where(lg == mx, eidx, N_EXPERTS), axis=0, keepdims=True)
        sel = eidx == idx
        vals.append(mx)
        sels.append(sel)
        idx_ref[k:k + 1, :] = idx
        lg = jnp.where(sel, -jnp.inf, lg)
    ex = [jnp.exp(v - vals[0]) for v in vals]
    den = ex[0] + ex[1] + ex[2] + ex[3]
    for k in range(TOP_K):
        gate_ref[k:k + 1, :] = ex[k] / den

    chosen = (sels[0] | sels[1] | sels[2] | sels[3])
    onehot = jnp.where(chosen, 1.0, 0.0)
    before = _dot(onehot.astype(jnp.bfloat16), tri_ref[...]) + carry_scr[...]
    for k in range(TOP_K):
        rank_ref[k:k + 1, :] = jnp.sum(jnp.where(sels[k], before, 0.0), axis=0,
                                       keepdims=True).astype(jnp.int32)
    carry_scr[...] = carry_scr[...] + jnp.sum(onehot, axis=1, keepdims=True)
    pad = jnp.zeros((8 - TOP_K, tm), jnp.int32)
    idx_ref[TOP_K:, :] = pad
    rank_ref[TOP_K:, :] = pad
    gate_ref[TOP_K:, :] = pad.astype(jnp.float32)
    cnt_ref[...] = jnp.broadcast_to(carry_scr[...], cnt_ref.shape).astype(jnp.int32)


def _post_attn(oa, od, x, wo_bf, g, b, wr_t, br, tri):
    T, D = x.shape
    tm = ROW_TILE
    row = lambda w: pl.BlockSpec((tm, w), lambda i: (i, 0))
    full = lambda a: pl.BlockSpec(a.shape, lambda i: (0, 0))
    col = pl.BlockSpec((8, tm), lambda i: (0, i))
    return pl.pallas_call(
        _post_attn_kernel,
        grid=(T // tm,),
        in_specs=[row(DIL_WIDTH), row(DIFF_WIDTH), row(D), full(wo_bf), full(g), full(b), full(wr_t),
                  full(br), full(tri)],
        out_specs=[row(D), row(D), col, col, col, pl.BlockSpec((N_EXPERTS, LANES), lambda i: (0, 0))],
        out_shape=[
            jax.ShapeDtypeStruct((T, D), jnp.float32),
            jax.ShapeDtypeStruct((T, D), jnp.bfloat16),
            jax.ShapeDtypeStruct((8, T), jnp.int32),
            jax.ShapeDtypeStruct((8, T), jnp.float32),
            jax.ShapeDtypeStruct((8, T), jnp.int32),
            jax.ShapeDtypeStruct((N_EXPERTS, LANES), jnp.int32),
        ],
        scratch_shapes=[pltpu.VMEM((N_EXPERTS, 1), jnp.float32)],
        compiler_params=_cparams(("arbitrary",)),
        name="post_attn_router",
    )(oa, od, x, wo_bf, g, b, wr_t, br, tri)


def _expert_kernel(tile_e_ref, n_used_ref, xs_ref, wu_ref, bu_ref, wd_ref, bd_ref, y_ref):
    i = pl.program_id(0)

    @pl.when(i < n_used_ref[0])
    def _():
        hu = _dot(xs_ref[...], wu_ref[0]) + bu_ref[0]
        g = jnp.minimum(hu[:, :D_FF], SWIGLU_LIMIT)
        u = jnp.clip(hu[:, D_FF:], -SWIGLU_LIMIT, SWIGLU_LIMIT)
        act = g * (1.0 / (1.0 + jnp.exp(-SWIGLU_ALPHA * g))) * (u + 1.0)
        y_ref[...] = _dot(act.astype(jnp.bfloat16), wd_ref[0]) + bd_ref[0]

    @pl.when(i >= n_used_ref[0])
    def _():
        y_ref[...] = jnp.zeros(y_ref.shape, y_ref.dtype)


def _experts(tile_e, n_used, xs, wu_bf, bu, wd_bf, bd):
    n_rows, D = xs.shape
    tm = MOE_TILE
    grid_spec = pltpu.PrefetchScalarGridSpec(
        num_scalar_prefetch=2,
        grid=(n_rows // tm,),
        in_specs=[
            pl.BlockSpec((tm, D), lambda i, te, nu: (i, 0)),
            pl.BlockSpec((1, D, 2 * D_FF), lambda i, te, nu: (te[i], 0, 0)),
            pl.BlockSpec((1, 1, 2 * D_FF), lambda i, te, nu: (te[i], 0, 0)),
            pl.BlockSpec((1, D_FF, D), lambda i, te, nu: (te[i], 0, 0)),
            pl.BlockSpec((1, 1, D), lambda i, te, nu: (te[i], 0, 0)),
        ],
        out_specs=pl.BlockSpec((tm, D), lambda i, te, nu: (i, 0)),
    )
    return pl.pallas_call(
        _expert_kernel,
        grid_spec=grid_spec,
        out_shape=jax.ShapeDtypeStruct((n_rows, D), jnp.float32),
        compiler_params=_cparams(("arbitrary",)),
        name="experts",
    )(tile_e, n_used, xs, wu_bf, bu, wd_bf, bd)


def _combine_kernel(x1_ref, yg_ref, gate_ref, g_ref, b_ref, o_ref):
    ffn = yg_ref[0] * gate_ref[:, 0:1]
    for k in range(1, TOP_K):
        ffn = ffn + yg_ref[k] * gate_ref[:, k:k + 1]
    o_ref[...] = _layer_norm(DEEPNORM_ALPHA * x1_ref[...] + ffn, g_ref[...], b_ref[...])


def _combine(x1, yg, gates_t, g, b):
    T, D = x1.shape
    tm = ROW_TILE
    full = lambda a: pl.BlockSpec(a.shape, lambda i: (0, 0))
    return pl.pallas_call(
        _combine_kernel,
        grid=(T // tm,),
        in_specs=[
            pl.BlockSpec((tm, D), lambda i: (i, 0)),
            pl.BlockSpec((TOP_K, tm, D), lambda i: (0, i, 0)),
            pl.BlockSpec((tm, 8), lambda i: (i, 0)),
            full(g), full(b),
        ],
        out_specs=pl.BlockSpec((tm, D), lambda i: (i, 0)),
        out_shape=jax.ShapeDtypeStruct((T, D), jnp.float32),
        compiler_params=_cparams(("parallel",)),
        name="combine_ln2",
    )(x1, yg, gates_t, g, b)


def kernel(x, w_in, w_out, lambda_q1, lambda_k1, lambda_q2, lambda_k2, diff_norm_g, ln1_g, ln1_b,
           w_router, b_router, w_up, b_up, w_down, b_down, ln2_g, ln2_b):
    B, S, D = x.shape
    T = B * S
    assert w_in.shape[0] == 1, "single layer"
    assert S % DIFF_TQ == 0 and DIFF_TK == DIFF_TQ and DIFF_TK % ROW_TILE == 0 and S % (16 * DIL_TQ) == 0

    na = 3 * DIL_WIDTH
    colscale = np.ones((3 * D,), np.float32)
    colscale[:DIL_WIDTH] = HEAD_DIM ** -0.5
    colscale[na:na + DIFF_WIDTH] = HEAD_DIM ** -0.5 * LOG2E
    w_in_bf = (w_in[0] * colscale).astype(jnp.bfloat16)
    nv = na + 2 * DIFF_WIDTH

    kpos, qpos, dabs = _position_columns()
    pa, qd, ka, vt = _in_proj(x, w_in_bf[:, :nv], w_in_bf[:, nv:].T, kpos)

    stats = None
    for n, (_, dil) in enumerate(DIL_PATTERNS):
        final = n == len(DIL_PATTERNS) - 1
        stats = _dilated_stage(pa, _dilated_bias(dil), dil, stats, final)
    oa = stats

    r2 = lambda a: a.reshape(1, -1).astype(jnp.float32)
    od = _diff_attention(qd, ka, vt, qpos, dabs, r2(lambda_q1[0]), r2(lambda_k1[0]), r2(lambda_q2[0]),
                         r2(lambda_k2[0]), diff_norm_g[0].reshape(-1, 1).astype(jnp.float32))

    tri = jnp.asarray(np.triu(np.ones((ROW_TILE, ROW_TILE), np.float32), k=1), jnp.bfloat16)
    x1, x1b, top_idx, gates, rank, counts = _post_attn(
        oa.reshape(T, DIL_WIDTH), od.reshape(T, DIFF_WIDTH), x.reshape(T, D), w_out[0].astype(jnp.bfloat16),
        r2(ln1_g[0]), r2(ln1_b[0]), w_router[0].T, b_router[0].reshape(N_EXPERTS, 1), tri)

    counts = counts[:, 0]
    pcounts = ((counts + MOE_TILE - 1) // MOE_TILE) * MOE_TILE
    pends = jnp.cumsum(pcounts)
    pstarts = pends - pcounts
    n_rows = T * TOP_K + N_EXPERTS * MOE_TILE
    n_tiles = n_rows // MOE_TILE
    dest = pstarts[top_idx[:TOP_K]] + rank[:TOP_K]
    tile_e = jnp.clip(jnp.searchsorted(pends, jnp.arange(n_tiles) * MOE_TILE, side='right'),
                      0, N_EXPERTS - 1).astype(jnp.int32)
    n_used = (pends[-1] // MOE_TILE).astype(jnp.int32).reshape(1)

    row_tok = jnp.zeros((n_rows,), jnp.int32).at[dest.reshape(-1)].set(
        jnp.tile(jnp.arange(T, dtype=jnp.int32), TOP_K))
    xs = x1b[row_tok]

    y = _experts(tile_e, n_used, xs, w_up[0].astype(jnp.bfloat16), b_up[0][:, None, :],
                 w_down[0].astype(jnp.bfloat16), b_down[0][:, None, :])
    yg = y[dest]

    out = _combine(x1, yg, gates.T, r2(ln2_g[0]), r2(ln2_b[0]))
    return out.reshape(B, S, D)
```

```python
import functools
import math

import jax
import jax.numpy as jnp
import numpy as np
from jax import lax
from jax.experimental import pallas as pl
from jax.experimental.pallas import tpu as pltpu

D_MODEL = 1024
HEAD_DIM = 64
DIL_WIDTH = 512
N_HEADS_DIL = 8
DIL_PATTERNS = ((128, 1), (512, 4), (2048, 16))
DIL_SIDE = 64
DIFF_WIDTH = 512
N_HEADS_DIFF = 4
DIFF_VDIM = 2 * HEAD_DIM
N_EXPERTS = 32
TOP_K = 4
D_FF = D_MODEL
SWIGLU_ALPHA = 1.702
SWIGLU_LIMIT = 7.0
DEEPNORM_ALPHA = 2.0 ** 0.25
LN_EPS = 1e-5
NEG_INF = -1e30
LAM_INIT = 0.8 - 0.6 * math.exp(-0.3 * 0)
LOG2E = math.log2(math.e)

LANES = 128
VMEM_LIMIT = 56 * 1024 * 1024

ROW_TILE = 512
DIFF_TQ = 1024
DIFF_TK = 1024
V_ROWS = DIFF_VDIM + 16
EXP2_HEADROOM = 4.0
DIL_TQ = 128
MOE_TILE = 512
COMBINE_TILE = 256
POS_SPLIT = 16


def _cparams(sem):
    return pltpu.CompilerParams(dimension_semantics=sem, vmem_limit_bytes=VMEM_LIMIT)


def _dot_nt(a, b, **kw):
    return lax.dot_general(a, b, (((1,), (1,)), ((), ())), preferred_element_type=jnp.float32, **kw)


def _dot(a, b, **kw):
    return jnp.dot(a, b, preferred_element_type=jnp.float32, **kw)


def _layer_norm(y, g, b):
    mu = jnp.mean(y, axis=-1, keepdims=True)
    yc = y - mu
    var = jnp.mean(yc * yc, axis=-1, keepdims=True)
    return yc * lax.rsqrt(var + LN_EPS) * g + b


def _in_proj_kernel(x_ref, w_ref, wvt_ref, kpos_ref, pa_ref, qd_ref, ka_ref, vt_ref):
    xb = x_ref[0].astype(jnp.bfloat16)
    na = 3 * DIL_WIDTH
    tm = xb.shape[0]
    pa_ref[0] = _dot(xb, w_ref[:, :na]).astype(jnp.bfloat16)
    q = _dot(xb, w_ref[:, na:na + DIFF_WIDTH]).astype(jnp.bfloat16)
    k = _dot(xb, w_ref[:, na + DIFF_WIDTH:]).astype(jnp.bfloat16)
    vt = _dot_nt(wvt_ref[...], xb).astype(jnp.bfloat16)
    ones = jnp.ones((V_ROWS - DIFF_VDIM, tm), jnp.bfloat16)
    for h in range(N_HEADS_DIFF):
        sl = slice(h * LANES, (h + 1) * LANES)
        qd_ref[0, h] = q[:, sl]
        ka_ref[0, h, :, :LANES] = k[:, sl]
        ka_ref[0, h, :, LANES:] = kpos_ref[h]
        vt_ref[0, h, :DIFF_VDIM, :] = vt[sl, :]
        vt_ref[0, h, DIFF_VDIM:, :] = ones


def _in_proj(x, w_bf, wvt_bf, kpos):
    B, S, D = x.shape
    tm = ROW_TILE
    na = 3 * DIL_WIDTH
    return pl.pallas_call(
        _in_proj_kernel,
        grid=(B, S // tm),
        in_specs=[
            pl.BlockSpec((1, tm, D), lambda b, i: (b, i, 0)),
            pl.BlockSpec(w_bf.shape, lambda b, i: (0, 0)),
            pl.BlockSpec(wvt_bf.shape, lambda b, i: (0, 0)),
            pl.BlockSpec((N_HEADS_DIFF, tm, LANES), lambda b, i: (0, i % (DIFF_TK // tm), 0)),
        ],
        out_specs=[
            pl.BlockSpec((1, tm, na), lambda b, i: (b, i, 0)),
            pl.BlockSpec((1, N_HEADS_DIFF, tm, LANES), lambda b, i: (b, 0, i, 0)),
            pl.BlockSpec((1, N_HEADS_DIFF, tm, 2 * LANES), lambda b, i: (b, 0, i, 0)),
            pl.BlockSpec((1, N_HEADS_DIFF, V_ROWS, tm), lambda b, i: (b, 0, 0, i)),
        ],
        out_shape=[
            jax.ShapeDtypeStruct((B, S, na), jnp.bfloat16),
            jax.ShapeDtypeStruct((B, N_HEADS_DIFF, S, LANES), jnp.bfloat16),
            jax.ShapeDtypeStruct((B, N_HEADS_DIFF, S, 2 * LANES), jnp.bfloat16),
            jax.ShapeDtypeStruct((B, N_HEADS_DIFF, V_ROWS, S), jnp.bfloat16),
        ],
        compiler_params=_cparams(("parallel", "parallel")),
        name="in_proj",
    )(x, w_bf, wvt_bf, kpos)


def _dilated_kernel(*refs, n_blocks, seq_len, has_prev, final):
    (q_ref, kp_ref, kc_ref, kn_ref, vp_ref, vc_ref, vn_ref, bias_ref), refs = refs[:8], refs[8:]
    if has_prev:
        (acc0_ref, m0_ref, l0_ref), refs = refs[:3], refs[3:]
    if final:
        (o_ref,) = refs
    else:
        acc_ref, m_ref, l_ref = refs
    tq = q_ref.shape[1]
    i = pl.program_id(2)
    q = q_ref[0]
    kcat = jnp.concatenate([kp_ref[0], kc_ref[0], kn_ref[0]], axis=0)
    vcat = jnp.concatenate([vp_ref[0], vc_ref[0], vn_ref[0]], axis=0)
    key_pos = (i - 1) * tq + lax.broadcasted_iota(jnp.int32, (1, 3 * tq), 1)
    edge = jnp.where((key_pos >= 0) & (key_pos < seq_len), 0.0, NEG_INF)
    low_half = lax.broadcasted_iota(jnp.int32, (tq, LANES), 1) < HEAD_DIM
    for pair in range(N_HEADS_DIL // 2):
        sl = slice(pair * LANES, (pair + 1) * LANES)
        qp, kp, vp = q[:, sl], kcat[:, sl], vcat[:, sl]
        res = []
        for hh in range(2):
            qm = jnp.where(low_half == (hh == 0), qp, jnp.zeros_like(qp))
            s = _dot_nt(qm, kp) + bias_ref[2 * pair + hh] + edge
            m = jnp.max(s, axis=-1, keepdims=True)
            p = jnp.exp(s - m)
            l = jnp.sum(p, axis=-1, keepdims=True)
            a = _dot(p.astype(jnp.bfloat16), vp)
            res.append((a, m, l))
        acc = jnp.where(low_half, res[0][0], res[1][0])
        m = jnp.where(low_half, res[0][1], res[1][1])
        l = jnp.where(low_half, res[0][2], res[1][2])
        if has_prev:
            m0 = m0_ref[0, :, sl]
            mm = jnp.maximum(m0, m)
            w0 = jnp.exp(m0 - mm)
            w1 = jnp.exp(m - mm)
            acc = acc0_ref[0, :, sl] * w0 + acc * w1
            l = l0_ref[0, :, sl] * w0 + l * w1
            m = mm
        if final:
            o_ref[0, :, sl] = (acc / l).astype(o_ref.dtype)
        else:
            acc_ref[0, :, sl] = acc
            m_ref[0, :, sl] = m
            l_ref[0, :, sl] = l


def _dilated_stage(pa, bias, dilation, prev, final):
    B, S, _ = pa.shape
    d = dilation
    L = S // d
    tq = DIL_TQ
    nb = L // tq
    w = DIL_WIDTH
    pav = pa.reshape(B, L, d * 3 * w)

    def spec(col, shift):
        def imap(b, r, i):
            return (b, jnp.clip(i + shift, 0, nb - 1), r * 3 + col)
        return pl.BlockSpec((1, tq, w), imap)

    stat_spec = pl.BlockSpec((1, tq, w), lambda b, r, i: (b, i, r))
    in_specs = [spec(0, 0), spec(1, -1), spec(1, 0), spec(1, 1), spec(2, -1), spec(2, 0), spec(2, 1),
                pl.BlockSpec(bias.shape, lambda b, r, i: (0, 0, 0))]
    args = [pav, pav, pav, pav, pav, pav, pav, bias]
    if prev is not None:
        in_specs += [stat_spec] * 3
        args += [a.reshape(B, L, d * w) for a in prev]
    if final:
        out_specs = stat_spec
        out_shape = jax.ShapeDtypeStruct((B, L, d * w), jnp.bfloat16)
    else:
        out_specs = [stat_spec] * 3
        out_shape = [jax.ShapeDtypeStruct((B, L, d * w), jnp.float32)] * 3
    out = pl.pallas_call(
        functools.partial(_dilated_kernel, n_blocks=nb, seq_len=L, has_prev=prev is not None, final=final),
        grid=(B, d, nb),
        in_specs=in_specs,
        out_specs=out_specs,
        out_shape=out_shape,
        compiler_params=_cparams(("parallel", "parallel", "parallel")),
        name=f"dilated_d{d}",
    )(*args)
    if final:
        return out.reshape(B, S, w)
    return tuple(o.reshape(B, S, w) for o in out)


def _dilated_bias(dilation):
    tq = DIL_TQ
    slopes = np.asarray([2.0 ** (-8.0 * (i + 1) / N_HEADS_DIL) for i in range(N_HEADS_DIL)], np.float32)
    rel = (np.arange(3 * tq)[None, :] - tq) - np.arange(tq)[:, None]
    band = np.abs(rel) <= DIL_SIDE
    pen = -(slopes * dilation)[:, None, None] * np.abs(rel).astype(np.float32)[None]
    return jnp.asarray(np.where(band[None], pen, np.float32(NEG_INF)), jnp.float32)


def _diff_kernel(q_ref, ka_ref, vt_ref, qpos_ref, dabs_ref, lq1_ref, lk1_ref, lq2_ref, lk2_ref, g_ref,
                 o_ref, qa_scr, m_scr, acc_scr, tmp_scr, *, slopes):
    h = pl.program_id(1)
    qi = pl.program_id(2)
    ki = pl.program_id(3)
    nk = pl.num_programs(3)
    tq = q_ref.shape[2]
    tk = ka_ref.shape[2]

    slope = jnp.float32(slopes[0])
    for hh in range(1, N_HEADS_DIFF):
        slope = jnp.where(h == hh, jnp.float32(slopes[hh]), slope)

    @pl.when(ki == 0)
    def _():
        q = q_ref[0, 0]
        low_half = lax.broadcasted_iota(jnp.int32, (tq, LANES), 1) < HEAD_DIM
        zero = jnp.zeros_like(q)
        qpos = qpos_ref[0]
        for c in range(2):
            qc = jnp.where(low_half == (c == 0), q, zero)
            for var, pos in enumerate((qpos, -qpos, jnp.zeros_like(qpos))):
                qa_scr[c, var, :, :LANES] = qc
                qa_scr[c, var, :, LANES:] = pos
        m_scr[...] = jnp.full(m_scr.shape, NEG_INF, jnp.float32)
        acc_scr[...] = jnp.zeros(acc_scr.shape, jnp.float32)

    kt = lax.rem(qi + ki, nk)
    var = jnp.where(kt < qi, 0, jnp.where(kt > qi, 1, 2))
    off = -slope * jnp.abs(qi * tq - kt * tk).astype(jnp.float32)
    ka = ka_ref[0, 0]
    vt = vt_ref[0, 0]


    def restabilise(diag):
        ss = []
        for c in range(2):
            s = _dot_nt(ka, qa_scr[c, var])
            if diag:
                s = s + dabs_ref[...] * slope
            ss.append(s)
        for c in range(2):
            s = ss[c]
            m_old = m_scr[c]
            m_new = jnp.maximum(m_old, jnp.max(s, axis=0, keepdims=True) + off)
            alpha = jnp.exp2(m_old - m_new)
            p = jnp.exp2((s - (m_new - off)).astype(jnp.bfloat16))
            acc_scr[c] = alpha * acc_scr[c] + _dot(vt, p)
            m_scr[c] = m_new

    @pl.when(ki == 0)
    def _():
        restabilise(True)

    @pl.when(ki != 0)
    def _():
        excess = None
        for c in range(2):
            s = _dot_nt(ka, qa_scr[c, var])
            m_eff = m_scr[c] - off
            over = jnp.max(jnp.max(s, axis=0, keepdims=True) - m_eff)
            excess = over if excess is None else jnp.maximum(excess, over)
            p = jnp.exp2((s - m_eff).astype(jnp.bfloat16))
            tmp_scr[c] = acc_scr[c] + _dot(vt, p)

        @pl.when(excess <= EXP2_HEADROOM)
        def _():
            acc_scr[...] = tmp_scr[...]

        @pl.when(excess > EXP2_HEADROOM)
        def _():
            restabilise(False)

    @pl.when(ki == nk - 1)
    def _():
        lam = (jnp.exp(jnp.sum(lq1_ref[...] * lk1_ref[...], axis=-1, keepdims=True))
               - jnp.exp(jnp.sum(lq2_ref[...] * lk2_ref[...], axis=-1, keepdims=True)) + LAM_INIT)
        a1, a2 = acc_scr[0], acc_scr[1]
        o = (a1[:DIFF_VDIM] / a1[DIFF_VDIM:DIFF_VDIM + 1]
             - lam * (a2[:DIFF_VDIM] / a2[DIFF_VDIM:DIFF_VDIM + 1]))
        o = o * lax.rsqrt(jnp.mean(o * o, axis=0, keepdims=True) + LN_EPS) * g_ref[...]
        o_ref[0] = (o * (1.0 - LAM_INIT)).T.astype(o_ref.dtype)


def _diff_attention(qd, ka, vt, qpos, dabs, lq1, lk1, lq2, lk2, g_col):
    B, H, S, _ = qd.shape
    tq, tk = DIFF_TQ, DIFF_TK
    slopes = tuple(float(s) for s in _diff_slopes_log2())
    nk = S // tk
    small = lambda a: pl.BlockSpec(a.shape, lambda b, h, qi, ki: (0, 0))
    return pl.pallas_call(
        functools.partial(_diff_kernel, slopes=slopes),
        grid=(B, H, S // tq, S // tk),
        in_specs=[
            pl.BlockSpec((1, 1, tq, LANES), lambda b, h, qi, ki: (b, h, qi, 0)),
            pl.BlockSpec((1, 1, tk, 2 * LANES), lambda b, h, qi, ki: (b, h, (qi + ki) % nk, 0)),
            pl.BlockSpec((1, 1, V_ROWS, tk), lambda b, h, qi, ki: (b, h, 0, (qi + ki) % nk)),
            pl.BlockSpec((1, tq, LANES), lambda b, h, qi, ki: (h, 0, 0)),
            small(dabs), small(lq1), small(lk1), small(lq2), small(lk2), small(g_col),
        ],
        out_specs=pl.BlockSpec((1, tq, LANES), lambda b, h, qi, ki: (b, qi, h)),
        out_shape=jax.ShapeDtypeStruct((B, S, H * LANES), jnp.bfloat16),
        scratch_shapes=[
            pltpu.VMEM((2, 3, tq, 2 * LANES), jnp.bfloat16),
            pltpu.VMEM((2, 1, tq), jnp.float32),
            pltpu.VMEM((2, V_ROWS, tq), jnp.float32),
            pltpu.VMEM((2, V_ROWS, tq), jnp.float32),
        ],
        compiler_params=_cparams(("parallel", "parallel", "parallel", "arbitrary")),
        name="diff_attention",
    )(qd, ka, vt, qpos, dabs, lq1, lk1, lq2, lk2, g_col)


def _diff_slopes_log2():
    return np.asarray([2.0 ** (-8.0 * (i + 1) / N_HEADS_DIFF) for i in range(N_HEADS_DIFF)],
                      np.float32) * np.float32(LOG2E)


def _position_columns():
    def bf16_round(v):
        u = np.asarray(v, np.float32).view(np.uint32)
        return ((u + 0x7FFF + ((u >> 16) & 1)) & np.uint32(0xFFFF0000)).view(np.float32)

    kpos = np.zeros((N_HEADS_DIFF, DIFF_TK, LANES), np.float32)
    qpos = np.zeros((N_HEADS_DIFF, DIFF_TQ, LANES), np.float32)
    pk, pq = np.arange(DIFF_TK), np.arange(DIFF_TQ)
    for h, a in enumerate(_diff_slopes_log2()):
        rest = np.float32(a)
        for n in range(3):
            a_n = np.float32(bf16_round(rest))
            rest = np.float32(rest - a_n)
            c = 4 * n
            kpos[h, :, c + 0] = (pk // POS_SPLIT) * POS_SPLIT
            kpos[h, :, c + 1] = pk % POS_SPLIT
            kpos[h, :, c + 2] = a_n
            kpos[h, :, c + 3] = a_n
            qpos[h, :, c + 0] = a_n
            qpos[h, :, c + 1] = a_n
            qpos[h, :, c + 2] = -((pq // POS_SPLIT) * POS_SPLIT)
            qpos[h, :, c + 3] = -(pq % POS_SPLIT)
    dabs = -np.abs(pq[:, None] - pk[None, :]).astype(np.float32)
    return jnp.asarray(kpos, jnp.bfloat16), jnp.asarray(qpos, jnp.bfloat16), jnp.asarray(dabs)


def _post_attn_kernel(oa_ref, od_ref, x_ref, wo_ref, g_ref, b_ref, wr_ref, br_ref, tri_ref,
                      x1_ref, x1b_ref, idx_ref, gate_ref, rank_ref, cnt_ref, carry_scr):
    step = pl.program_id(0)

    @pl.when(step == 0)
    def _():
        carry_scr[...] = jnp.zeros(carry_scr.shape, jnp.float32)

    mix = _dot(oa_ref[...], wo_ref[:DIL_WIDTH, :]) + _dot(od_ref[...], wo_ref[DIL_WIDTH:, :])
    x1 = _layer_norm(DEEPNORM_ALPHA * x_ref[...] + mix, g_ref[...], b_ref[...])
    x1_ref[...] = x1
    bits = pltpu.bitcast(x1.astype(jnp.bfloat16).astype(jnp.float32), jnp.uint32)
    half = x1.shape[1] // 2
    x1b_ref[...] = (bits[:, :half] >> 16) | (bits[:, half:] & jnp.uint32(0xFFFF0000))

    tm = x1.shape[0]
    lg = _dot_nt(wr_ref[...], x1, precision=lax.Precision.HIGHEST) + br_ref[...]
    eidx = lax.broadcasted_iota(jnp.int32, (N_EXPERTS, tm), 0)
    vals, sels = [], []
    for k in range(TOP_K):
        mx = jnp.max(lg, axis=0, keepdims=True)
        idx = jnp.min(jnp.where(lg == mx, eidx, N_EXPERTS), axis=0, keepdims=True)
        sel = eidx == idx
        vals.append(mx)
        sels.append(sel)
        idx_ref[k:k + 1, :] = idx
        lg = jnp.where(sel, -jnp.inf, lg)
    ex = [jnp.exp(v - vals[0]) for v in vals]
    den = ex[0] + ex[1] + ex[2] + ex[3]
    for k in range(TOP_K):
        gate_ref[k:k + 1, :] = ex[k] / den

    chosen = (sels[0] | sels[1] | sels[2] | sels[3])
    onehot = jnp.where(chosen, 1.0, 0.0)
    before = _dot(onehot.astype(jnp.bfloat16), tri_ref[...]) + carry_scr[...]
    for k in range(TOP_K):
        rank_ref[k:k + 1, :] = jnp.sum(jnp.where(sels[k], before, 0.0), axis=0,
                                       keepdims=True).astype(jnp.int32)
    carry_scr[...] = carry_scr[...] + jnp.sum(onehot, axis=1, keepdims=True)
    pad = jnp.zeros((8 - TOP_K, tm), jnp.int32)
    idx_ref[TOP_K:, :] = pad
    rank_ref[TOP_K:, :] = pad
    gate_ref[TOP_K:, :] = pad.astype(jnp.float32)
    cnt_ref[...] = jnp.broadcast_to(carry_scr[...], cnt_ref.shape).astype(jnp.int32)


def _post_attn(oa, od, x, wo_bf, g, b, wr_t, br, tri):
    T, D = x.shape
    tm = ROW_TILE
    row = lambda w: pl.BlockSpec((tm, w), lambda i: (i, 0))
    full = lambda a: pl.BlockSpec(a.shape, lambda i: (0, 0))
    col = pl.BlockSpec((8, tm), lambda i: (0, i))
    return pl.pallas_call(
        _post_attn_kernel,
        grid=(T // tm,),
        in_specs=[row(DIL_WIDTH), row(DIFF_WIDTH), row(D), full(wo_bf), full(g), full(b), full(wr_t),
                  full(br), full(tri)],
        out_specs=[row(D), row(D // 2), col, col, col, pl.BlockSpec((N_EXPERTS, LANES), lambda i: (0, 0))],
        out_shape=[
            jax.ShapeDtypeStruct((T, D), jnp.float32),
            jax.ShapeDtypeStruct((T, D // 2), jnp.uint32),
            jax.ShapeDtypeStruct((8, T), jnp.int32),
            jax.ShapeDtypeStruct((8, T), jnp.float32),
            jax.ShapeDtypeStruct((8, T), jnp.int32),
            jax.ShapeDtypeStruct((N_EXPERTS, LANES), jnp.int32),
        ],
        scratch_shapes=[pltpu.VMEM((N_EXPERTS, 1), jnp.float32)],
        compiler_params=_cparams(("arbitrary",)),
        name="post_attn_router",
    )(oa, od, x, wo_bf, g, b, wr_t, br, tri)


def _fetch_rows_index(dest_hbm, step, idx_smem, sem):
    cp = pltpu.make_async_copy(dest_hbm.at[step], idx_smem, sem)
    cp.start()
    cp.wait()


def _dispatch_kernel(dest_hbm, x_ref, xs_in_hbm, xs_hbm, idx_smem, idx_sem, row_sem):
    del xs_in_hbm
    tm = x_ref.shape[0]
    _fetch_rows_index(dest_hbm, pl.program_id(0), idx_smem, idx_sem)

    def row_copy(t, k):
        return pltpu.make_async_copy(x_ref.at[pl.ds(t, 1)], xs_hbm.at[pl.ds(idx_smem[k * tm + t], 1)], row_sem)

    def issue(t, carry):
        for k in range(TOP_K):
            row_copy(t, k).start()
        return carry

    lax.fori_loop(0, tm, issue, 0, unroll=8)
    for k in range(TOP_K):
        pltpu.make_async_copy(x_ref, xs_hbm.at[pl.ds(0, tm)], row_sem).wait()


def _dispatch(dest_steps, x1p, n_rows):
    T, W = x1p.shape
    n_steps, per_step = dest_steps.shape
    tm = per_step // TOP_K
    xs0 = jnp.zeros((n_rows, W), x1p.dtype)
    return pl.pallas_call(
        _dispatch_kernel,
        grid=(n_steps,),
        in_specs=[
            pl.BlockSpec(memory_space=pl.ANY),
            pl.BlockSpec((tm, W), lambda i: (i, 0)),
            pl.BlockSpec(memory_space=pl.ANY),
        ],
        out_specs=pl.BlockSpec(memory_space=pl.ANY),
        out_shape=jax.ShapeDtypeStruct((n_rows, W), x1p.dtype),
        scratch_shapes=[pltpu.SMEM((per_step,), jnp.int32), pltpu.SemaphoreType.DMA(()),
                        pltpu.SemaphoreType.DMA(())],
        input_output_aliases={2: 0},
        compiler_params=_cparams(("arbitrary",)),
        name="dispatch_rows",
    )(dest_steps, x1p, xs0)


def _expert_kernel(tile_e_ref, n_used_ref, xs_ref, wu_ref, bu_ref, wd_ref, bd_ref, y_ref, wu_scr, wd_scr):
    i = pl.program_id(0)
    used = i < n_used_ref[0]
    new_expert = (i == 0) | (tile_e_ref[i] != tile_e_ref[jnp.maximum(i - 1, 0)])

    @pl.when(used & new_expert)
    def _():
        wu_scr[...] = wu_ref[0].astype(jnp.bfloat16)
        wd_scr[...] = wd_ref[0].astype(jnp.bfloat16)

    @pl.when(used)
    def _():
        w = xs_ref[...]
        half = w.shape[1]
        lo = pltpu.bitcast(w << 16, jnp.float32).astype(jnp.bfloat16)
        hi = pltpu.bitcast(w & jnp.uint32(0xFFFF0000), jnp.float32).astype(jnp.bfloat16)
        hu = _dot(lo, wu_scr[:half, :]) + _dot(hi, wu_scr[half:, :]) + bu_ref[0]
        g = jnp.minimum(hu[:, :D_FF], SWIGLU_LIMIT)
        u = jnp.clip(hu[:, D_FF:], -SWIGLU_LIMIT, SWIGLU_LIMIT)
        act = g * (1.0 / (1.0 + jnp.exp(-SWIGLU_ALPHA * g))) * (u + 1.0)
        y_ref[...] = _dot(act.astype(jnp.bfloat16), wd_scr[...]) + bd_ref[0]

    @pl.when(jnp.logical_not(used))
    def _():
        y_ref[...] = jnp.zeros(y_ref.shape, y_ref.dtype)


def _experts(tile_e, n_used, xs, wu, bu, wd, bd):
    n_rows, W = xs.shape
    D = 2 * W
    tm = MOE_TILE
    grid_spec = pltpu.PrefetchScalarGridSpec(
        num_scalar_prefetch=2,
        grid=(n_rows // tm,),
        in_specs=[
            pl.BlockSpec((tm, W), lambda i, te, nu: (i, 0)),
            pl.BlockSpec((1, D, 2 * D_FF), lambda i, te, nu: (te[i], 0, 0)),
            pl.BlockSpec((1, 1, 2 * D_FF), lambda i, te, nu: (te[i], 0, 0)),
            pl.BlockSpec((1, D_FF, D), lambda i, te, nu: (te[i], 0, 0)),
            pl.BlockSpec((1, 1, D), lambda i, te, nu: (te[i], 0, 0)),
        ],
        out_specs=pl.BlockSpec((tm, D), lambda i, te, nu: (i, 0)),
        scratch_shapes=[pltpu.VMEM((D, 2 * D_FF), jnp.bfloat16), pltpu.VMEM((D_FF, D), jnp.bfloat16)],
    )
    return pl.pallas_call(
        _expert_kernel,
        grid_spec=grid_spec,
        out_shape=jax.ShapeDtypeStruct((n_rows, D), jnp.float32),
        compiler_params=_cparams(("arbitrary",)),
        name="experts",
    )(tile_e, n_used, xs, wu, bu, wd, bd)


def _combine_kernel(dest_hbm, y_hbm, x1_ref, gate_ref, g_ref, b_ref, o_ref, buf, idx_smem, idx_sem, row_sems):
    i = pl.program_id(0)
    n = pl.num_programs(0)
    tm = x1_ref.shape[0]

    def issue(step, slot):
        _fetch_rows_index(dest_hbm, step, idx_smem, idx_sem)

        def body(t, carry):
            for k in range(TOP_K):
                pltpu.make_async_copy(y_hbm.at[pl.ds(idx_smem[k * tm + t], 1)],
                                      buf.at[slot, k, pl.ds(t, 1)], row_sems.at[slot]).start()
            return carry

        lax.fori_loop(0, tm, body, 0, unroll=8)

    @pl.when(i == 0)
    def _():
        issue(0, 0)

    @pl.when(i + 1 < n)
    def _():
        issue(i + 1, (i + 1) % 2)

    slot = i % 2
    for k in range(TOP_K):
        pltpu.make_async_copy(y_hbm.at[pl.ds(0, tm)], buf.at[slot, k], row_sems.at[slot]).wait()
    ffn = buf[slot, 0] * gate_ref[:, 0:1]
    for k in range(1, TOP_K):
        ffn = ffn + buf[slot, k] * gate_ref[:, k:k + 1]
    o_ref[...] = _layer_norm(DEEPNORM_ALPHA * x1_ref[...] + ffn, g_ref[...], b_ref[...])


def _combine(dest_steps, y, x1, gates_t, g, b):
    T, D = x1.shape
    n_steps, per_step = dest_steps.shape
    tm = per_step // TOP_K
    full = lambda a: pl.BlockSpec(a.shape, lambda i: (0, 0))
    return pl.pallas_call(
        _combine_kernel,
        grid=(n_steps,),
        in_specs=[
            pl.BlockSpec(memory_space=pl.ANY),
            pl.BlockSpec(memory_space=pl.ANY),
            pl.BlockSpec((tm, D), lambda i: (i, 0)),
            pl.BlockSpec((tm, 8), lambda i: (i, 0)),
            full(g), full(b),
        ],
        out_specs=pl.BlockSpec((tm, D), lambda i: (i, 0)),
        out_shape=jax.ShapeDtypeStruct((T, D), jnp.float32),
        scratch_shapes=[pltpu.VMEM((2, TOP_K, tm, D), jnp.float32), pltpu.SMEM((per_step,), jnp.int32),
                        pltpu.SemaphoreType.DMA(()), pltpu.SemaphoreType.DMA((2,))],
        compiler_params=_cparams(("arbitrary",)),
        name="combine_ln2",
    )(dest_steps, y, x1, gates_t, g, b)


def kernel(x, w_in, w_out, lambda_q1, lambda_k1, lambda_q2, lambda_k2, diff_norm_g, ln1_g, ln1_b,
           w_router, b_router, w_up, b_up, w_down, b_down, ln2_g, ln2_b):
    B, S, D = x.shape
    T = B * S
    assert w_in.shape[0] == 1, "single layer"
    assert S % DIFF_TQ == 0 and DIFF_TK == DIFF_TQ and DIFF_TK % ROW_TILE == 0 and S % (16 * DIL_TQ) == 0

    na = 3 * DIL_WIDTH
    colscale = np.ones((3 * D,), np.float32)
    colscale[:DIL_WIDTH] = HEAD_DIM ** -0.5
    colscale[na:na + DIFF_WIDTH] = HEAD_DIM ** -0.5 * LOG2E
    w_in_bf = (w_in[0] * colscale).astype(jnp.bfloat16)
    nv = na + 2 * DIFF_WIDTH

    kpos, qpos, dabs = _position_columns()
    pa, qd, ka, vt = _in_proj(x, w_in_bf[:, :nv], w_in_bf[:, nv:].T, kpos)

    stats = None
    for n, (_, dil) in enumerate(DIL_PATTERNS):
        final = n == len(DIL_PATTERNS) - 1
        stats = _dilated_stage(pa, _dilated_bias(dil), dil, stats, final)
    oa = stats

    r2 = lambda a: a.reshape(1, -1).astype(jnp.float32)
    od = _diff_attention(qd, ka, vt, qpos, dabs, r2(lambda_q1[0]), r2(lambda_k1[0]), r2(lambda_q2[0]),
                         r2(lambda_k2[0]), diff_norm_g[0].reshape(-1, 1).astype(jnp.float32))

    tri = jnp.asarray(np.triu(np.ones((ROW_TILE, ROW_TILE), np.float32), k=1), jnp.bfloat16)
    x1, x1p, top_idx, gates, rank, counts = _post_attn(
        oa.reshape(T, DIL_WIDTH), od.reshape(T, DIFF_WIDTH), x.reshape(T, D), w_out[0].astype(jnp.bfloat16),
        r2(ln1_g[0]), r2(ln1_b[0]), w_router[0].T, b_router[0].reshape(N_EXPERTS, 1), tri)

    counts = counts[:, 0]
    pcounts = ((counts + MOE_TILE - 1) // MOE_TILE) * MOE_TILE
    pends = jnp.cumsum(pcounts)
    pstarts = pends - pcounts
    n_rows = T * TOP_K + N_EXPERTS * MOE_TILE
    n_tiles = n_rows // MOE_TILE
    dest = pstarts[top_idx[:TOP_K]] + rank[:TOP_K]
    tile_e = jnp.clip(jnp.searchsorted(pends, jnp.arange(n_tiles) * MOE_TILE, side='right'),
                      0, N_EXPERTS - 1).astype(jnp.int32)
    n_used = (pends[-1] // MOE_TILE).astype(jnp.int32).reshape(1)

    def per_step(tm):
        return dest.reshape(TOP_K, T // tm, tm).transpose(1, 0, 2).reshape(T // tm, TOP_K * tm)

    xs = _dispatch(per_step(ROW_TILE), x1p, n_rows)
    y = _experts(tile_e, n_used, xs, w_up[0], b_up[0][:, None, :], w_down[0], b_down[0][:, None, :])
    out = _combine(per_step(COMBINE_TILE), y, x1, gates.T, r2(ln2_g[0]), r2(ln2_b[0]))
    return out.reshape(B, S, D)
```

```python
import functools
import math

import jax
import jax.numpy as jnp
import numpy as np
from jax import lax
from jax.experimental import pallas as pl
from jax.experimental.pallas import tpu as pltpu

D_MODEL = 1024
HEAD_DIM = 64
DIL_WIDTH = 512
N_HEADS_DIL = 8
DIL_PATTERNS = ((128, 1), (512, 4), (2048, 16))
DIL_SIDE = 64
DIFF_WIDTH = 512
N_HEADS_DIFF = 4
DIFF_VDIM = 2 * HEAD_DIM
N_EXPERTS = 32
TOP_K = 4
D_FF = D_MODEL
SWIGLU_ALPHA = 1.702
SWIGLU_LIMIT = 7.0
DEEPNORM_ALPHA = 2.0 ** 0.25
LN_EPS = 1e-5
NEG_INF = -1e30
LAM_INIT = 0.8 - 0.6 * math.exp(-0.3 * 0)
LOG2E = math.log2(math.e)

LANES = 128
VMEM_LIMIT = 56 * 1024 * 1024

ROW_TILE = 512
DIFF_TQ = 1024
DIFF_TK = 1024
V_ROWS = DIFF_VDIM + 16
EXP2_HEADROOM = 4.0
UNDERFLOW_LOG2 = 160.0
NORM_SLACK = 1.01
DIL_TQ = 128
MOE_TILE = 512
COMBINE_TILE = 256
POS_SPLIT = 16


def _cparams(sem):
    return pltpu.CompilerParams(dimension_semantics=sem, vmem_limit_bytes=VMEM_LIMIT)


def _dot_nt(a, b, **kw):
    return lax.dot_general(a, b, (((1,), (1,)), ((), ())), preferred_element_type=jnp.float32, **kw)


def _dot(a, b, **kw):
    return jnp.dot(a, b, preferred_element_type=jnp.float32, **kw)


def _layer_norm(y, g, b):
    mu = jnp.mean(y, axis=-1, keepdims=True)
    yc = y - mu
    var = jnp.mean(yc * yc, axis=-1, keepdims=True)
    return yc * lax.rsqrt(var + LN_EPS) * g + b


def _in_proj_kernel(x_ref, w_ref, wvt_ref, kpos_ref, grp_ref, pa_ref, qd_ref, ka_ref, vt_ref, qn_ref, kn_ref):
    xb = x_ref[0].astype(jnp.bfloat16)
    na = 3 * DIL_WIDTH
    tm = xb.shape[0]
    pa_ref[0] = _dot(xb, w_ref[:, :na]).astype(jnp.bfloat16)
    q = _dot(xb, w_ref[:, na:na + DIFF_WIDTH]).astype(jnp.bfloat16)
    k = _dot(xb, w_ref[:, na + DIFF_WIDTH:]).astype(jnp.bfloat16)
    vt = _dot_nt(wvt_ref[...], xb).astype(jnp.bfloat16)
    ones = jnp.ones((V_ROWS - DIFF_VDIM, tm), jnp.bfloat16)

    @pl.when(pl.program_id(1) == 0)
    def _():
        qn_ref[...] = jnp.zeros(qn_ref.shape, jnp.float32)
        kn_ref[...] = jnp.zeros(kn_ref.shape, jnp.float32)

    for src, dst in ((q, qn_ref), (k, kn_ref)):
        f = src.astype(jnp.float32)
        sq = _dot(f * f, grp_ref[...], precision=lax.Precision.HIGHEST)
        dst[0] = jnp.maximum(dst[0], jnp.max(sq, axis=0, keepdims=True))
    for h in range(N_HEADS_DIFF):
        sl = slice(h * LANES, (h + 1) * LANES)
        qd_ref[0, h] = q[:, sl]
        ka_ref[0, h, :, :LANES] = k[:, sl]
        ka_ref[0, h, :, LANES:] = kpos_ref[h]
        vt_ref[0, h, :DIFF_VDIM, :] = vt[sl, :]
        vt_ref[0, h, DIFF_VDIM:, :] = ones


def _in_proj(x, w_bf, wvt_bf, kpos):
    B, S, D = x.shape
    tm = ROW_TILE
    na = 3 * DIL_WIDTH
    grp = np.zeros((DIFF_WIDTH, LANES), np.float32)
    grp[np.arange(DIFF_WIDTH), np.arange(DIFF_WIDTH) // HEAD_DIM] = 1.0
    grp = jnp.asarray(grp)
    norm_spec = pl.BlockSpec((1, 1, LANES), lambda b, i: (b, 0, 0))
    return pl.pallas_call(
        _in_proj_kernel,
        grid=(B, S // tm),
        in_specs=[
            pl.BlockSpec((1, tm, D), lambda b, i: (b, i, 0)),
            pl.BlockSpec(w_bf.shape, lambda b, i: (0, 0)),
            pl.BlockSpec(wvt_bf.shape, lambda b, i: (0, 0)),
            pl.BlockSpec((N_HEADS_DIFF, tm, LANES), lambda b, i: (0, i % (DIFF_TK // tm), 0)),
            pl.BlockSpec(grp.shape, lambda b, i: (0, 0)),
        ],
        out_specs=[
            pl.BlockSpec((1, tm, na), lambda b, i: (b, i, 0)),
            pl.BlockSpec((1, N_HEADS_DIFF, tm, LANES), lambda b, i: (b, 0, i, 0)),
            pl.BlockSpec((1, N_HEADS_DIFF, tm, 2 * LANES), lambda b, i: (b, 0, i, 0)),
            pl.BlockSpec((1, N_HEADS_DIFF, V_ROWS, tm), lambda b, i: (b, 0, 0, i)),
            norm_spec, norm_spec,
        ],
        out_shape=[
            jax.ShapeDtypeStruct((B, S, na), jnp.bfloat16),
            jax.ShapeDtypeStruct((B, N_HEADS_DIFF, S, LANES), jnp.bfloat16),
            jax.ShapeDtypeStruct((B, N_HEADS_DIFF, S, 2 * LANES), jnp.bfloat16),
            jax.ShapeDtypeStruct((B, N_HEADS_DIFF, V_ROWS, S), jnp.bfloat16),
            jax.ShapeDtypeStruct((B, 1, LANES), jnp.float32),
            jax.ShapeDtypeStruct((B, 1, LANES), jnp.float32),
        ],
        compiler_params=_cparams(("parallel", "arbitrary")),
        name="in_proj",
    )(x, w_bf, wvt_bf, kpos, grp)


def _dilated_kernel(*refs, n_blocks, seq_len, has_prev, final):
    (q_ref, kp_ref, kc_ref, kn_ref, vp_ref, vc_ref, vn_ref, bias_ref), refs = refs[:8], refs[8:]
    if has_prev:
        (acc0_ref, m0_ref, l0_ref), refs = refs[:3], refs[3:]
    if final:
        (o_ref,) = refs
    else:
        acc_ref, m_ref, l_ref = refs
    tq = q_ref.shape[1]
    i = pl.program_id(2)
    q = q_ref[0]
    kcat = jnp.concatenate([kp_ref[0], kc_ref[0], kn_ref[0]], axis=0)
    vcat = jnp.concatenate([vp_ref[0], vc_ref[0], vn_ref[0]], axis=0)
    key_pos = (i - 1) * tq + lax.broadcasted_iota(jnp.int32, (1, 3 * tq), 1)
    edge = jnp.where((key_pos >= 0) & (key_pos < seq_len), 0.0, NEG_INF)
    low_half = lax.broadcasted_iota(jnp.int32, (tq, LANES), 1) < HEAD_DIM
    for pair in range(N_HEADS_DIL // 2):
        sl = slice(pair * LANES, (pair + 1) * LANES)
        qp, kp, vp = q[:, sl], kcat[:, sl], vcat[:, sl]
        res = []
        for hh in range(2):
            qm = jnp.where(low_half == (hh == 0), qp, jnp.zeros_like(qp))
            s = _dot_nt(qm, kp) + bias_ref[2 * pair + hh] + edge
            m = jnp.max(s, axis=-1, keepdims=True)
            p = jnp.exp(s - m)
            l = jnp.sum(p, axis=-1, keepdims=True)
            a = _dot(p.astype(jnp.bfloat16), vp)
            res.append((a, m, l))
        acc = jnp.where(low_half, res[0][0], res[1][0])
        m = jnp.where(low_half, res[0][1], res[1][1])
        l = jnp.where(low_half, res[0][2], res[1][2])
        if has_prev:
            m0 = m0_ref[0, :, sl]
            mm = jnp.maximum(m0, m)
            w0 = jnp.exp(m0 - mm)
            w1 = jnp.exp(m - mm)
            acc = acc0_ref[0, :, sl] * w0 + acc * w1
            l = l0_ref[0, :, sl] * w0 + l * w1
            m = mm
        if final:
            o_ref[0, :, sl] = (acc / l).astype(o_ref.dtype)
        else:
            acc_ref[0, :, sl] = acc
            m_ref[0, :, sl] = m
            l_ref[0, :, sl] = l


def _dilated_stage(pa, bias, dilation, prev, final):
    B, S, _ = pa.shape
    d = dilation
    L = S // d
    tq = DIL_TQ
    nb = L // tq
    w = DIL_WIDTH
    pav = pa.reshape(B, L, d * 3 * w)

    def spec(col, shift):
        def imap(b, r, i):
            return (b, jnp.clip(i + shift, 0, nb - 1), r * 3 + col)
        return pl.BlockSpec((1, tq, w), imap)

    stat_spec = pl.BlockSpec((1, tq, w), lambda b, r, i: (b, i, r))
    in_specs = [spec(0, 0), spec(1, -1), spec(1, 0), spec(1, 1), spec(2, -1), spec(2, 0), spec(2, 1),
                pl.BlockSpec(bias.shape, lambda b, r, i: (0, 0, 0))]
    args = [pav, pav, pav, pav, pav, pav, pav, bias]
    if prev is not None:
        in_specs += [stat_spec] * 3
        args += [a.reshape(B, L, d * w) for a in prev]
    if final:
        out_specs = stat_spec
        out_shape = jax.ShapeDtypeStruct((B, L, d * w), jnp.bfloat16)
    else:
        out_specs = [stat_spec] * 3
        out_shape = [jax.ShapeDtypeStruct((B, L, d * w), jnp.float32)] * 3
    out = pl.pallas_call(
        functools.partial(_dilated_kernel, n_blocks=nb, seq_len=L, has_prev=prev is not None, final=final),
        grid=(B, d, nb),
        in_specs=in_specs,
        out_specs=out_specs,
        out_shape=out_shape,
        compiler_params=_cparams(("parallel", "parallel", "parallel")),
        name=f"dilated_d{d}",
    )(*args)
    if final:
        return out.reshape(B, S, w)
    return tuple(o.reshape(B, S, w) for o in out)


def _dilated_bias(dilation):
    tq = DIL_TQ
    slopes = np.asarray([2.0 ** (-8.0 * (i + 1) / N_HEADS_DIL) for i in range(N_HEADS_DIL)], np.float32)
    rel = (np.arange(3 * tq)[None, :] - tq) - np.arange(tq)[:, None]
    band = np.abs(rel) <= DIL_SIDE
    pen = -(slopes * dilation)[:, None, None] * np.abs(rel).astype(np.float32)[None]
    return jnp.asarray(np.where(band[None], pen, np.float32(NEG_INF)), jnp.float32)


def _diff_kernel(reach_ref, q_ref, ka_ref, vt_ref, qpos_ref, dabs_ref, lq1_ref, lk1_ref, lq2_ref, lk2_ref,
                 g_ref, o_ref, qa_scr, m_scr, acc_scr, tmp_scr, *, slopes):
    b = pl.program_id(0)
    h = pl.program_id(1)
    qi = pl.program_id(2)
    ki = pl.program_id(3)
    nk = pl.num_programs(3)
    tq = q_ref.shape[2]
    tk = ka_ref.shape[2]

    slope = jnp.float32(slopes[0])
    for hh in range(1, N_HEADS_DIFF):
        slope = jnp.where(h == hh, jnp.float32(slopes[hh]), slope)

    @pl.when(ki == 0)
    def _():
        q = q_ref[0, 0]
        low_half = lax.broadcasted_iota(jnp.int32, (tq, LANES), 1) < HEAD_DIM
        zero = jnp.zeros_like(q)
        qpos = qpos_ref[0]
        for c in range(2):
            qc = jnp.where(low_half == (c == 0), q, zero)
            for var, pos in enumerate((qpos, -qpos, jnp.zeros_like(qpos))):
                qa_scr[c, var, :, :LANES] = qc
                qa_scr[c, var, :, LANES:] = pos
        m_scr[...] = jnp.full(m_scr.shape, NEG_INF, jnp.float32)
        acc_scr[...] = jnp.zeros(acc_scr.shape, jnp.float32)

    kt = lax.rem(qi + ki, nk)
    var = jnp.where(kt < qi, 0, jnp.where(kt > qi, 1, 2))
    off = -slope * jnp.abs(qi * tq - kt * tk).astype(jnp.float32)
    ka = ka_ref[0, 0]
    vt = vt_ref[0, 0]


    def restabilise(diag):
        ss = []
        for c in range(2):
            s = _dot_nt(ka, qa_scr[c, var])
            if diag:
                s = s + dabs_ref[...] * slope
            ss.append(s)
        for c in range(2):
            s = ss[c]
            m_old = m_scr[c]
            m_new = jnp.maximum(m_old, jnp.max(s, axis=0, keepdims=True) + off)
            alpha = jnp.exp2(m_old - m_new)
            p = jnp.exp2((s - (m_new - off)).astype(jnp.bfloat16))
            acc_scr[c] = alpha * acc_scr[c] + _dot(vt, p)
            m_scr[c] = m_new

    @pl.when(ki == 0)
    def _():
        restabilise(True)

    @pl.when((ki != 0) & (jnp.abs(kt - qi) <= reach_ref[b * N_HEADS_DIFF + h]))
    def _():
        excess = None
        for c in range(2):
            s = _dot_nt(ka, qa_scr[c, var])
            m_eff = m_scr[c] - off
            over = jnp.max(jnp.max(s, axis=0, keepdims=True) - m_eff)
            excess = over if excess is None else jnp.maximum(excess, over)
            p = jnp.exp2((s - m_eff).astype(jnp.bfloat16))
            tmp_scr[c] = acc_scr[c] + _dot(vt, p)

        @pl.when(excess <= EXP2_HEADROOM)
        def _():
            acc_scr[...] = tmp_scr[...]

        @pl.when(excess > EXP2_HEADROOM)
        def _():
            restabilise(False)

    @pl.when(ki == nk - 1)
    def _():
        lam = (jnp.exp(jnp.sum(lq1_ref[...] * lk1_ref[...], axis=-1, keepdims=True))
               - jnp.exp(jnp.sum(lq2_ref[...] * lk2_ref[...], axis=-1, keepdims=True)) + LAM_INIT)
        a1, a2 = acc_scr[0], acc_scr[1]
        o = (a1[:DIFF_VDIM] / a1[DIFF_VDIM:DIFF_VDIM + 1]
             - lam * (a2[:DIFF_VDIM] / a2[DIFF_VDIM:DIFF_VDIM + 1]))
        o = o * lax.rsqrt(jnp.mean(o * o, axis=0, keepdims=True) + LN_EPS) * g_ref[...]
        o_ref[0] = (o * (1.0 - LAM_INIT)).T.astype(o_ref.dtype)


def _tile_reach(qn, kn, n_tiles):
    B = qn.shape[0]
    ng = 2 * N_HEADS_DIFF
    qmax = jnp.sqrt(qn[:, 0, :ng]) * NORM_SLACK
    kmax = jnp.sqrt(kn[:, 0, :ng]) * NORM_SLACK
    bound = jnp.max((2.0 * qmax * kmax).reshape(B, N_HEADS_DIFF, 2), axis=-1) + UNDERFLOW_LOG2
    far = bound / jnp.asarray(_diff_slopes_log2())[None, :]
    reach = jnp.floor((far - 1.0) / DIFF_TK) + 1.0
    reach = jnp.where(jnp.isfinite(reach), reach, n_tiles)
    return jnp.clip(reach, 0, n_tiles).astype(jnp.int32).reshape(-1)


def _diff_attention(reach, qd, ka, vt, qpos, dabs, lq1, lk1, lq2, lk2, g_col):
    B, H, S, _ = qd.shape
    tq, tk = DIFF_TQ, DIFF_TK
    slopes = tuple(float(s) for s in _diff_slopes_log2())
    nk = S // tk

    def key_tile(b, h, qi, ki, reach):
        kt = (qi + ki) % nk
        return jnp.where(jnp.abs(kt - qi) <= reach[b * H + h], kt, qi)

    small = lambda a: pl.BlockSpec(a.shape, lambda b, h, qi, ki, reach: (0, 0))
    grid_spec = pltpu.PrefetchScalarGridSpec(
        num_scalar_prefetch=1,
        grid=(B, H, S // tq, nk),
        in_specs=[
            pl.BlockSpec((1, 1, tq, LANES), lambda b, h, qi, ki, reach: (b, h, qi, 0)),
            pl.BlockSpec((1, 1, tk, 2 * LANES),
                         lambda b, h, qi, ki, reach: (b, h, key_tile(b, h, qi, ki, reach), 0)),
            pl.BlockSpec((1, 1, V_ROWS, tk),
                         lambda b, h, qi, ki, reach: (b, h, 0, key_tile(b, h, qi, ki, reach))),
            pl.BlockSpec((1, tq, LANES), lambda b, h, qi, ki, reach: (h, 0, 0)),
            small(dabs), small(lq1), small(lk1), small(lq2), small(lk2), small(g_col),
        ],
        out_specs=pl.BlockSpec((1, tq, LANES), lambda b, h, qi, ki, reach: (b, qi, h)),
        scratch_shapes=[
            pltpu.VMEM((2, 3, tq, 2 * LANES), jnp.bfloat16),
            pltpu.VMEM((2, 1, tq), jnp.float32),
            pltpu.VMEM((2, V_ROWS, tq), jnp.float32),
            pltpu.VMEM((2, V_ROWS, tq), jnp.float32),
        ],
    )
    return pl.pallas_call(
        functools.partial(_diff_kernel, slopes=slopes),
        grid_spec=grid_spec,
        out_shape=jax.ShapeDtypeStruct((B, S, H * LANES), jnp.bfloat16),
        compiler_params=_cparams(("parallel", "parallel", "parallel", "arbitrary")),
        name="diff_attention",
    )(reach, qd, ka, vt, qpos, dabs, lq1, lk1, lq2, lk2, g_col)


def _diff_slopes_log2():
    return np.asarray([2.0 ** (-8.0 * (i + 1) / N_HEADS_DIFF) for i in range(N_HEADS_DIFF)],
                      np.float32) * np.float32(LOG2E)


def _position_columns():
    def bf16_round(v):
        u = np.asarray(v, np.float32).view(np.uint32)
        return ((u + 0x7FFF + ((u >> 16) & 1)) & np.uint32(0xFFFF0000)).view(np.float32)

    kpos = np.zeros((N_HEADS_DIFF, DIFF_TK, LANES), np.float32)
    qpos = np.zeros((N_HEADS_DIFF, DIFF_TQ, LANES), np.float32)
    pk, pq = np.arange(DIFF_TK), np.arange(DIFF_TQ)
    for h, a in enumerate(_diff_slopes_log2()):
        rest = np.float32(a)
        for n in range(3):
            a_n = np.float32(bf16_round(rest))
            rest = np.float32(rest - a_n)
            c = 4 * n
            kpos[h, :, c + 0] = (pk // POS_SPLIT) * POS_SPLIT
            kpos[h, :, c + 1] = pk % POS_SPLIT
            kpos[h, :, c + 2] = a_n
            kpos[h, :, c + 3] = a_n
            qpos[h, :, c + 0] = a_n
            qpos[h, :, c + 1] = a_n
            qpos[h, :, c + 2] = -((pq // POS_SPLIT) * POS_SPLIT)
            qpos[h, :, c + 3] = -(pq % POS_SPLIT)
    dabs = -np.abs(pq[:, None] - pk[None, :]).astype(np.float32)
    return jnp.asarray(kpos, jnp.bfloat16), jnp.asarray(qpos, jnp.bfloat16), jnp.asarray(dabs)


def _post_attn_kernel(oa_ref, od_ref, x_ref, wo_ref, g_ref, b_ref, wr_ref, br_ref, tri_ref,
                      x1_ref, x1b_ref, idx_ref, gate_ref, rank_ref, cnt_ref, carry_scr):
    step = pl.program_id(0)

    @pl.when(step == 0)
    def _():
        carry_scr[...] = jnp.zeros(carry_scr.shape, jnp.float32)

    mix = _dot(oa_ref[...], wo_ref[:DIL_WIDTH, :]) + _dot(od_ref[...], wo_ref[DIL_WIDTH:, :])
    x1 = _layer_norm(DEEPNORM_ALPHA * x_ref[...] + mix, g_ref[...], b_ref[...])
    x1_ref[...] = x1
    bits = pltpu.bitcast(x1.astype(jnp.bfloat16).astype(jnp.float32), jnp.uint32)
    half = x1.shape[1] // 2
    x1b_ref[...] = (bits[:, :half] >> 16) | (bits[:, half:] & jnp.uint32(0xFFFF0000))

    tm = x1.shape[0]
    lg = _dot_nt(wr_ref[...], x1, precision=lax.Precision.HIGHEST) + br_ref[...]
    eidx = lax.broadcasted_iota(jnp.int32, (N_EXPERTS, tm), 0)
    vals, sels = [], []
    for k in range(TOP_K):
        mx = jnp.max(lg, axis=0, keepdims=True)
        idx = jnp.min(jnp.where(lg == mx, eidx, N_EXPERTS), axis=0, keepdims=True)
        sel = eidx == idx
        vals.append(mx)
        sels.append(sel)
        idx_ref[k:k + 1, :] = idx
        lg = jnp.where(sel, -jnp.inf, lg)
    ex = [jnp.exp(v - vals[0]) for v in vals]
    den = ex[0] + ex[1] + ex[2] + ex[3]
    for k in range(TOP_K):
        gate_ref[k:k + 1, :] = ex[k] / den

    chosen = (sels[0] | sels[1] | sels[2] | sels[3])
    onehot = jnp.where(chosen, 1.0, 0.0)
    before = _dot(onehot.astype(jnp.bfloat16), tri_ref[...]) + carry_scr[...]
    for k in range(TOP_K):
        rank_ref[k:k + 1, :] = jnp.sum(jnp.where(sels[k], before, 0.0), axis=0,
                                       keepdims=True).astype(jnp.int32)
    carry_scr[...] = carry_scr[...] + jnp.sum(onehot, axis=1, keepdims=True)
    pad = jnp.zeros((8 - TOP_K, tm), jnp.int32)
    idx_ref[TOP_K:, :] = pad
    rank_ref[TOP_K:, :] = pad
    gate_ref[TOP_K:, :] = pad.astype(jnp.float32)
    cnt_ref[...] = jnp.broadcast_to(carry_scr[...], cnt_ref.shape).astype(jnp.int32)


def _post_attn(oa, od, x, wo_bf, g, b, wr_t, br, tri):
    T, D = x.shape
    tm = ROW_TILE
    row = lambda w: pl.BlockSpec((tm, w), lambda i: (i, 0))
    full = lambda a: pl.BlockSpec(a.shape, lambda i: (0, 0))
    col = pl.BlockSpec((8, tm), lambda i: (0, i))
    return pl.pallas_call(
        _post_attn_kernel,
        grid=(T // tm,),
        in_specs=[row(DIL_WIDTH), row(DIFF_WIDTH), row(D), full(wo_bf), full(g), full(b), full(wr_t),
                  full(br), full(tri)],
        out_specs=[row(D), row(D // 2), col, col, col, pl.BlockSpec((N_EXPERTS, LANES), lambda i: (0, 0))],
        out_shape=[
            jax.ShapeDtypeStruct((T, D), jnp.float32),
            jax.ShapeDtypeStruct((T, D // 2), jnp.uint32),
            jax.ShapeDtypeStruct((8, T), jnp.int32),
            jax.ShapeDtypeStruct((8, T), jnp.float32),
            jax.ShapeDtypeStruct((8, T), jnp.int32),
            jax.ShapeDtypeStruct((N_EXPERTS, LANES), jnp.int32),
        ],
        scratch_shapes=[pltpu.VMEM((N_EXPERTS, 1), jnp.float32)],
        compiler_params=_cparams(("arbitrary",)),
        name="post_attn_router",
    )(oa, od, x, wo_bf, g, b, wr_t, br, tri)


def _fetch_rows_index(dest_hbm, step, idx_smem, sem):
    cp = pltpu.make_async_copy(dest_hbm.at[step], idx_smem, sem)
    cp.start()
    cp.wait()


def _dispatch_kernel(dest_hbm, x_ref, xs_in_hbm, xs_hbm, idx_smem, idx_sem, row_sem):
    del xs_in_hbm
    tm = x_ref.shape[0]
    _fetch_rows_index(dest_hbm, pl.program_id(0), idx_smem, idx_sem)

    def row_copy(t, k):
        return pltpu.make_async_copy(x_ref.at[pl.ds(t, 1)], xs_hbm.at[pl.ds(idx_smem[k * tm + t], 1)], row_sem)

    def issue(t, carry):
        for k in range(TOP_K):
            row_copy(t, k).start(priority=k % 2)
        return carry

    lax.fori_loop(0, tm, issue, 0, unroll=8)
    for k in range(TOP_K):
        pltpu.make_async_copy(x_ref, xs_hbm.at[pl.ds(0, tm)], row_sem).wait()


def _dispatch(dest_steps, x1p, n_rows):
    T, W = x1p.shape
    n_steps, per_step = dest_steps.shape
    tm = per_step // TOP_K
    xs0 = jnp.zeros((n_rows, W), x1p.dtype)
    return pl.pallas_call(
        _dispatch_kernel,
        grid=(n_steps,),
        in_specs=[
            pl.BlockSpec(memory_space=pl.ANY),
            pl.BlockSpec((tm, W), lambda i: (i, 0)),
            pl.BlockSpec(memory_space=pl.ANY),
        ],
        out_specs=pl.BlockSpec(memory_space=pl.ANY),
        out_shape=jax.ShapeDtypeStruct((n_rows, W), x1p.dtype),
        scratch_shapes=[pltpu.SMEM((per_step,), jnp.int32), pltpu.SemaphoreType.DMA(()),
                        pltpu.SemaphoreType.DMA(())],
        input_output_aliases={2: 0},
        compiler_params=_cparams(("arbitrary",)),
        name="dispatch_rows",
    )(dest_steps, x1p, xs0)


def _expert_kernel(tile_e_ref, n_used_ref, xs_ref, wu_ref, bu_ref, wd_ref, bd_ref, y_ref, wu_scr, wd_scr):
    i = pl.program_id(0)
    used = i < n_used_ref[0]
    new_expert = (i == 0) | (tile_e_ref[i] != tile_e_ref[jnp.maximum(i - 1, 0)])

    @pl.when(used & new_expert)
    def _():
        wu_scr[...] = wu_ref[0].astype(jnp.bfloat16)
        wd_scr[...] = wd_ref[0].astype(jnp.bfloat16)

    @pl.when(used)
    def _():
        w = xs_ref[...]
        half = w.shape[1]
        lo = pltpu.bitcast(w << 16, jnp.float32).astype(jnp.bfloat16)
        hi = pltpu.bitcast(w & jnp.uint32(0xFFFF0000), jnp.float32).astype(jnp.bfloat16)
        hu = _dot(lo, wu_scr[:half, :]) + _dot(hi, wu_scr[half:, :]) + bu_ref[0]
        g = jnp.minimum(hu[:, :D_FF], SWIGLU_LIMIT)
        u = jnp.clip(hu[:, D_FF:], -SWIGLU_LIMIT, SWIGLU_LIMIT)
        act = g * (1.0 / (1.0 + jnp.exp(-SWIGLU_ALPHA * g))) * (u + 1.0)
        y_ref[...] = _dot(act.astype(jnp.bfloat16), wd_scr[...]) + bd_ref[0]

    @pl.when(jnp.logical_not(used))
    def _():
        y_ref[...] = jnp.zeros(y_ref.shape, y_ref.dtype)


def _experts(tile_e, n_used, xs, wu, bu, wd, bd):
    n_rows, W = xs.shape
    D = 2 * W
    tm = MOE_TILE
    grid_spec = pltpu.PrefetchScalarGridSpec(
        num_scalar_prefetch=2,
        grid=(n_rows // tm,),
        in_specs=[
            pl.BlockSpec((tm, W), lambda i, te, nu: (i, 0)),
            pl.BlockSpec((1, D, 2 * D_FF), lambda i, te, nu: (te[i], 0, 0)),
            pl.BlockSpec((1, 1, 2 * D_FF), lambda i, te, nu: (te[i], 0, 0)),
            pl.BlockSpec((1, D_FF, D), lambda i, te, nu: (te[i], 0, 0)),
            pl.BlockSpec((1, 1, D), lambda i, te, nu: (te[i], 0, 0)),
        ],
        out_specs=pl.BlockSpec((tm, D), lambda i, te, nu: (i, 0)),
        scratch_shapes=[pltpu.VMEM((D, 2 * D_FF), jnp.bfloat16), pltpu.VMEM((D_FF, D), jnp.bfloat16)],
    )
    return pl.pallas_call(
        _expert_kernel,
        grid_spec=grid_spec,
        out_shape=jax.ShapeDtypeStruct((n_rows, D), jnp.float32),
        compiler_params=_cparams(("arbitrary",)),
        name="experts",
    )(tile_e, n_used, xs, wu, bu, wd, bd)


def _combine_kernel(dest_hbm, y_hbm, x1_ref, gate_ref, g_ref, b_ref, o_ref, buf, idx_smem, idx_sem, row_sems):
    i = pl.program_id(0)
    n = pl.num_programs(0)
    tm = x1_ref.shape[0]

    def issue(step, slot):
        _fetch_rows_index(dest_hbm, step, idx_smem, idx_sem)

        def body(t, carry):
            for k in range(TOP_K):
                pltpu.make_async_copy(y_hbm.at[pl.ds(idx_smem[k * tm + t], 1)],
                                      buf.at[slot, k, pl.ds(t, 1)], row_sems.at[slot]).start(priority=k % 2)
            return carry

        lax.fori_loop(0, tm, body, 0, unroll=8)

    @pl.when(i == 0)
    def _():
        issue(0, 0)

    @pl.when(i + 1 < n)
    def _():
        issue(i + 1, (i + 1) % 2)

    slot = i % 2
    for k in range(TOP_K):
        pltpu.make_async_copy(y_hbm.at[pl.ds(0, tm)], buf.at[slot, k], row_sems.at[slot]).wait()
    ffn = buf[slot, 0] * gate_ref[:, 0:1]
    for k in range(1, TOP_K):
        ffn = ffn + buf[slot, k] * gate_ref[:, k:k + 1]
    o_ref[...] = _layer_norm(DEEPNORM_ALPHA * x1_ref[...] + ffn, g_ref[...], b_ref[...])


def _combine(dest_steps, y, x1, gates_t, g, b):
    T, D = x1.shape
    n_steps, per_step = dest_steps.shape
    tm = per_step // TOP_K
    full = lambda a: pl.BlockSpec(a.shape, lambda i: (0, 0))
    return pl.pallas_call(
        _combine_kernel,
        grid=(n_steps,),
        in_specs=[
            pl.BlockSpec(memory_space=pl.ANY),
            pl.BlockSpec(memory_space=pl.ANY),
            pl.BlockSpec((tm, D), lambda i: (i, 0)),
            pl.BlockSpec((tm, 8), lambda i: (i, 0)),
            full(g), full(b),
        ],
        out_specs=pl.BlockSpec((tm, D), lambda i: (i, 0)),
        out_shape=jax.ShapeDtypeStruct((T, D), jnp.float32),
        scratch_shapes=[pltpu.VMEM((2, TOP_K, tm, D), jnp.float32), pltpu.SMEM((per_step,), jnp.int32),
                        pltpu.SemaphoreType.DMA(()), pltpu.SemaphoreType.DMA((2,))],
        compiler_params=_cparams(("arbitrary",)),
        name="combine_ln2",
    )(dest_steps, y, x1, gates_t, g, b)


def kernel(x, w_in, w_out, lambda_q1, lambda_k1, lambda_q2, lambda_k2, diff_norm_g, ln1_g, ln1_b,
           w_router, b_router, w_up, b_up, w_down, b_down, ln2_g, ln2_b):
    B, S, D = x.shape
    T = B * S
    assert w_in.shape[0] == 1, "single layer"
    assert S % DIFF_TQ == 0 and DIFF_TK == DIFF_TQ and DIFF_TK % ROW_TILE == 0 and S % (16 * DIL_TQ) == 0

    na = 3 * DIL_WIDTH
    colscale = np.ones((3 * D,), np.float32)
    colscale[:DIL_WIDTH] = HEAD_DIM ** -0.5
    colscale[na:na + DIFF_WIDTH] = HEAD_DIM ** -0.5 * LOG2E
    w_in_bf = (w_in[0] * colscale).astype(jnp.bfloat16)
    nv = na + 2 * DIFF_WIDTH

    kpos, qpos, dabs = _position_columns()
    pa, qd, ka, vt, qn, kn = _in_proj(x, w_in_bf[:, :nv], w_in_bf[:, nv:].T, kpos)

    stats = None
    for n, (_, dil) in enumerate(DIL_PATTERNS):
        final = n == len(DIL_PATTERNS) - 1
        stats = _dilated_stage(pa, _dilated_bias(dil), dil, stats, final)
    oa = stats

    r2 = lambda a: a.reshape(1, -1).astype(jnp.float32)
    od = _diff_attention(_tile_reach(qn, kn, S // DIFF_TK), qd, ka, vt, qpos, dabs, r2(lambda_q1[0]), r2(lambda_k1[0]), r2(lambda_q2[0]),
                         r2(lambda_k2[0]), diff_norm_g[0].reshape(-1, 1).astype(jnp.float32))

    tri = jnp.asarray(np.triu(np.ones((ROW_TILE, ROW_TILE), np.float32), k=1), jnp.bfloat16)
    x1, x1p, top_idx, gates, rank, counts = _post_attn(
        oa.reshape(T, DIL_WIDTH), od.reshape(T, DIFF_WIDTH), x.reshape(T, D), w_out[0].astype(jnp.bfloat16),
        r2(ln1_g[0]), r2(ln1_b[0]), w_router[0].T, b_router[0].reshape(N_EXPERTS, 1), tri)

    counts = counts[:, 0]
    pcounts = ((counts + MOE_TILE - 1) // MOE_TILE) * MOE_TILE
    pends = jnp.cumsum(pcounts)
    pstarts = pends - pcounts
    n_rows = T * TOP_K + N_EXPERTS * MOE_TILE
    n_tiles = n_rows // MOE_TILE
    dest = rank[:TOP_K]
    for e in range(N_EXPERTS):
        dest = dest + jnp.where(top_idx[:TOP_K] == e, pstarts[e], 0)
    tile_starts = jnp.arange(n_tiles, dtype=jnp.int32) * MOE_TILE
    tile_e = jnp.minimum(jnp.sum(pends[None, :] <= tile_starts[:, None], axis=1), N_EXPERTS - 1).astype(jnp.int32)
    n_used = (pends[-1] // MOE_TILE).astype(jnp.int32).reshape(1)

    def per_step(tm):
        return dest.reshape(TOP_K, T // tm, tm).transpose(1, 0, 2).reshape(T // tm, TOP_K * tm)

    xs = _dispatch(per_step(ROW_TILE), x1p, n_rows)
    y = _experts(tile_e, n_used, xs, w_up[0], b_up[0][:, None, :], w_down[0], b_down[0][:, None, :])
    out = _combine(per_step(COMBINE_TILE), y, x1, gates.T, r2(ln2_g[0]), r2(ln2_b[0]))
    return out.reshape(B, S, D)
```

```python
import functools
import math

import jax
import jax.numpy as jnp
import numpy as np
from jax import lax
from jax.experimental import pallas as pl
from jax.experimental.pallas import tpu as pltpu

D_MODEL = 1024
HEAD_DIM = 64
DIL_WIDTH = 512
N_HEADS_DIL = 8
DIL_PATTERNS = ((128, 1), (512, 4), (2048, 16))
DIL_SIDE = 64
DIFF_WIDTH = 512
N_HEADS_DIFF = 4
DIFF_VDIM = 2 * HEAD_DIM
N_EXPERTS = 32
TOP_K = 4
D_FF = D_MODEL
SWIGLU_ALPHA = 1.702
SWIGLU_LIMIT = 7.0
DEEPNORM_ALPHA = 2.0 ** 0.25
LN_EPS = 1e-5
NEG_INF = -1e30
LAM_INIT = 0.8 - 0.6 * math.exp(-0.3 * 0)
LOG2E = math.log2(math.e)

LANES = 128
VMEM_LIMIT = 56 * 1024 * 1024

ROW_TILE = 512
DIFF_TQ = 1024
DIFF_TK = 1024
V_ROWS = DIFF_VDIM + 16
EXP2_HEADROOM = 4.0
UNDERFLOW_LOG2 = 160.0
NORM_SLACK = 1.01
DIL_TQ = 128
MOE_TILE = 512
COMBINE_TILE = 256
POS_SPLIT = 16


def _cparams(sem):
    return pltpu.CompilerParams(dimension_semantics=sem, vmem_limit_bytes=VMEM_LIMIT)


def _dot_nt(a, b, **kw):
    return lax.dot_general(a, b, (((1,), (1,)), ((), ())), preferred_element_type=jnp.float32, **kw)


def _dot(a, b, **kw):
    return jnp.dot(a, b, preferred_element_type=jnp.float32, **kw)


def _layer_norm(y, g, b):
    mu = jnp.mean(y, axis=-1, keepdims=True)
    yc = y - mu
    var = jnp.mean(yc * yc, axis=-1, keepdims=True)
    return yc * lax.rsqrt(var + LN_EPS) * g + b


def _in_proj_kernel(x_ref, w_ref, wvt_ref, kpos_ref, grp_ref, pa_ref, pa4_ref, pa16_ref, qd_ref, ka_ref,
                    vt_ref, qn_ref, kn_ref, pa_scr):
    xb = x_ref[0].astype(jnp.bfloat16)
    na = 3 * DIL_WIDTH
    tm = xb.shape[0]
    paf = _dot(xb, w_ref[:, :na])
    pa_ref[0] = paf.astype(jnp.bfloat16)
    for c in range(na // LANES):
        cols = slice(c * LANES, (c + 1) * LANES)
        pa_scr[c] = paf[:, cols]
        for d, ref in ((DIL_PATTERNS[1][1], pa4_ref), (DIL_PATTERNS[2][1], pa16_ref)):
            for r in range(d):
                ref[0, r, :, cols] = pa_scr[c, pl.ds(r, tm // d, stride=d), :].astype(jnp.bfloat16)
    q = _dot(xb, w_ref[:, na:na + DIFF_WIDTH]).astype(jnp.bfloat16)
    k = _dot(xb, w_ref[:, na + DIFF_WIDTH:]).astype(jnp.bfloat16)
    vt = _dot_nt(wvt_ref[...], xb).astype(jnp.bfloat16)
    ones = jnp.ones((V_ROWS - DIFF_VDIM, tm), jnp.bfloat16)

    @pl.when(pl.program_id(1) == 0)
    def _():
        qn_ref[...] = jnp.zeros(qn_ref.shape, jnp.float32)
        kn_ref[...] = jnp.zeros(kn_ref.shape, jnp.float32)

    for src, dst in ((q, qn_ref), (k, kn_ref)):
        f = src.astype(jnp.float32)
        sq = f * f
        sq_hi = sq.astype(jnp.bfloat16)
        sq_lo = (sq - sq_hi.astype(jnp.float32)).astype(jnp.bfloat16)
        gs = _dot(sq_hi, grp_ref[...]) + _dot(sq_lo, grp_ref[...])
        dst[0] = jnp.maximum(dst[0], jnp.max(gs, axis=0, keepdims=True))
    for h in range(N_HEADS_DIFF):
        sl = slice(h * LANES, (h + 1) * LANES)
        qd_ref[0, h] = q[:, sl]
        ka_ref[0, h, :, :LANES] = k[:, sl]
        ka_ref[0, h, :, LANES:] = kpos_ref[h]
        vt_ref[0, h, :DIFF_VDIM, :] = vt[sl, :]
        vt_ref[0, h, DIFF_VDIM:, :] = ones


def _in_proj(x, w_bf, wvt_bf, kpos):
    B, S, D = x.shape
    tm = ROW_TILE
    na = 3 * DIL_WIDTH
    grp = np.zeros((DIFF_WIDTH, LANES), np.float32)
    grp[np.arange(DIFF_WIDTH), np.arange(DIFF_WIDTH) // HEAD_DIM] = 1.0
    grp = jnp.asarray(grp, jnp.bfloat16)
    norm_spec = pl.BlockSpec((1, 1, LANES), lambda b, i: (b, 0, 0))
    d4, d16 = DIL_PATTERNS[1][1], DIL_PATTERNS[2][1]
    return pl.pallas_call(
        _in_proj_kernel,
        grid=(B, S // tm),
        in_specs=[
            pl.BlockSpec((1, tm, D), lambda b, i: (b, i, 0)),
            pl.BlockSpec(w_bf.shape, lambda b, i: (0, 0)),
            pl.BlockSpec(wvt_bf.shape, lambda b, i: (0, 0)),
            pl.BlockSpec((N_HEADS_DIFF, tm, LANES), lambda b, i: (0, i % (DIFF_TK // tm), 0)),
            pl.BlockSpec(grp.shape, lambda b, i: (0, 0)),
        ],
        out_specs=[
            pl.BlockSpec((1, tm, na), lambda b, i: (b, i, 0)),
            pl.BlockSpec((1, d4, tm // d4, na), lambda b, i: (b, 0, i, 0)),
            pl.BlockSpec((1, d16, tm // d16, na), lambda b, i: (b, 0, i, 0)),
            pl.BlockSpec((1, N_HEADS_DIFF, tm, LANES), lambda b, i: (b, 0, i, 0)),
            pl.BlockSpec((1, N_HEADS_DIFF, tm, 2 * LANES), lambda b, i: (b, 0, i, 0)),
            pl.BlockSpec((1, N_HEADS_DIFF, V_ROWS, tm), lambda b, i: (b, 0, 0, i)),
            norm_spec, norm_spec,
        ],
        out_shape=[
            jax.ShapeDtypeStruct((B, S, na), jnp.bfloat16),
            jax.ShapeDtypeStruct((B, d4, S // d4, na), jnp.bfloat16),
            jax.ShapeDtypeStruct((B, d16, S // d16, na), jnp.bfloat16),
            jax.ShapeDtypeStruct((B, N_HEADS_DIFF, S, LANES), jnp.bfloat16),
            jax.ShapeDtypeStruct((B, N_HEADS_DIFF, S, 2 * LANES), jnp.bfloat16),
            jax.ShapeDtypeStruct((B, N_HEADS_DIFF, V_ROWS, S), jnp.bfloat16),
            jax.ShapeDtypeStruct((B, 1, LANES), jnp.float32),
            jax.ShapeDtypeStruct((B, 1, LANES), jnp.float32),
        ],
        scratch_shapes=[pltpu.VMEM((na // LANES, tm, LANES), jnp.float32)],
        compiler_params=_cparams(("parallel", "arbitrary")),
        name="in_proj",
    )(x, w_bf, wvt_bf, kpos, grp)


def _dilated_kernel(q_ref, k0_ref, k1_ref, k2_ref, k3_ref, v0_ref, v1_ref, v2_ref, v3_ref, bias_ref,
                    o_ref, lse_ref, *, seq_len):
    tq = q_ref.shape[2]
    nk = tq + 2 * DIL_SIDE
    i = pl.program_id(2)
    q = q_ref[0, 0]
    kwin = jnp.concatenate([r[0, 0] for r in (k0_ref, k1_ref, k2_ref, k3_ref)], axis=0)
    vwin = jnp.concatenate([r[0, 0] for r in (v0_ref, v1_ref, v2_ref, v3_ref)], axis=0)
    key_pos = i * tq - DIL_SIDE + lax.broadcasted_iota(jnp.int32, (1, nk), 1)
    edge = jnp.where((key_pos >= 0) & (key_pos < seq_len), 0.0, NEG_INF)
    low_half = lax.broadcasted_iota(jnp.int32, (tq, LANES), 1) < HEAD_DIM
    ss = []
    for h in range(N_HEADS_DIL):
        sl = slice((h // 2) * LANES, (h // 2 + 1) * LANES)
        qm = jnp.where(low_half == (h % 2 == 0), q[:, sl], jnp.zeros((tq, LANES), q.dtype))
        ss.append(_dot_nt(qm, kwin[:, sl]) + bias_ref[h])
    s = jnp.concatenate(ss, axis=0) + edge
    m = jnp.max(s, axis=-1, keepdims=True)
    p = jnp.exp(s - m)
    l = jnp.sum(p, axis=-1, keepdims=True)
    lse = m + jnp.log(l)
    inv_l = 1.0 / l
    pb = p.astype(jnp.bfloat16)
    for pair in range(N_HEADS_DIL // 2):
        sl = slice(pair * LANES, (pair + 1) * LANES)
        r0 = slice(2 * pair * tq, (2 * pair + 1) * tq)
        r1 = slice((2 * pair + 1) * tq, (2 * pair + 2) * tq)
        a0 = _dot(pb[r0], vwin[:, sl]) * inv_l[r0]
        a1 = _dot(pb[r1], vwin[:, sl]) * inv_l[r1]
        o_ref[0, 0, :, sl] = jnp.where(low_half, a0, a1).astype(o_ref.dtype)
        lse_ref[0, 0, :, sl] = jnp.where(low_half, lse[r0], lse[r1])


def _dilated_stage(pa_d, dilation):
    B, d, L, _ = pa_d.shape
    tq = DIL_TQ
    w = DIL_WIDTH
    halo = DIL_SIDE
    n_halo = L // halo
    bias = _dilated_bias(dilation)

    def halo_spec(col, j):
        def imap(b, r, i):
            return (b, r, jnp.clip(i * (tq // halo) - 1 + j, 0, n_halo - 1), col)
        return pl.BlockSpec((1, 1, halo, w), imap)

    out_spec = pl.BlockSpec((1, 1, tq, w), lambda b, r, i: (b, r, i, 0))
    in_specs = ([pl.BlockSpec((1, 1, tq, w), lambda b, r, i: (b, r, i, 0))]
                + [halo_spec(1, j) for j in range(4)] + [halo_spec(2, j) for j in range(4)]
                + [pl.BlockSpec(bias.shape, lambda b, r, i: (0, 0, 0))])
    return pl.pallas_call(
        functools.partial(_dilated_kernel, seq_len=L),
        grid=(B, d, L // tq),
        in_specs=in_specs,
        out_specs=[out_spec, out_spec],
        out_shape=[jax.ShapeDtypeStruct((B, d, L, w), jnp.bfloat16),
                   jax.ShapeDtypeStruct((B, d, L, w), jnp.float32)],
        compiler_params=_cparams(("parallel", "parallel", "parallel")),
        name=f"dilated_d{d}",
    )(*([pa_d] * 9), bias)


def _dilated_bias(dilation):
    tq = DIL_TQ
    slopes = np.asarray([2.0 ** (-8.0 * (i + 1) / N_HEADS_DIL) for i in range(N_HEADS_DIL)], np.float32)
    rel = (np.arange(tq + 2 * DIL_SIDE)[None, :] - DIL_SIDE) - np.arange(tq)[:, None]
    band = np.abs(rel) <= DIL_SIDE
    pen = -(slopes * dilation)[:, None, None] * np.abs(rel).astype(np.float32)[None]
    return jnp.asarray(np.where(band[None], pen, np.float32(NEG_INF)), jnp.float32)


def _diff_kernel(reach_ref, q_ref, ka_ref, vt_ref, qpos_ref, dabs_ref, lq1_ref, lk1_ref, lq2_ref, lk2_ref,
                 g_ref, o_ref, qa_scr, m_scr, acc_scr, tmp_scr, *, slopes):
    b = pl.program_id(0)
    h = pl.program_id(1)
    qi = pl.program_id(2)
    ki = pl.program_id(3)
    nk = pl.num_programs(3)
    tq = q_ref.shape[2]
    tk = ka_ref.shape[2]

    slope = jnp.float32(slopes[0])
    for hh in range(1, N_HEADS_DIFF):
        slope = jnp.where(h == hh, jnp.float32(slopes[hh]), slope)

    @pl.when(ki == 0)
    def _():
        q = q_ref[0, 0]
        low_half = lax.broadcasted_iota(jnp.int32, (tq, LANES), 1) < HEAD_DIM
        zero = jnp.zeros_like(q)
        qpos = qpos_ref[0]
        for c in range(2):
            qc = jnp.where(low_half == (c == 0), q, zero)
            for var, pos in enumerate((qpos, -qpos, jnp.zeros_like(qpos))):
                qa_scr[c, var, :, :LANES] = qc
                qa_scr[c, var, :, LANES:] = pos
        m_scr[...] = jnp.full(m_scr.shape, NEG_INF, jnp.float32)
        acc_scr[...] = jnp.zeros(acc_scr.shape, jnp.float32)

    kt = lax.rem(qi + ki, nk)
    var = jnp.where(kt < qi, 0, jnp.where(kt > qi, 1, 2))
    off = -slope * jnp.abs(qi * tq - kt * tk).astype(jnp.float32)
    ka = ka_ref[0, 0]
    vt = vt_ref[0, 0]


    def restabilise(diag):
        ss = []
        for c in range(2):
            s = _dot_nt(ka, qa_scr[c, var])
            if diag:
                s = s + dabs_ref[...] * slope
            ss.append(s)
        for c in range(2):
            s = ss[c]
            m_old = m_scr[c]
            m_new = jnp.maximum(m_old, jnp.max(s, axis=0, keepdims=True) + off)
            alpha = jnp.exp2(m_old - m_new)
            p = jnp.exp2((s - (m_new - off)).astype(jnp.bfloat16))
            acc_scr[c] = alpha * acc_scr[c] + _dot(vt, p)
            m_scr[c] = m_new

    @pl.when(ki == 0)
    def _():
        restabilise(True)

    @pl.when((ki != 0) & (jnp.abs(kt - qi) <= reach_ref[b * N_HEADS_DIFF + h]))
    def _():
        excess = None
        for c in range(2):
            s = _dot_nt(ka, qa_scr[c, var])
            m_eff = m_scr[c] - off
            over = jnp.max(jnp.max(s, axis=0, keepdims=True) - m_eff)
            excess = over if excess is None else jnp.maximum(excess, over)
            p = jnp.exp2((s - m_eff).astype(jnp.bfloat16))
            tmp_scr[c] = acc_scr[c] + _dot(vt, p)

        @pl.when(excess <= EXP2_HEADROOM)
        def _():
            acc_scr[...] = tmp_scr[...]

        @pl.when(excess > EXP2_HEADROOM)
        def _():
            restabilise(False)

    @pl.when(ki == nk - 1)
    def _():
        lam = (jnp.exp(jnp.sum(lq1_ref[...] * lk1_ref[...], axis=-1, keepdims=True))
               - jnp.exp(jnp.sum(lq2_ref[...] * lk2_ref[...], axis=-1, keepdims=True)) + LAM_INIT)
        a1, a2 = acc_scr[0], acc_scr[1]
        o = (a1[:DIFF_VDIM] / a1[DIFF_VDIM:DIFF_VDIM + 1]
             - lam * (a2[:DIFF_VDIM] / a2[DIFF_VDIM:DIFF_VDIM + 1]))
        o = o * lax.rsqrt(jnp.mean(o * o, axis=0, keepdims=True) + LN_EPS) * g_ref[...]
        o_ref[0] = (o * (1.0 - LAM_INIT)).T.astype(o_ref.dtype)


def _tile_reach(qn, kn, n_tiles):
    B = qn.shape[0]
    ng = 2 * N_HEADS_DIFF
    qmax = jnp.sqrt(qn[:, 0, :ng]) * NORM_SLACK
    kmax = jnp.sqrt(kn[:, 0, :ng]) * NORM_SLACK
    bound = jnp.max((2.0 * qmax * kmax).reshape(B, N_HEADS_DIFF, 2), axis=-1) + UNDERFLOW_LOG2
    far = bound / jnp.asarray(_diff_slopes_log2())[None, :]
    reach = jnp.floor((far - 1.0) / DIFF_TK) + 1.0
    reach = jnp.where(jnp.isfinite(reach), reach, n_tiles)
    return jnp.clip(reach, 0, n_tiles).astype(jnp.int32).reshape(-1)


def _diff_attention(reach, qd, ka, vt, qpos, dabs, lq1, lk1, lq2, lk2, g_col):
    B, H, S, _ = qd.shape
    tq, tk = DIFF_TQ, DIFF_TK
    slopes = tuple(float(s) for s in _diff_slopes_log2())
    nk = S // tk

    def key_tile(b, h, qi, ki, reach):
        kt = (qi + ki) % nk
        return jnp.where(jnp.abs(kt - qi) <= reach[b * H + h], kt, qi)

    small = lambda a: pl.BlockSpec(a.shape, lambda b, h, qi, ki, reach: (0, 0))
    grid_spec = pltpu.PrefetchScalarGridSpec(
        num_scalar_prefetch=1,
        grid=(B, H, S // tq, nk),
        in_specs=[
            pl.BlockSpec((1, 1, tq, LANES), lambda b, h, qi, ki, reach: (b, h, qi, 0)),
            pl.BlockSpec((1, 1, tk, 2 * LANES),
                         lambda b, h, qi, ki, reach: (b, h, key_tile(b, h, qi, ki, reach), 0)),
            pl.BlockSpec((1, 1, V_ROWS, tk),
                         lambda b, h, qi, ki, reach: (b, h, 0, key_tile(b, h, qi, ki, reach))),
            pl.BlockSpec((1, tq, LANES), lambda b, h, qi, ki, reach: (h, 0, 0)),
            small(dabs), small(lq1), small(lk1), small(lq2), small(lk2), small(g_col),
        ],
        out_specs=pl.BlockSpec((1, tq, LANES), lambda b, h, qi, ki, reach: (b, qi, h)),
        scratch_shapes=[
            pltpu.VMEM((2, 3, tq, 2 * LANES), jnp.bfloat16),
            pltpu.VMEM((2, 1, tq), jnp.float32),
            pltpu.VMEM((2, V_ROWS, tq), jnp.float32),
            pltpu.VMEM((2, V_ROWS, tq), jnp.float32),
        ],
    )
    return pl.pallas_call(
        functools.partial(_diff_kernel, slopes=slopes),
        grid_spec=grid_spec,
        out_shape=jax.ShapeDtypeStruct((B, S, H * LANES), jnp.bfloat16),
        compiler_params=_cparams(("parallel", "parallel", "parallel", "arbitrary")),
        name="diff_attention",
    )(reach, qd, ka, vt, qpos, dabs, lq1, lk1, lq2, lk2, g_col)


def _diff_slopes_log2():
    return np.asarray([2.0 ** (-8.0 * (i + 1) / N_HEADS_DIFF) for i in range(N_HEADS_DIFF)],
                      np.float32) * np.float32(LOG2E)


def _position_columns():
    def bf16_round(v):
        u = np.asarray(v, np.float32).view(np.uint32)
        return ((u + 0x7FFF + ((u >> 16) & 1)) & np.uint32(0xFFFF0000)).view(np.float32)

    kpos = np.zeros((N_HEADS_DIFF, DIFF_TK, LANES), np.float32)
    qpos = np.zeros((N_HEADS_DIFF, DIFF_TQ, LANES), np.float32)
    pk, pq = np.arange(DIFF_TK), np.arange(DIFF_TQ)
    for h, a in enumerate(_diff_slopes_log2()):
        rest = np.float32(a)
        for n in range(3):
            a_n = np.float32(bf16_round(rest))
            rest = np.float32(rest - a_n)
            c = 4 * n
            kpos[h, :, c + 0] = (pk // POS_SPLIT) * POS_SPLIT
            kpos[h, :, c + 1] = pk % POS_SPLIT
            kpos[h, :, c + 2] = a_n
            kpos[h, :, c + 3] = a_n
            qpos[h, :, c + 0] = a_n
            qpos[h, :, c + 1] = a_n
            qpos[h, :, c + 2] = -((pq // POS_SPLIT) * POS_SPLIT)
            qpos[h, :, c + 3] = -(pq % POS_SPLIT)
    dabs = -np.abs(pq[:, None] - pk[None, :]).astype(np.float32)
    return jnp.asarray(kpos, jnp.bfloat16), jnp.asarray(qpos, jnp.bfloat16), jnp.asarray(dabs)


def _post_attn_kernel(o1_ref, l1_ref, o4_ref, l4_ref, o16_ref, l16_ref, od_ref, x_ref, wo_ref, g_ref, b_ref,
                      wr_ref, br_ref, tri_ref,
                      x1_ref, x1b_ref, idx_ref, gate_ref, rank_ref, cnt_ref, carry_scr, order_scr):
    step = pl.program_id(0)

    @pl.when(step == 0)
    def _():
        carry_scr[...] = jnp.zeros(carry_scr.shape, jnp.float32)

    tm = x_ref.shape[0]
    nc = DIL_WIDTH // LANES
    for n, (o_ref, l_ref) in enumerate(((o4_ref, l4_ref), (o16_ref, l16_ref))):
        d = o_ref.shape[1]
        for r in range(d):
            o_r = o_ref[0, r].astype(jnp.float32)
            l_r = l_ref[0, r]
            for c in range(nc):
                cols = slice(c * LANES, (c + 1) * LANES)
                order_scr[2 * n, c, pl.ds(r, tm // d, stride=d), :] = o_r[:, cols]
                order_scr[2 * n + 1, c, pl.ds(r, tm // d, stride=d), :] = l_r[:, cols]
    in_order = lambda n: jnp.concatenate([order_scr[n, c] for c in range(nc)], axis=1)
    outs = (o1_ref[0, 0].astype(jnp.float32), in_order(0), in_order(2))
    lses = (l1_ref[0, 0], in_order(1), in_order(3))
    top = jnp.maximum(jnp.maximum(lses[0], lses[1]), lses[2])
    wts = [jnp.exp(l - top) for l in lses]
    oa = ((wts[0] * outs[0] + wts[1] * outs[1] + wts[2] * outs[2])
          / (wts[0] + wts[1] + wts[2])).astype(jnp.bfloat16)

    mix = _dot(oa, wo_ref[:DIL_WIDTH, :]) + _dot(od_ref[...], wo_ref[DIL_WIDTH:, :])
    x1 = _layer_norm(DEEPNORM_ALPHA * x_ref[...] + mix, g_ref[...], b_ref[...])
    x1_ref[...] = x1
    bits = pltpu.bitcast(x1.astype(jnp.bfloat16).astype(jnp.float32), jnp.uint32)
    half = x1.shape[1] // 2
    x1b_ref[...] = (bits[:, :half] >> 16) | (bits[:, half:] & jnp.uint32(0xFFFF0000))

    tm = x1.shape[0]
    lg = _dot_nt(wr_ref[...], x1, precision=lax.Precision.HIGHEST) + br_ref[...]
    eidx = lax.broadcasted_iota(jnp.int32, (N_EXPERTS, tm), 0)
    vals, sels = [], []
    for k in range(TOP_K):
        mx = jnp.max(lg, axis=0, keepdims=True)
        idx = jnp.min(jnp.where(lg == mx, eidx, N_EXPERTS), axis=0, keepdims=True)
        sel = eidx == idx
        vals.append(mx)
        sels.append(sel)
        idx_ref[k:k + 1, :] = idx
        lg = jnp.where(sel, -jnp.inf, lg)
    ex = [jnp.exp(v - vals[0]) for v in vals]
    den = ex[0] + ex[1] + ex[2] + ex[3]
    for k in range(TOP_K):
        gate_ref[k:k + 1, :] = ex[k] / den

    chosen = (sels[0] | sels[1] | sels[2] | sels[3])
    onehot = jnp.where(chosen, 1.0, 0.0)
    before = _dot(onehot.astype(jnp.bfloat16), tri_ref[...]) + carry_scr[...]
    for k in range(TOP_K):
        rank_ref[k:k + 1, :] = jnp.sum(jnp.where(sels[k], before, 0.0), axis=0,
                                       keepdims=True).astype(jnp.int32)
    carry_scr[...] = carry_scr[...] + jnp.sum(onehot, axis=1, keepdims=True)
    pad = jnp.zeros((8 - TOP_K, tm), jnp.int32)
    idx_ref[TOP_K:, :] = pad
    rank_ref[TOP_K:, :] = pad
    gate_ref[TOP_K:, :] = pad.astype(jnp.float32)
    cnt_ref[...] = jnp.broadcast_to(carry_scr[...], cnt_ref.shape).astype(jnp.int32)


def _post_attn(dil, od, x, wo_bf, g, b, wr_t, br, tri):
    T, D = x.shape
    tm = ROW_TILE
    nb = dil[0][0].shape[2] // tm
    row = lambda w: pl.BlockSpec((tm, w), lambda i: (i, 0))
    full = lambda a: pl.BlockSpec(a.shape, lambda i: (0, 0))
    col = pl.BlockSpec((8, tm), lambda i: (0, i))
    dil_specs, dil_args = [], []
    for o_d, lse_d in dil:
        d = o_d.shape[1]
        spec = pl.BlockSpec((1, d, tm // d, DIL_WIDTH), lambda i: (i // nb, 0, i % nb, 0))
        dil_specs += [spec, spec]
        dil_args += [o_d, lse_d]
    return pl.pallas_call(
        _post_attn_kernel,
        grid=(T // tm,),
        in_specs=dil_specs + [row(DIFF_WIDTH), row(D), full(wo_bf), full(g), full(b), full(wr_t),
                              full(br), full(tri)],
        out_specs=[row(D), row(D // 2), col, col, col, pl.BlockSpec((N_EXPERTS, LANES), lambda i: (0, 0))],
        out_shape=[
            jax.ShapeDtypeStruct((T, D), jnp.float32),
            jax.ShapeDtypeStruct((T, D // 2), jnp.uint32),
            jax.ShapeDtypeStruct((8, T), jnp.int32),
            jax.ShapeDtypeStruct((8, T), jnp.float32),
            jax.ShapeDtypeStruct((8, T), jnp.int32),
            jax.ShapeDtypeStruct((N_EXPERTS, LANES), jnp.int32),
        ],
        scratch_shapes=[pltpu.VMEM((N_EXPERTS, 1), jnp.float32),
                        pltpu.VMEM((4, DIL_WIDTH // LANES, tm, LANES), jnp.float32)],
        compiler_params=_cparams(("arbitrary",)),
        name="post_attn_router",
    )(*dil_args, od, x, wo_bf, g, b, wr_t, br, tri)


def _fetch_rows_index(dest_hbm, step, idx_smem, sem):
    cp = pltpu.make_async_copy(dest_hbm.at[step], idx_smem, sem)
    cp.start()
    cp.wait()


def _dispatch_kernel(dest_hbm, x_ref, xs_in_hbm, xs_hbm, idx_smem, idx_sem, row_sem):
    del xs_in_hbm
    tm = x_ref.shape[0]
    _fetch_rows_index(dest_hbm, pl.program_id(0), idx_smem, idx_sem)

    def row_copy(t, k):
        return pltpu.make_async_copy(x_ref.at[pl.ds(t, 1)], xs_hbm.at[pl.ds(idx_smem[k * tm + t], 1)], row_sem)

    def issue(t, carry):
        for k in range(TOP_K):
            row_copy(t, k).start(priority=k % 2)
        return carry

    lax.fori_loop(0, tm, issue, 0, unroll=8)
    for k in range(TOP_K):
        pltpu.make_async_copy(x_ref, xs_hbm.at[pl.ds(0, tm)], row_sem).wait()


def _dispatch(dest_steps, x1p, n_rows):
    T, W = x1p.shape
    n_steps, per_step = dest_steps.shape
    tm = per_step // TOP_K
    xs0 = jnp.zeros((n_rows, W), x1p.dtype)
    return pl.pallas_call(
        _dispatch_kernel,
        grid=(n_steps,),
        in_specs=[
            pl.BlockSpec(memory_space=pl.ANY),
            pl.BlockSpec((tm, W), lambda i: (i, 0)),
            pl.BlockSpec(memory_space=pl.ANY),
        ],
        out_specs=pl.BlockSpec(memory_space=pl.ANY),
        out_shape=jax.ShapeDtypeStruct((n_rows, W), x1p.dtype),
        scratch_shapes=[pltpu.SMEM((per_step,), jnp.int32), pltpu.SemaphoreType.DMA(()),
                        pltpu.SemaphoreType.DMA(())],
        input_output_aliases={2: 0},
        compiler_params=_cparams(("arbitrary",)),
        name="dispatch_rows",
    )(dest_steps, x1p, xs0)


def _expert_kernel(tile_e_ref, n_used_ref, xs_ref, wu_ref, bu_ref, wd_ref, bd_ref, y_ref, wu_scr, wd_scr):
    i = pl.program_id(0)
    used = i < n_used_ref[0]
    new_expert = (i == 0) | (tile_e_ref[i] != tile_e_ref[jnp.maximum(i - 1, 0)])

    @pl.when(used & new_expert)
    def _():
        wu_scr[...] = wu_ref[0].astype(jnp.bfloat16)
        wd_scr[...] = wd_ref[0].astype(jnp.bfloat16)

    @pl.when(used)
    def _():
        w = xs_ref[...]
        half = w.shape[1]
        lo = pltpu.bitcast(w << 16, jnp.float32).astype(jnp.bfloat16)
        hi = pltpu.bitcast(w & jnp.uint32(0xFFFF0000), jnp.float32).astype(jnp.bfloat16)
        hu = _dot(lo, wu_scr[:half, :]) + _dot(hi, wu_scr[half:, :]) + bu_ref[0]
        g = jnp.minimum(hu[:, :D_FF], SWIGLU_LIMIT)
        u = jnp.clip(hu[:, D_FF:], -SWIGLU_LIMIT, SWIGLU_LIMIT)
        act = g * (1.0 / (1.0 + jnp.exp(-SWIGLU_ALPHA * g))) * (u + 1.0)
        y_ref[...] = _dot(act.astype(jnp.bfloat16), wd_scr[...]) + bd_ref[0]

    @pl.when(jnp.logical_not(used))
    def _():
        y_ref[...] = jnp.zeros(y_ref.shape, y_ref.dtype)


def _experts(tile_e, n_used, xs, wu, bu, wd, bd):
    n_rows, W = xs.shape
    D = 2 * W
    tm = MOE_TILE
    grid_spec = pltpu.PrefetchScalarGridSpec(
        num_scalar_prefetch=2,
        grid=(n_rows // tm,),
        in_specs=[
            pl.BlockSpec((tm, W), lambda i, te, nu: (i, 0)),
            pl.BlockSpec((1, D, 2 * D_FF), lambda i, te, nu: (te[i], 0, 0)),
            pl.BlockSpec((1, 1, 2 * D_FF), lambda i, te, nu: (te[i], 0, 0)),
            pl.BlockSpec((1, D_FF, D), lambda i, te, nu: (te[i], 0, 0)),
            pl.BlockSpec((1, 1, D), lambda i, te, nu: (te[i], 0, 0)),
        ],
        out_specs=pl.BlockSpec((tm, D), lambda i, te, nu: (i, 0)),
        scratch_shapes=[pltpu.VMEM((D, 2 * D_FF), jnp.bfloat16), pltpu.VMEM((D_FF, D), jnp.bfloat16)],
    )
    return pl.pallas_call(
        _expert_kernel,
        grid_spec=grid_spec,
        out_shape=jax.ShapeDtypeStruct((n_rows, D), jnp.float32),
        compiler_params=_cparams(("arbitrary",)),
        name="experts",
    )(tile_e, n_used, xs, wu, bu, wd, bd)


def _combine_kernel(dest_hbm, y_hbm, x1_ref, gate_ref, g_ref, b_ref, o_ref, buf, idx_smem, idx_sem, row_sems):
    i = pl.program_id(0)
    n = pl.num_programs(0)
    tm = x1_ref.shape[0]

    def issue(step, slot):
        _fetch_rows_index(dest_hbm, step, idx_smem, idx_sem)

        def body(t, carry):
            for k in range(TOP_K):
                pltpu.make_async_copy(y_hbm.at[pl.ds(idx_smem[k * tm + t], 1)],
                                      buf.at[slot, k, pl.ds(t, 1)], row_sems.at[slot]).start(priority=k % 2)
            return carry

        lax.fori_loop(0, tm, body, 0, unroll=8)

    @pl.when(i == 0)
    def _():
        issue(0, 0)

    @pl.when(i + 1 < n)
    def _():
        issue(i + 1, (i + 1) % 2)

    slot = i % 2
    for k in range(TOP_K):
        pltpu.make_async_copy(y_hbm.at[pl.ds(0, tm)], buf.at[slot, k], row_sems.at[slot]).wait()
    ffn = buf[slot, 0] * gate_ref[:, 0:1]
    for k in range(1, TOP_K):
        ffn = ffn + buf[slot, k] * gate_ref[:, k:k + 1]
    o_ref[...] = _layer_norm(DEEPNORM_ALPHA * x1_ref[...] + ffn, g_ref[...], b_ref[...])


def _combine(dest_steps, y, x1, gates_t, g, b):
    T, D = x1.shape
    n_steps, per_step = dest_steps.shape
    tm = per_step // TOP_K
    full = lambda a: pl.BlockSpec(a.shape, lambda i: (0, 0))
    return pl.pallas_call(
        _combine_kernel,
        grid=(n_steps,),
        in_specs=[
            pl.BlockSpec(memory_space=pl.ANY),
            pl.BlockSpec(memory_space=pl.ANY),
            pl.BlockSpec((tm, D), lambda i: (i, 0)),
            pl.BlockSpec((tm, 8), lambda i: (i, 0)),
            full(g), full(b),
        ],
        out_specs=pl.BlockSpec((tm, D), lambda i: (i, 0)),
        out_shape=jax.ShapeDtypeStruct((T, D), jnp.float32),
        scratch_shapes=[pltpu.VMEM((2, TOP_K, tm, D), jnp.float32), pltpu.SMEM((per_step,), jnp.int32),
                        pltpu.SemaphoreType.DMA(()), pltpu.SemaphoreType.DMA((2,))],
        compiler_params=_cparams(("arbitrary",)),
        name="combine_ln2",
    )(dest_steps, y, x1, gates_t, g, b)


def kernel(x, w_in, w_out, lambda_q1, lambda_k1, lambda_q2, lambda_k2, diff_norm_g, ln1_g, ln1_b,
           w_router, b_router, w_up, b_up, w_down, b_down, ln2_g, ln2_b):
    B, S, D = x.shape
    T = B * S
    assert w_in.shape[0] == 1, "single layer"
    assert S % DIFF_TQ == 0 and DIFF_TK == DIFF_TQ and DIFF_TK % ROW_TILE == 0 and S % (16 * DIL_TQ) == 0

    na = 3 * DIL_WIDTH
    colscale = np.ones((3 * D,), np.float32)
    colscale[:DIL_WIDTH] = HEAD_DIM ** -0.5
    colscale[na:na + DIFF_WIDTH] = HEAD_DIM ** -0.5 * LOG2E
    w_in_bf = (w_in[0] * colscale).astype(jnp.bfloat16)
    nv = na + 2 * DIFF_WIDTH

    kpos, qpos, dabs = _position_columns()
    pa, pa4, pa16, qd, ka, vt, qn, kn = _in_proj(x, w_in_bf[:, :nv], w_in_bf[:, nv:].T, kpos)
    dil = [_dilated_stage(p, d) for p, (_, d) in zip((pa[:, None], pa4, pa16), DIL_PATTERNS)]

    r2 = lambda a: a.reshape(1, -1).astype(jnp.float32)
    od = _diff_attention(_tile_reach(qn, kn, S // DIFF_TK), qd, ka, vt, qpos, dabs, r2(lambda_q1[0]), r2(lambda_k1[0]), r2(lambda_q2[0]),
                         r2(lambda_k2[0]), diff_norm_g[0].reshape(-1, 1).astype(jnp.float32))

    tri = jnp.asarray(np.triu(np.ones((ROW_TILE, ROW_TILE), np.float32), k=1), jnp.bfloat16)
    x1, x1p, top_idx, gates, rank, counts = _post_attn(
        dil, od.reshape(T, DIFF_WIDTH), x.reshape(T, D), w_out[0].astype(jnp.bfloat16),
        r2(ln1_g[0]), r2(ln1_b[0]), w_router[0].T, b_router[0].reshape(N_EXPERTS, 1), tri)

    counts = counts[:, 0]
    pcounts = ((counts + MOE_TILE - 1) // MOE_TILE) * MOE_TILE
    pends = jnp.cumsum(pcounts)
    pstarts = pends - pcounts
    n_rows = T * TOP_K + N_EXPERTS * MOE_TILE
    n_tiles = n_rows // MOE_TILE
    dest = rank[:TOP_K]
    for e in range(N_EXPERTS):
        dest = dest + jnp.where(top_idx[:TOP_K] == e, pstarts[e], 0)
    tile_starts = jnp.arange(n_tiles, dtype=jnp.int32) * MOE_TILE
    tile_e = jnp.minimum(jnp.sum(pends[None, :] <= tile_starts[:, None], axis=1), N_EXPERTS - 1).astype(jnp.int32)
    n_used = (pends[-1] // MOE_TILE).astype(jnp.int32).reshape(1)

    def per_step(tm):
        return dest.reshape(TOP_K, T // tm, tm).transpose(1, 0, 2).reshape(T // tm, TOP_K * tm)

    xs = _dispatch(per_step(ROW_TILE), x1p, n_rows)
    y = _experts(tile_e, n_used, xs, w_up[0], b_up[0][:, None, :], w_down[0], b_down[0][:, None, :])
    out = _combine(per_step(COMBINE_TILE), y, x1, gates.T, r2(ln2_g[0]), r2(ln2_b[0]))
    return out.reshape(B, S, D)
```

```python
import functools
import math

import jax
import jax.numpy as jnp
import numpy as np
from jax import lax
from jax.experimental import pallas as pl
from jax.experimental.pallas import tpu as pltpu

D_MODEL = 1024
HEAD_DIM = 64
DIL_WIDTH = 512
N_HEADS_DIL = 8
DIL_PATTERNS = ((128, 1), (512, 4), (2048, 16))
DIL_SIDE = 64
DIFF_WIDTH = 512
N_HEADS_DIFF = 4
DIFF_VDIM = 2 * HEAD_DIM
N_EXPERTS = 32
TOP_K = 4
D_FF = D_MODEL
SWIGLU_ALPHA = 1.702
SWIGLU_LIMIT = 7.0
DEEPNORM_ALPHA = 2.0 ** 0.25
LN_EPS = 1e-5
NEG_INF = -1e30
LAM_INIT = 0.8 - 0.6 * math.exp(-0.3 * 0)
LOG2E = math.log2(math.e)

LANES = 128
VMEM_LIMIT = 56 * 1024 * 1024

ROW_TILE = 512
DIFF_TQ = 1024
DIFF_TK = 1024
V_ROWS = DIFF_VDIM + 16
EXP2_HEADROOM = 4.0
UNDERFLOW_LOG2 = 160.0
NORM_SLACK = 1.01
DIL_TQ = 128
MOE_TILE = 512
COMBINE_TILE = 256
ROW_SUBLANES = 8
ZERO_FILL_COPIES = 8
POS_SPLIT = 16


def _cparams(sem):
    return pltpu.CompilerParams(dimension_semantics=sem, vmem_limit_bytes=VMEM_LIMIT)


def _dot_nt(a, b, **kw):
    return lax.dot_general(a, b, (((1,), (1,)), ((), ())), preferred_element_type=jnp.float32, **kw)


def _dot(a, b, **kw):
    return jnp.dot(a, b, preferred_element_type=jnp.float32, **kw)


def _layer_norm(y, g, b):
    mu = jnp.mean(y, axis=-1, keepdims=True)
    yc = y - mu
    var = jnp.mean(yc * yc, axis=-1, keepdims=True)
    return yc * lax.rsqrt(var + LN_EPS) * g + b


def _in_proj_kernel(x_ref, w_ref, wvt_ref, kpos_ref, grp_ref, pa_ref, pa4_ref, pa16_ref, qd_ref, ka_ref,
                    vt_ref, qn_ref, kn_ref, pa_scr):
    xb = x_ref[0].astype(jnp.bfloat16)
    na = 3 * DIL_WIDTH
    tm = xb.shape[0]
    paf = _dot(xb, w_ref[:, :na])
    pa_ref[0] = paf.astype(jnp.bfloat16)
    for c in range(na // LANES):
        cols = slice(c * LANES, (c + 1) * LANES)
        pa_scr[c] = paf[:, cols]
        for d, ref in ((DIL_PATTERNS[1][1], pa4_ref), (DIL_PATTERNS[2][1], pa16_ref)):
            for r in range(d):
                ref[0, r, :, cols] = pa_scr[c, pl.ds(r, tm // d, stride=d), :].astype(jnp.bfloat16)
    q = _dot(xb, w_ref[:, na:na + DIFF_WIDTH]).astype(jnp.bfloat16)
    k = _dot(xb, w_ref[:, na + DIFF_WIDTH:]).astype(jnp.bfloat16)
    vt = _dot_nt(wvt_ref[...], xb).astype(jnp.bfloat16)
    ones = jnp.ones((V_ROWS - DIFF_VDIM, tm), jnp.bfloat16)

    @pl.when(pl.program_id(1) == 0)
    def _():
        qn_ref[...] = jnp.zeros(qn_ref.shape, jnp.float32)
        kn_ref[...] = jnp.zeros(kn_ref.shape, jnp.float32)

    for src, dst in ((q, qn_ref), (k, kn_ref)):
        f = src.astype(jnp.float32)
        sq = f * f
        sq_hi = sq.astype(jnp.bfloat16)
        sq_lo = (sq - sq_hi.astype(jnp.float32)).astype(jnp.bfloat16)
        gs = _dot(sq_hi, grp_ref[...]) + _dot(sq_lo, grp_ref[...])
        dst[0] = jnp.maximum(dst[0], jnp.max(gs, axis=0, keepdims=True))
    for h in range(N_HEADS_DIFF):
        sl = slice(h * LANES, (h + 1) * LANES)
        qd_ref[0, h] = q[:, sl]
        ka_ref[0, h, :, :LANES] = k[:, sl]
        ka_ref[0, h, :, LANES:] = kpos_ref[h]
        vt_ref[0, h, :DIFF_VDIM, :] = vt[sl, :]
        vt_ref[0, h, DIFF_VDIM:, :] = ones


def _in_proj(x, w_bf, wvt_bf, kpos):
    B, S, D = x.shape
    tm = ROW_TILE
    na = 3 * DIL_WIDTH
    grp = np.zeros((DIFF_WIDTH, LANES), np.float32)
    grp[np.arange(DIFF_WIDTH), np.arange(DIFF_WIDTH) // HEAD_DIM] = 1.0
    grp = jnp.asarray(grp, jnp.bfloat16)
    norm_spec = pl.BlockSpec((1, 1, LANES), lambda b, i: (b, 0, 0))
    d4, d16 = DIL_PATTERNS[1][1], DIL_PATTERNS[2][1]
    return pl.pallas_call(
        _in_proj_kernel,
        grid=(B, S // tm),
        in_specs=[
            pl.BlockSpec((1, tm, D), lambda b, i: (b, i, 0)),
            pl.BlockSpec(w_bf.shape, lambda b, i: (0, 0)),
            pl.BlockSpec(wvt_bf.shape, lambda b, i: (0, 0)),
            pl.BlockSpec((N_HEADS_DIFF, tm, LANES), lambda b, i: (0, i % (DIFF_TK // tm), 0)),
            pl.BlockSpec(grp.shape, lambda b, i: (0, 0)),
        ],
        out_specs=[
            pl.BlockSpec((1, tm, na), lambda b, i: (b, i, 0)),
            pl.BlockSpec((1, d4, tm // d4, na), lambda b, i: (b, 0, i, 0)),
            pl.BlockSpec((1, d16, tm // d16, na), lambda b, i: (b, 0, i, 0)),
            pl.BlockSpec((1, N_HEADS_DIFF, tm, LANES), lambda b, i: (b, 0, i, 0)),
            pl.BlockSpec((1, N_HEADS_DIFF, tm, 2 * LANES), lambda b, i: (b, 0, i, 0)),
            pl.BlockSpec((1, N_HEADS_DIFF, V_ROWS, tm), lambda b, i: (b, 0, 0, i)),
            norm_spec, norm_spec,
        ],
        out_shape=[
            jax.ShapeDtypeStruct((B, S, na), jnp.bfloat16),
            jax.ShapeDtypeStruct((B, d4, S // d4, na), jnp.bfloat16),
            jax.ShapeDtypeStruct((B, d16, S // d16, na), jnp.bfloat16),
            jax.ShapeDtypeStruct((B, N_HEADS_DIFF, S, LANES), jnp.bfloat16),
            jax.ShapeDtypeStruct((B, N_HEADS_DIFF, S, 2 * LANES), jnp.bfloat16),
            jax.ShapeDtypeStruct((B, N_HEADS_DIFF, V_ROWS, S), jnp.bfloat16),
            jax.ShapeDtypeStruct((B, 1, LANES), jnp.float32),
            jax.ShapeDtypeStruct((B, 1, LANES), jnp.float32),
        ],
        scratch_shapes=[pltpu.VMEM((na // LANES, tm, LANES), jnp.float32)],
        compiler_params=_cparams(("parallel", "arbitrary")),
        name="in_proj",
    )(x, w_bf, wvt_bf, kpos, grp)


def _dilated_kernel(q_ref, k0_ref, k1_ref, k2_ref, k3_ref, v0_ref, v1_ref, v2_ref, v3_ref, bias_ref,
                    o_ref, lse_ref, *, seq_len):
    tq = q_ref.shape[2]
    nk = tq + 2 * DIL_SIDE
    i = pl.program_id(2)
    q = q_ref[0, 0]
    kwin = jnp.concatenate([r[0, 0] for r in (k0_ref, k1_ref, k2_ref, k3_ref)], axis=0)
    vwin = jnp.concatenate([r[0, 0] for r in (v0_ref, v1_ref, v2_ref, v3_ref)], axis=0)
    key_pos = i * tq - DIL_SIDE + lax.broadcasted_iota(jnp.int32, (1, nk), 1)
    edge = jnp.where((key_pos >= 0) & (key_pos < seq_len), 0.0, NEG_INF)
    low_half = lax.broadcasted_iota(jnp.int32, (tq, LANES), 1) < HEAD_DIM
    ss = []
    for h in range(N_HEADS_DIL):
        sl = slice((h // 2) * LANES, (h // 2 + 1) * LANES)
        qm = jnp.where(low_half == (h % 2 == 0), q[:, sl], jnp.zeros((tq, LANES), q.dtype))
        ss.append(_dot_nt(qm, kwin[:, sl]) + bias_ref[h])
    s = jnp.concatenate(ss, axis=0) + edge
    m = jnp.max(s, axis=-1, keepdims=True)
    p = jnp.exp(s - m)
    l = jnp.sum(p, axis=-1, keepdims=True)
    lse = m + jnp.log(l)
    inv_l = 1.0 / l
    pb = p.astype(jnp.bfloat16)
    for pair in range(N_HEADS_DIL // 2):
        sl = slice(pair * LANES, (pair + 1) * LANES)
        r0 = slice(2 * pair * tq, (2 * pair + 1) * tq)
        r1 = slice((2 * pair + 1) * tq, (2 * pair + 2) * tq)
        a0 = _dot(pb[r0], vwin[:, sl]) * inv_l[r0]
        a1 = _dot(pb[r1], vwin[:, sl]) * inv_l[r1]
        o_ref[0, 0, :, sl] = jnp.where(low_half, a0, a1).astype(o_ref.dtype)
        lse_ref[0, 0, :, sl] = jnp.where(low_half, lse[r0], lse[r1])


def _dilated_stage(pa_d, dilation):
    B, d, L, _ = pa_d.shape
    tq = DIL_TQ
    w = DIL_WIDTH
    halo = DIL_SIDE
    n_halo = L // halo
    bias = _dilated_bias(dilation)

    def halo_spec(col, j):
        def imap(b, r, i):
            return (b, r, jnp.clip(i * (tq // halo) - 1 + j, 0, n_halo - 1), col)
        return pl.BlockSpec((1, 1, halo, w), imap)

    out_spec = pl.BlockSpec((1, 1, tq, w), lambda b, r, i: (b, r, i, 0))
    in_specs = ([pl.BlockSpec((1, 1, tq, w), lambda b, r, i: (b, r, i, 0))]
                + [halo_spec(1, j) for j in range(4)] + [halo_spec(2, j) for j in range(4)]
                + [pl.BlockSpec(bias.shape, lambda b, r, i: (0, 0, 0))])
    return pl.pallas_call(
        functools.partial(_dilated_kernel, seq_len=L),
        grid=(B, d, L // tq),
        in_specs=in_specs,
        out_specs=[out_spec, out_spec],
        out_shape=[jax.ShapeDtypeStruct((B, d, L, w), jnp.bfloat16),
                   jax.ShapeDtypeStruct((B, d, L, w), jnp.float32)],
        compiler_params=_cparams(("parallel", "parallel", "parallel")),
        name=f"dilated_d{d}",
    )(*([pa_d] * 9), bias)


def _dilated_bias(dilation):
    tq = DIL_TQ
    slopes = np.asarray([2.0 ** (-8.0 * (i + 1) / N_HEADS_DIL) for i in range(N_HEADS_DIL)], np.float32)
    rel = (np.arange(tq + 2 * DIL_SIDE)[None, :] - DIL_SIDE) - np.arange(tq)[:, None]
    band = np.abs(rel) <= DIL_SIDE
    pen = -(slopes * dilation)[:, None, None] * np.abs(rel).astype(np.float32)[None]
    return jnp.asarray(np.where(band[None], pen, np.float32(NEG_INF)), jnp.float32)


def _diff_kernel(reach_ref, q_ref, ka_ref, vt_ref, qpos_ref, dabs_ref, lq1_ref, lk1_ref, lq2_ref, lk2_ref,
                 g_ref, o_ref, qa_scr, m_scr, acc_scr, tmp_scr, *, slopes):
    b = pl.program_id(0)
    h = pl.program_id(1)
    qi = pl.program_id(2)
    ki = pl.program_id(3)
    nk = pl.num_programs(3)
    tq = q_ref.shape[2]
    tk = ka_ref.shape[2]

    slope = jnp.float32(slopes[0])
    for hh in range(1, N_HEADS_DIFF):
        slope = jnp.where(h == hh, jnp.float32(slopes[hh]), slope)

    @pl.when(ki == 0)
    def _():
        q = q_ref[0, 0]
        low_half = lax.broadcasted_iota(jnp.int32, (tq, LANES), 1) < HEAD_DIM
        zero = jnp.zeros_like(q)
        qpos = qpos_ref[0]
        for c in range(2):
            qc = jnp.where(low_half == (c == 0), q, zero)
            for var, pos in enumerate((qpos, -qpos, jnp.zeros_like(qpos))):
                qa_scr[c, var, :, :LANES] = qc
                qa_scr[c, var, :, LANES:] = pos
        m_scr[...] = jnp.full(m_scr.shape, NEG_INF, jnp.float32)
        acc_scr[...] = jnp.zeros(acc_scr.shape, jnp.float32)

    kt = lax.rem(qi + ki, nk)
    var = jnp.where(kt < qi, 0, jnp.where(kt > qi, 1, 2))
    off = -slope * jnp.abs(qi * tq - kt * tk).astype(jnp.float32)
    ka = ka_ref[0, 0]
    vt = vt_ref[0, 0]


    def restabilise(diag):
        ss = []
        for c in range(2):
            s = _dot_nt(ka, qa_scr[c, var])
            if diag:
                s = s + dabs_ref[...] * slope
            ss.append(s)
        for c in range(2):
            s = ss[c]
            m_old = m_scr[c]
            m_new = jnp.maximum(m_old, jnp.max(s, axis=0, keepdims=True) + off)
            alpha = jnp.exp2(m_old - m_new)
            p = jnp.exp2((s - (m_new - off)).astype(jnp.bfloat16))
            acc_scr[c] = alpha * acc_scr[c] + _dot(vt, p)
            m_scr[c] = m_new

    @pl.when(ki == 0)
    def _():
        restabilise(True)

    @pl.when((ki != 0) & (jnp.abs(kt - qi) <= reach_ref[b * N_HEADS_DIFF + h]))
    def _():
        excess = None
        for c in range(2):
            s = _dot_nt(ka, qa_scr[c, var])
            m_eff = m_scr[c] - off
            over = jnp.max(jnp.max(s, axis=0, keepdims=True) - m_eff)
            excess = over if excess is None else jnp.maximum(excess, over)
            p = jnp.exp2((s - m_eff).astype(jnp.bfloat16))
            tmp_scr[c] = acc_scr[c] + _dot(vt, p)

        @pl.when(excess <= EXP2_HEADROOM)
        def _():
            acc_scr[...] = tmp_scr[...]

        @pl.when(excess > EXP2_HEADROOM)
        def _():
            restabilise(False)

    @pl.when(ki == nk - 1)
    def _():
        lam = (jnp.exp(jnp.sum(lq1_ref[...] * lk1_ref[...], axis=-1, keepdims=True))
               - jnp.exp(jnp.sum(lq2_ref[...] * lk2_ref[...], axis=-1, keepdims=True)) + LAM_INIT)
        a1, a2 = acc_scr[0], acc_scr[1]
        o = (a1[:DIFF_VDIM] / a1[DIFF_VDIM:DIFF_VDIM + 1]
             - lam * (a2[:DIFF_VDIM] / a2[DIFF_VDIM:DIFF_VDIM + 1]))
        o = o * lax.rsqrt(jnp.mean(o * o, axis=0, keepdims=True) + LN_EPS) * g_ref[...]
        o_ref[0] = (o * (1.0 - LAM_INIT)).T.astype(o_ref.dtype)


def _tile_reach(qn, kn, n_tiles):
    B = qn.shape[0]
    ng = 2 * N_HEADS_DIFF
    qmax = jnp.sqrt(qn[:, 0, :ng]) * NORM_SLACK
    kmax = jnp.sqrt(kn[:, 0, :ng]) * NORM_SLACK
    bound = jnp.max((2.0 * qmax * kmax).reshape(B, N_HEADS_DIFF, 2), axis=-1) + UNDERFLOW_LOG2
    far = bound / jnp.asarray(_diff_slopes_log2())[None, :]
    reach = jnp.floor((far - 1.0) / DIFF_TK) + 1.0
    reach = jnp.where(jnp.isfinite(reach), reach, n_tiles)
    return jnp.clip(reach, 0, n_tiles).astype(jnp.int32).reshape(-1)


def _diff_attention(reach, qd, ka, vt, qpos, dabs, lq1, lk1, lq2, lk2, g_col):
    B, H, S, _ = qd.shape
    tq, tk = DIFF_TQ, DIFF_TK
    slopes = tuple(float(s) for s in _diff_slopes_log2())
    nk = S // tk

    def key_tile(b, h, qi, ki, reach):
        kt = (qi + ki) % nk
        return jnp.where(jnp.abs(kt - qi) <= reach[b * H + h], kt, qi)

    small = lambda a: pl.BlockSpec(a.shape, lambda b, h, qi, ki, reach: (0, 0))
    grid_spec = pltpu.PrefetchScalarGridSpec(
        num_scalar_prefetch=1,
        grid=(B, H, S // tq, nk),
        in_specs=[
            pl.BlockSpec((1, 1, tq, LANES), lambda b, h, qi, ki, reach: (b, h, qi, 0)),
            pl.BlockSpec((1, 1, tk, 2 * LANES),
                         lambda b, h, qi, ki, reach: (b, h, key_tile(b, h, qi, ki, reach), 0)),
            pl.BlockSpec((1, 1, V_ROWS, tk),
                         lambda b, h, qi, ki, reach: (b, h, 0, key_tile(b, h, qi, ki, reach))),
            pl.BlockSpec((1, tq, LANES), lambda b, h, qi, ki, reach: (h, 0, 0)),
            small(dabs), small(lq1), small(lk1), small(lq2), small(lk2), small(g_col),
        ],
        out_specs=pl.BlockSpec((1, tq, LANES), lambda b, h, qi, ki, reach: (b, qi, h)),
        scratch_shapes=[
            pltpu.VMEM((2, 3, tq, 2 * LANES), jnp.bfloat16),
            pltpu.VMEM((2, 1, tq), jnp.float32),
            pltpu.VMEM((2, V_ROWS, tq), jnp.float32),
            pltpu.VMEM((2, V_ROWS, tq), jnp.float32),
        ],
    )
    return pl.pallas_call(
        functools.partial(_diff_kernel, slopes=slopes),
        grid_spec=grid_spec,
        out_shape=jax.ShapeDtypeStruct((B, S, H * LANES), jnp.bfloat16),
        compiler_params=_cparams(("parallel", "parallel", "parallel", "arbitrary")),
        name="diff_attention",
    )(reach, qd, ka, vt, qpos, dabs, lq1, lk1, lq2, lk2, g_col)


def _diff_slopes_log2():
    return np.asarray([2.0 ** (-8.0 * (i + 1) / N_HEADS_DIFF) for i in range(N_HEADS_DIFF)],
                      np.float32) * np.float32(LOG2E)


def _position_columns():
    def bf16_round(v):
        u = np.asarray(v, np.float32).view(np.uint32)
        return ((u + 0x7FFF + ((u >> 16) & 1)) & np.uint32(0xFFFF0000)).view(np.float32)

    kpos = np.zeros((N_HEADS_DIFF, DIFF_TK, LANES), np.float32)
    qpos = np.zeros((N_HEADS_DIFF, DIFF_TQ, LANES), np.float32)
    pk, pq = np.arange(DIFF_TK), np.arange(DIFF_TQ)
    for h, a in enumerate(_diff_slopes_log2()):
        rest = np.float32(a)
        for n in range(3):
            a_n = np.float32(bf16_round(rest))
            rest = np.float32(rest - a_n)
            c = 4 * n
            kpos[h, :, c + 0] = (pk // POS_SPLIT) * POS_SPLIT
            kpos[h, :, c + 1] = pk % POS_SPLIT
            kpos[h, :, c + 2] = a_n
            kpos[h, :, c + 3] = a_n
            qpos[h, :, c + 0] = a_n
            qpos[h, :, c + 1] = a_n
            qpos[h, :, c + 2] = -((pq // POS_SPLIT) * POS_SPLIT)
            qpos[h, :, c + 3] = -(pq % POS_SPLIT)
    dabs = -np.abs(pq[:, None] - pk[None, :]).astype(np.float32)
    return jnp.asarray(kpos, jnp.bfloat16), jnp.asarray(qpos, jnp.bfloat16), jnp.asarray(dabs)


def _post_attn_kernel(o1_ref, l1_ref, o4_ref, l4_ref, o16_ref, l16_ref, od_ref, x_ref, wo_ref, g_ref, b_ref,
                      wr_ref, br_ref, tri_ref,
                      x1_ref, x1t_ref, xs0_hbm, idx_ref, gate_ref, rank_ref, cnt_ref, carry_scr, order_scr,
                      zero_scr, zero_sem, *, n_fill):
    step = pl.program_id(0)

    @pl.when(step == 0)
    def _():
        carry_scr[...] = jnp.zeros(carry_scr.shape, jnp.float32)
        zero_scr[...] = jnp.zeros(zero_scr.shape, zero_scr.dtype)

    zrows = zero_scr.shape[0]

    def fill_copy(j):
        start = pl.multiple_of((step * n_fill + j) * zrows, zrows)
        return pltpu.make_async_copy(zero_scr, xs0_hbm.at[pl.ds(start, zrows)], zero_sem)

    for j in range(n_fill):
        fill_copy(j).start()

    tm = x_ref.shape[0]
    nc = DIL_WIDTH // LANES
    for n, (o_ref, l_ref) in enumerate(((o4_ref, l4_ref), (o16_ref, l16_ref))):
        d = o_ref.shape[1]
        for r in range(d):
            o_r = o_ref[0, r].astype(jnp.float32)
            l_r = l_ref[0, r]
            for c in range(nc):
                cols = slice(c * LANES, (c + 1) * LANES)
                order_scr[2 * n, c, pl.ds(r, tm // d, stride=d), :] = o_r[:, cols]
                order_scr[2 * n + 1, c, pl.ds(r, tm // d, stride=d), :] = l_r[:, cols]
    in_order = lambda n: jnp.concatenate([order_scr[n, c] for c in range(nc)], axis=1)
    outs = (o1_ref[0, 0].astype(jnp.float32), in_order(0), in_order(2))
    lses = (l1_ref[0, 0], in_order(1), in_order(3))
    top = jnp.maximum(jnp.maximum(lses[0], lses[1]), lses[2])
    wts = [jnp.exp(l - top) for l in lses]
    oa = ((wts[0] * outs[0] + wts[1] * outs[1] + wts[2] * outs[2])
          / (wts[0] + wts[1] + wts[2])).astype(jnp.bfloat16)

    mix = _dot(oa, wo_ref[:DIL_WIDTH, :]) + _dot(od_ref[...], wo_ref[DIL_WIDTH:, :])
    x1 = _layer_norm(DEEPNORM_ALPHA * x_ref[...] + mix, g_ref[...], b_ref[...])
    x1_ref[...] = x1
    for s in range(ROW_SUBLANES):
        x1t_ref[pl.ds(s, x1.shape[0], stride=ROW_SUBLANES), :] = x1[:, s * LANES:(s + 1) * LANES]

    tm = x1.shape[0]
    lg = _dot_nt(wr_ref[...], x1, precision=lax.Precision.HIGHEST) + br_ref[...]
    eidx = lax.broadcasted_iota(jnp.int32, (N_EXPERTS, tm), 0)
    vals, sels = [], []
    for k in range(TOP_K):
        mx = jnp.max(lg, axis=0, keepdims=True)
        idx = jnp.min(jnp.where(lg == mx, eidx, N_EXPERTS), axis=0, keepdims=True)
        sel = eidx == idx
        vals.append(mx)
        sels.append(sel)
        idx_ref[k:k + 1, :] = idx
        lg = jnp.where(sel, -jnp.inf, lg)
    ex = [jnp.exp(v - vals[0]) for v in vals]
    den = ex[0] + ex[1] + ex[2] + ex[3]
    for k in range(TOP_K):
        gate_ref[k:k + 1, :] = ex[k] / den

    chosen = (sels[0] | sels[1] | sels[2] | sels[3])
    onehot = jnp.where(chosen, 1.0, 0.0)
    before = _dot(onehot.astype(jnp.bfloat16), tri_ref[...]) + carry_scr[...]
    for k in range(TOP_K):
        rank_ref[k:k + 1, :] = jnp.sum(jnp.where(sels[k], before, 0.0), axis=0,
                                       keepdims=True).astype(jnp.int32)
    carry_scr[...] = carry_scr[...] + jnp.sum(onehot, axis=1, keepdims=True)
    pad = jnp.zeros((8 - TOP_K, tm), jnp.int32)
    idx_ref[TOP_K:, :] = pad
    rank_ref[TOP_K:, :] = pad
    gate_ref[TOP_K:, :] = pad.astype(jnp.float32)
    cnt_ref[...] = jnp.broadcast_to(carry_scr[...], cnt_ref.shape).astype(jnp.int32)
    for j in range(n_fill):
        fill_copy(j).wait()


def _post_attn(dil, od, x, wo_bf, g, b, wr_t, br, tri, n_rows):
    T, D = x.shape
    assert D == ROW_SUBLANES * LANES
    tm = ROW_TILE
    nb = dil[0][0].shape[2] // tm
    row = lambda w: pl.BlockSpec((tm, w), lambda i: (i, 0))
    full = lambda a: pl.BlockSpec(a.shape, lambda i: (0, 0))
    col = pl.BlockSpec((8, tm), lambda i: (0, i))
    dil_specs, dil_args = [], []
    for o_d, lse_d in dil:
        d = o_d.shape[1]
        spec = pl.BlockSpec((1, d, tm // d, DIL_WIDTH), lambda i: (i // nb, 0, i % nb, 0))
        dil_specs += [spec, spec]
        dil_args += [o_d, lse_d]
    steps = T // tm
    fill_rows, rem = divmod(n_rows * ROW_SUBLANES, steps * ZERO_FILL_COPIES)
    assert rem == 0 and fill_rows % ROW_SUBLANES == 0
    return pl.pallas_call(
        functools.partial(_post_attn_kernel, n_fill=ZERO_FILL_COPIES),
        grid=(steps,),
        in_specs=dil_specs + [row(DIFF_WIDTH), row(D), full(wo_bf), full(g), full(b), full(wr_t),
                              full(br), full(tri)],
        out_specs=[row(D), pl.BlockSpec((tm * ROW_SUBLANES, LANES), lambda i: (i, 0)),
                   pl.BlockSpec(memory_space=pl.ANY), col, col, col,
                   pl.BlockSpec((N_EXPERTS, LANES), lambda i: (0, 0))],
        out_shape=[
            jax.ShapeDtypeStruct((T, D), jnp.float32),
            jax.ShapeDtypeStruct((T * ROW_SUBLANES, LANES), jnp.float32),
            jax.ShapeDtypeStruct((n_rows * ROW_SUBLANES, LANES), jnp.float32),
            jax.ShapeDtypeStruct((8, T), jnp.int32),
            jax.ShapeDtypeStruct((8, T), jnp.float32),
            jax.ShapeDtypeStruct((8, T), jnp.int32),
            jax.ShapeDtypeStruct((N_EXPERTS, LANES), jnp.int32),
        ],
        scratch_shapes=[pltpu.VMEM((N_EXPERTS, 1), jnp.float32),
                        pltpu.VMEM((4, DIL_WIDTH // LANES, tm, LANES), jnp.float32),
                        pltpu.VMEM((fill_rows, LANES), jnp.float32), pltpu.SemaphoreType.DMA(())],
        compiler_params=_cparams(("arbitrary",)),
        name="post_attn_router",
    )(*dil_args, od, x, wo_bf, g, b, wr_t, br, tri)


def _fetch_rows_index(dest_hbm, step, idx_smem, sem):
    cp = pltpu.make_async_copy(dest_hbm.at[step], idx_smem, sem)
    cp.start()
    cp.wait()


def _row_tile(ref, row):
    return ref.at[pl.ds(pl.multiple_of(row * ROW_SUBLANES, ROW_SUBLANES), ROW_SUBLANES)]


def _rows_from_tiles(ref, n):
    return jnp.concatenate([ref[pl.ds(s, n, stride=ROW_SUBLANES), :] for s in range(ROW_SUBLANES)], axis=1)


def _rows_to_tiles(ref, rows):
    for s in range(ROW_SUBLANES):
        ref[pl.ds(s, rows.shape[0], stride=ROW_SUBLANES), :] = rows[:, s * LANES:(s + 1) * LANES]


def _dispatch_kernel(dest_hbm, x_ref, xs_in_hbm, xs_hbm, idx_smem, idx_sem, row_sem):
    del xs_in_hbm
    tm = x_ref.shape[0] // ROW_SUBLANES
    _fetch_rows_index(dest_hbm, pl.program_id(0), idx_smem, idx_sem)

    def issue(t, carry):
        for k in range(TOP_K):
            pltpu.make_async_copy(_row_tile(x_ref, t), _row_tile(xs_hbm, idx_smem[k * tm + t]),
                                  row_sem).start(priority=k % 2)
        return carry

    lax.fori_loop(0, tm, issue, 0, unroll=8)
    for k in range(TOP_K):
        pltpu.make_async_copy(x_ref, xs_hbm.at[pl.ds(0, tm * ROW_SUBLANES)], row_sem).wait()


def _dispatch(dest_steps, x1t, xs0):
    n_steps, per_step = dest_steps.shape
    tm = per_step // TOP_K
    return pl.pallas_call(
        _dispatch_kernel,
        grid=(n_steps,),
        in_specs=[
            pl.BlockSpec(memory_space=pl.ANY),
            pl.BlockSpec((tm * ROW_SUBLANES, LANES), lambda i: (i, 0)),
            pl.BlockSpec(memory_space=pl.ANY),
        ],
        out_specs=pl.BlockSpec(memory_space=pl.ANY),
        out_shape=jax.ShapeDtypeStruct(xs0.shape, xs0.dtype),
        scratch_shapes=[pltpu.SMEM((per_step,), jnp.int32), pltpu.SemaphoreType.DMA(()),
                        pltpu.SemaphoreType.DMA(())],
        input_output_aliases={2: 0},
        compiler_params=_cparams(("arbitrary",)),
        name="dispatch_rows",
    )(dest_steps, x1t, xs0)


def _expert_kernel(tile_e_ref, n_used_ref, xs_ref, wu_ref, bu_ref, wd_ref, bd_ref, y_ref, wu_scr, wd_scr):
    i = pl.program_id(0)
    used = i < n_used_ref[0]
    new_expert = (i == 0) | (tile_e_ref[i] != tile_e_ref[jnp.maximum(i - 1, 0)])
    tm = xs_ref.shape[0] // ROW_SUBLANES

    @pl.when(used & new_expert)
    def _():
        wu_scr[...] = wu_ref[0].astype(jnp.bfloat16)
        wd_scr[...] = wd_ref[0].astype(jnp.bfloat16)

    @pl.when(used)
    def _():
        xs = _rows_from_tiles(xs_ref, tm).astype(jnp.bfloat16)
        hu = _dot(xs, wu_scr[...]) + bu_ref[0]
        g = jnp.minimum(hu[:, :D_FF], SWIGLU_LIMIT)
        u = jnp.clip(hu[:, D_FF:], -SWIGLU_LIMIT, SWIGLU_LIMIT)
        act = g * (1.0 / (1.0 + jnp.exp(-SWIGLU_ALPHA * g))) * (u + 1.0)
        _rows_to_tiles(y_ref, _dot(act.astype(jnp.bfloat16), wd_scr[...]) + bd_ref[0])

    @pl.when(jnp.logical_not(used))
    def _():
        y_ref[...] = jnp.zeros(y_ref.shape, y_ref.dtype)


def _experts(tile_e, n_used, xs, wu, bu, wd, bd):
    D = ROW_SUBLANES * LANES
    n_rows = xs.shape[0] // ROW_SUBLANES
    tm = MOE_TILE
    tile_spec = pl.BlockSpec((tm * ROW_SUBLANES, LANES), lambda i, te, nu: (i, 0))
    grid_spec = pltpu.PrefetchScalarGridSpec(
        num_scalar_prefetch=2,
        grid=(n_rows // tm,),
        in_specs=[
            tile_spec,
            pl.BlockSpec((1, D, 2 * D_FF), lambda i, te, nu: (te[i], 0, 0)),
            pl.BlockSpec((1, 1, 2 * D_FF), lambda i, te, nu: (te[i], 0, 0)),
            pl.BlockSpec((1, D_FF, D), lambda i, te, nu: (te[i], 0, 0)),
            pl.BlockSpec((1, 1, D), lambda i, te, nu: (te[i], 0, 0)),
        ],
        out_specs=tile_spec,
        scratch_shapes=[pltpu.VMEM((D, 2 * D_FF), jnp.bfloat16), pltpu.VMEM((D_FF, D), jnp.bfloat16)],
    )
    return pl.pallas_call(
        _expert_kernel,
        grid_spec=grid_spec,
        out_shape=jax.ShapeDtypeStruct(xs.shape, jnp.float32),
        compiler_params=_cparams(("arbitrary",)),
        name="experts",
    )(tile_e, n_used, xs, wu, bu, wd, bd)


def _combine_kernel(dest_hbm, y_hbm, x1_ref, gate_ref, g_ref, b_ref, o_ref, buf, idx_smem, idx_sem, row_sems):
    i = pl.program_id(0)
    n = pl.num_programs(0)
    tm = x1_ref.shape[0]

    def issue(step, slot):
        _fetch_rows_index(dest_hbm, step, idx_smem, idx_sem)

        def body(t, carry):
            for k in range(TOP_K):
                pltpu.make_async_copy(_row_tile(y_hbm, idx_smem[k * tm + t]), _row_tile(buf.at[slot, k], t),
                                      row_sems.at[slot]).start(priority=k % 2)
            return carry

        lax.fori_loop(0, tm, body, 0, unroll=8)

    @pl.when(i == 0)
    def _():
        issue(0, 0)

    @pl.when(i + 1 < n)
    def _():
        issue(i + 1, (i + 1) % 2)

    slot = i % 2
    for k in range(TOP_K):
        pltpu.make_async_copy(y_hbm.at[pl.ds(0, tm * ROW_SUBLANES)], buf.at[slot, k], row_sems.at[slot]).wait()
    ffn = _rows_from_tiles(buf.at[slot, 0], tm) * gate_ref[:, 0:1]
    for k in range(1, TOP_K):
        ffn = ffn + _rows_from_tiles(buf.at[slot, k], tm) * gate_ref[:, k:k + 1]
    o_ref[...] = _layer_norm(DEEPNORM_ALPHA * x1_ref[...] + ffn, g_ref[...], b_ref[...])


def _combine(dest_steps, y, x1, gates_t, g, b):
    T, D = x1.shape
    n_steps, per_step = dest_steps.shape
    tm = per_step // TOP_K
    full = lambda a: pl.BlockSpec(a.shape, lambda i: (0, 0))
    return pl.pallas_call(
        _combine_kernel,
        grid=(n_steps,),
        in_specs=[
            pl.BlockSpec(memory_space=pl.ANY),
            pl.BlockSpec(memory_space=pl.ANY),
            pl.BlockSpec((tm, D), lambda i: (i, 0)),
            pl.BlockSpec((tm, 8), lambda i: (i, 0)),
            full(g), full(b),
        ],
        out_specs=pl.BlockSpec((tm, D), lambda i: (i, 0)),
        out_shape=jax.ShapeDtypeStruct((T, D), jnp.float32),
        scratch_shapes=[pltpu.VMEM((2, TOP_K, tm * ROW_SUBLANES, LANES), jnp.float32),
                        pltpu.SMEM((per_step,), jnp.int32),
                        pltpu.SemaphoreType.DMA(()), pltpu.SemaphoreType.DMA((2,))],
        compiler_params=_cparams(("arbitrary",)),
        name="combine_ln2",
    )(dest_steps, y, x1, gates_t, g, b)


def kernel(x, w_in, w_out, lambda_q1, lambda_k1, lambda_q2, lambda_k2, diff_norm_g, ln1_g, ln1_b,
           w_router, b_router, w_up, b_up, w_down, b_down, ln2_g, ln2_b):
    B, S, D = x.shape
    T = B * S
    assert w_in.shape[0] == 1, "single layer"
    assert S % DIFF_TQ == 0 and DIFF_TK == DIFF_TQ and DIFF_TK % ROW_TILE == 0 and S % (16 * DIL_TQ) == 0

    na = 3 * DIL_WIDTH
    colscale = np.ones((3 * D,), np.float32)
    colscale[:DIL_WIDTH] = HEAD_DIM ** -0.5
    colscale[na:na + DIFF_WIDTH] = HEAD_DIM ** -0.5 * LOG2E
    w_in_bf = (w_in[0] * colscale).astype(jnp.bfloat16)
    nv = na + 2 * DIFF_WIDTH

    kpos, qpos, dabs = _position_columns()
    pa, pa4, pa16, qd, ka, vt, qn, kn = _in_proj(x, w_in_bf[:, :nv], w_in_bf[:, nv:].T, kpos)
    dil = [_dilated_stage(p, d) for p, (_, d) in zip((pa[:, None], pa4, pa16), DIL_PATTERNS)]

    r2 = lambda a: a.reshape(1, -1).astype(jnp.float32)
    od = _diff_attention(_tile_reach(qn, kn, S // DIFF_TK), qd, ka, vt, qpos, dabs, r2(lambda_q1[0]), r2(lambda_k1[0]), r2(lambda_q2[0]),
                         r2(lambda_k2[0]), diff_norm_g[0].reshape(-1, 1).astype(jnp.float32))

    tri = jnp.asarray(np.triu(np.ones((ROW_TILE, ROW_TILE), np.float32), k=1), jnp.bfloat16)
    n_rows = T * TOP_K + N_EXPERTS * MOE_TILE
    n_tiles = n_rows // MOE_TILE
    x1, x1t, xs0, top_idx, gates, rank, counts = _post_attn(
        dil, od.reshape(T, DIFF_WIDTH), x.reshape(T, D), w_out[0].astype(jnp.bfloat16),
        r2(ln1_g[0]), r2(ln1_b[0]), w_router[0].T, b_router[0].reshape(N_EXPERTS, 1), tri, n_rows)

    counts = counts[:, 0]
    pcounts = ((counts + MOE_TILE - 1) // MOE_TILE) * MOE_TILE
    pends = jnp.cumsum(pcounts)
    pstarts = pends - pcounts
    dest = rank[:TOP_K]
    for e in range(N_EXPERTS):
        dest = dest + jnp.where(top_idx[:TOP_K] == e, pstarts[e], 0)
    tile_starts = jnp.arange(n_tiles, dtype=jnp.int32) * MOE_TILE
    tile_e = jnp.minimum(jnp.sum(pends[None, :] <= tile_starts[:, None], axis=1), N_EXPERTS - 1).astype(jnp.int32)
    n_used = (pends[-1] // MOE_TILE).astype(jnp.int32).reshape(1)

    def per_step(tm):
        return dest.reshape(TOP_K, T // tm, tm).transpose(1, 0, 2).reshape(T // tm, TOP_K * tm)

    xs = _dispatch(per_step(ROW_TILE), x1t, xs0)
    y = _experts(tile_e, n_used, xs, w_up[0], b_up[0][:, None, :], w_down[0], b_down[0][:, None, :])
    out = _combine(per_step(COMBINE_TILE), y, x1, gates.T, r2(ln2_g[0]), r2(ln2_b[0]))
    return out.reshape(B, S, D)
```

```python
import functools
import math

import jax
import jax.numpy as jnp
import numpy as np
from jax import lax
from jax.experimental import pallas as pl
from jax.experimental.pallas import tpu as pltpu

D_MODEL = 1024
HEAD_DIM = 64
DIL_WIDTH = 512
N_HEADS_DIL = 8
DIL_PATTERNS = ((128, 1), (512, 4), (2048, 16))
DIL_SIDE = 64
DIFF_WIDTH = 512
N_HEADS_DIFF = 4
DIFF_VDIM = 2 * HEAD_DIM
N_EXPERTS = 32
TOP_K = 4
D_FF = D_MODEL
SWIGLU_ALPHA = 1.702
SWIGLU_LIMIT = 7.0
DEEPNORM_ALPHA = 2.0 ** 0.25
LN_EPS = 1e-5
NEG_INF = -1e30
LAM_INIT = 0.8 - 0.6 * math.exp(-0.3 * 0)
LOG2E = math.log2(math.e)

LANES = 128
VMEM_LIMIT = 56 * 1024 * 1024

ROW_TILE = 512
DIFF_TQ = 1024
DIFF_TK = 1024
V_ROWS = DIFF_VDIM + 16
EXP2_HEADROOM = 4.0
UNDERFLOW_LOG2 = 160.0
NORM_SLACK = 1.01
DIL_TQ = 128
MOE_TILE = 512
COMBINE_TILE = 256
ROW_SUBLANES = 8
ZERO_FILL_COPIES = 8
POS_SPLIT = 16


def _cparams(sem):
    return pltpu.CompilerParams(dimension_semantics=sem, vmem_limit_bytes=VMEM_LIMIT)


def _dot_nt(a, b, **kw):
    return lax.dot_general(a, b, (((1,), (1,)), ((), ())), preferred_element_type=jnp.float32, **kw)


def _dot(a, b, **kw):
    return jnp.dot(a, b, preferred_element_type=jnp.float32, **kw)


def _layer_norm(y, g, b):
    mu = jnp.mean(y, axis=-1, keepdims=True)
    yc = y - mu
    var = jnp.mean(yc * yc, axis=-1, keepdims=True)
    return yc * lax.rsqrt(var + LN_EPS) * g + b


def _in_proj_kernel(x_ref, w_ref, wvt_ref, kpos_ref, grp_ref, pa_ref, pa4_ref, pa16_ref, qd_ref, ka_ref,
                    vt_ref, qn_ref, kn_ref, pa_scr):
    xb = x_ref[0].astype(jnp.bfloat16)
    na = 3 * DIL_WIDTH
    tm = xb.shape[0]
    paf = _dot(xb, w_ref[:, :na])
    pa_ref[0] = paf.astype(jnp.bfloat16)
    for c in range(na // LANES):
        cols = slice(c * LANES, (c + 1) * LANES)
        pa_scr[c] = paf[:, cols]
        for d, ref in ((DIL_PATTERNS[1][1], pa4_ref), (DIL_PATTERNS[2][1], pa16_ref)):
            for r in range(d):
                ref[0, r, :, cols] = pa_scr[c, pl.ds(r, tm // d, stride=d), :].astype(jnp.bfloat16)
    q = _dot(xb, w_ref[:, na:na + DIFF_WIDTH]).astype(jnp.bfloat16)
    k = _dot(xb, w_ref[:, na + DIFF_WIDTH:]).astype(jnp.bfloat16)
    vt = _dot_nt(wvt_ref[...], xb).astype(jnp.bfloat16)
    ones = jnp.ones((V_ROWS - DIFF_VDIM, tm), jnp.bfloat16)

    @pl.when(pl.program_id(1) == 0)
    def _():
        qn_ref[...] = jnp.zeros(qn_ref.shape, jnp.float32)
        kn_ref[...] = jnp.zeros(kn_ref.shape, jnp.float32)

    for src, dst in ((q, qn_ref), (k, kn_ref)):
        f = src.astype(jnp.float32)
        sq = f * f
        sq_hi = sq.astype(jnp.bfloat16)
        sq_lo = (sq - sq_hi.astype(jnp.float32)).astype(jnp.bfloat16)
        gs = _dot(sq_hi, grp_ref[...]) + _dot(sq_lo, grp_ref[...])
        dst[0] = jnp.maximum(dst[0], jnp.max(gs, axis=0, keepdims=True))
    for h in range(N_HEADS_DIFF):
        sl = slice(h * LANES, (h + 1) * LANES)
        qd_ref[0, h] = q[:, sl]
        ka_ref[0, h, :, :LANES] = k[:, sl]
        ka_ref[0, h, :, LANES:] = kpos_ref[h]
        vt_ref[0, h, :DIFF_VDIM, :] = vt[sl, :]
        vt_ref[0, h, DIFF_VDIM:, :] = ones


def _in_proj(x, w_bf, wvt_bf, kpos):
    B, S, D = x.shape
    tm = ROW_TILE
    na = 3 * DIL_WIDTH
    grp = np.zeros((DIFF_WIDTH, LANES), np.float32)
    grp[np.arange(DIFF_WIDTH), np.arange(DIFF_WIDTH) // HEAD_DIM] = 1.0
    grp = jnp.asarray(grp, jnp.bfloat16)
    norm_spec = pl.BlockSpec((1, 1, LANES), lambda b, i: (b, 0, 0))
    d4, d16 = DIL_PATTERNS[1][1], DIL_PATTERNS[2][1]
    return pl.pallas_call(
        _in_proj_kernel,
        grid=(B, S // tm),
        in_specs=[
            pl.BlockSpec((1, tm, D), lambda b, i: (b, i, 0)),
            pl.BlockSpec(w_bf.shape, lambda b, i: (0, 0)),
            pl.BlockSpec(wvt_bf.shape, lambda b, i: (0, 0)),
            pl.BlockSpec((N_HEADS_DIFF, tm, LANES), lambda b, i: (0, i % (DIFF_TK // tm), 0)),
            pl.BlockSpec(grp.shape, lambda b, i: (0, 0)),
        ],
        out_specs=[
            pl.BlockSpec((1, tm, na), lambda b, i: (b, i, 0)),
            pl.BlockSpec((1, d4, tm // d4, na), lambda b, i: (b, 0, i, 0)),
            pl.BlockSpec((1, d16, tm // d16, na), lambda b, i: (b, 0, i, 0)),
            pl.BlockSpec((1, N_HEADS_DIFF, tm, LANES), lambda b, i: (b, 0, i, 0)),
            pl.BlockSpec((1, N_HEADS_DIFF, tm, 2 * LANES), lambda b, i: (b, 0, i, 0)),
            pl.BlockSpec((1, N_HEADS_DIFF, V_ROWS, tm), lambda b, i: (b, 0, 0, i)),
            norm_spec, norm_spec,
        ],
        out_shape=[
            jax.ShapeDtypeStruct((B, S, na), jnp.bfloat16),
            jax.ShapeDtypeStruct((B, d4, S // d4, na), jnp.bfloat16),
            jax.ShapeDtypeStruct((B, d16, S // d16, na), jnp.bfloat16),
            jax.ShapeDtypeStruct((B, N_HEADS_DIFF, S, LANES), jnp.bfloat16),
            jax.ShapeDtypeStruct((B, N_HEADS_DIFF, S, 2 * LANES), jnp.bfloat16),
            jax.ShapeDtypeStruct((B, N_HEADS_DIFF, V_ROWS, S), jnp.bfloat16),
            jax.ShapeDtypeStruct((B, 1, LANES), jnp.float32),
            jax.ShapeDtypeStruct((B, 1, LANES), jnp.float32),
        ],
        scratch_shapes=[pltpu.VMEM((na // LANES, tm, LANES), jnp.float32)],
        compiler_params=_cparams(("parallel", "arbitrary")),
        name="in_proj",
    )(x, w_bf, wvt_bf, kpos, grp)


def _dilated_kernel(q_ref, k0_ref, k1_ref, k2_ref, k3_ref, v0_ref, v1_ref, v2_ref, v3_ref, bias_ref,
                    o_ref, lse_ref, *, seq_len):
    tq = q_ref.shape[2]
    nk = tq + 2 * DIL_SIDE
    i = pl.program_id(2)
    q = q_ref[0, 0]
    kwin = jnp.concatenate([r[0, 0] for r in (k0_ref, k1_ref, k2_ref, k3_ref)], axis=0)
    vwin = jnp.concatenate([r[0, 0] for r in (v0_ref, v1_ref, v2_ref, v3_ref)], axis=0)
    key_pos = i * tq - DIL_SIDE + lax.broadcasted_iota(jnp.int32, (1, nk), 1)
    edge = jnp.where((key_pos >= 0) & (key_pos < seq_len), 0.0, NEG_INF)
    low_half = lax.broadcasted_iota(jnp.int32, (tq, LANES), 1) < HEAD_DIM
    ss = []
    for h in range(N_HEADS_DIL):
        sl = slice((h // 2) * LANES, (h // 2 + 1) * LANES)
        qm = jnp.where(low_half == (h % 2 == 0), q[:, sl], jnp.zeros((tq, LANES), q.dtype))
        ss.append(_dot_nt(qm, kwin[:, sl]) + bias_ref[h])
    s = jnp.concatenate(ss, axis=0) + edge
    m = jnp.max(s, axis=-1, keepdims=True)
    p = jnp.exp(s - m)
    l = jnp.sum(p, axis=-1, keepdims=True)
    lse = m + jnp.log(l)
    inv_l = 1.0 / l
    pb = p.astype(jnp.bfloat16)
    for pair in range(N_HEADS_DIL // 2):
        sl = slice(pair * LANES, (pair + 1) * LANES)
        r0 = slice(2 * pair * tq, (2 * pair + 1) * tq)
        r1 = slice((2 * pair + 1) * tq, (2 * pair + 2) * tq)
        a0 = _dot(pb[r0], vwin[:, sl]) * inv_l[r0]
        a1 = _dot(pb[r1], vwin[:, sl]) * inv_l[r1]
        o_ref[0, 0, :, sl] = jnp.where(low_half, a0, a1).astype(o_ref.dtype)
        lse_ref[0, 0, :, sl] = jnp.where(low_half, lse[r0], lse[r1])


def _dilated_stage(pa_d, dilation):
    B, d, L, _ = pa_d.shape
    tq = DIL_TQ
    w = DIL_WIDTH
    halo = DIL_SIDE
    n_halo = L // halo
    bias = _dilated_bias(dilation)

    def halo_spec(col, j):
        def imap(b, r, i):
            return (b, r, jnp.clip(i * (tq // halo) - 1 + j, 0, n_halo - 1), col)
        return pl.BlockSpec((1, 1, halo, w), imap)

    out_spec = pl.BlockSpec((1, 1, tq, w), lambda b, r, i: (b, r, i, 0))
    in_specs = ([pl.BlockSpec((1, 1, tq, w), lambda b, r, i: (b, r, i, 0))]
                + [halo_spec(1, j) for j in range(4)] + [halo_spec(2, j) for j in range(4)]
                + [pl.BlockSpec(bias.shape, lambda b, r, i: (0, 0, 0))])
    return pl.pallas_call(
        functools.partial(_dilated_kernel, seq_len=L),
        grid=(B, d, L // tq),
        in_specs=in_specs,
        out_specs=[out_spec, out_spec],
        out_shape=[jax.ShapeDtypeStruct((B, d, L, w), jnp.bfloat16),
                   jax.ShapeDtypeStruct((B, d, L, w), jnp.float32)],
        compiler_params=_cparams(("parallel", "parallel", "parallel")),
        name=f"dilated_d{d}",
    )(*([pa_d] * 9), bias)


def _dilated_bias(dilation):
    tq = DIL_TQ
    slopes = np.asarray([2.0 ** (-8.0 * (i + 1) / N_HEADS_DIL) for i in range(N_HEADS_DIL)], np.float32)
    rel = (np.arange(tq + 2 * DIL_SIDE)[None, :] - DIL_SIDE) - np.arange(tq)[:, None]
    band = np.abs(rel) <= DIL_SIDE
    pen = -(slopes * dilation)[:, None, None] * np.abs(rel).astype(np.float32)[None]
    return jnp.asarray(np.where(band[None], pen, np.float32(NEG_INF)), jnp.float32)


def _diff_kernel(reach_ref, q_ref, ka_hbm, vt_hbm, qpos_ref, dabs_ref, lq1_ref, lk1_ref, lq2_ref, lk2_ref,
                 g_ref, o_ref, qa_scr, m_scr, acc_scr, tmp_scr, k_buf, v_buf, sems, *, slopes, n_key_tiles):
    b = pl.program_id(0)
    h = pl.program_id(1)
    qi = pl.program_id(2)
    tq = q_ref.shape[2]
    tk = k_buf.shape[1]

    slope = jnp.float32(slopes[0])
    for hh in range(1, N_HEADS_DIFF):
        slope = jnp.where(h == hh, jnp.float32(slopes[hh]), slope)

    reach = reach_ref[b * N_HEADS_DIFF + h]
    first = jnp.maximum(qi - reach, 0)
    n_active = jnp.minimum(qi + reach, n_key_tiles - 1) - first + 1

    def key_tile(j):
        t = first + j - 1
        return jnp.where(j == 0, qi, jnp.where(t < qi, t, t + 1))

    def tile_copies(j, slot):
        start = pl.multiple_of(key_tile(j) * tk, tk)
        return (pltpu.make_async_copy(ka_hbm.at[b, h, pl.ds(start, tk), :], k_buf.at[slot], sems.at[0, slot]),
                pltpu.make_async_copy(vt_hbm.at[b, h, :, pl.ds(start, tk)], v_buf.at[slot], sems.at[1, slot]))

    for cp in tile_copies(0, 0):
        cp.start()

    @pl.when(n_active > 1)
    def _():
        for cp in tile_copies(1, 1):
            cp.start()

    q = q_ref[0, 0]
    low_half = lax.broadcasted_iota(jnp.int32, (tq, LANES), 1) < HEAD_DIM
    zero = jnp.zeros_like(q)
    qpos = qpos_ref[0]
    for c in range(2):
        qc = jnp.where(low_half == (c == 0), q, zero)
        for var, pos in enumerate((qpos, -qpos, jnp.zeros_like(qpos))):
            qa_scr[c, var, :, :LANES] = qc
            qa_scr[c, var, :, LANES:] = pos
    m_scr[...] = jnp.full(m_scr.shape, NEG_INF, jnp.float32)
    acc_scr[...] = jnp.zeros(acc_scr.shape, jnp.float32)


    def restabilise(slot, kt, diag):
        var = 2 if diag else jnp.where(kt < qi, 0, 1)
        off = -slope * jnp.abs(qi * tq - kt * tk).astype(jnp.float32)
        ka = k_buf[slot]
        vt = v_buf[slot]
        ss = []
        for c in range(2):
            s = _dot_nt(ka, qa_scr[c, var])
            if diag:
                s = s + dabs_ref[...] * slope
            ss.append(s)
        for c in range(2):
            s = ss[c]
            m_old = m_scr[c]
            m_new = jnp.maximum(m_old, jnp.max(s, axis=0, keepdims=True) + off)
            alpha = jnp.exp2(m_old - m_new)
            p = jnp.exp2((s - (m_new - off)).astype(jnp.bfloat16))
            acc_scr[c] = alpha * acc_scr[c] + _dot(vt, p)
            m_scr[c] = m_new

    for cp in tile_copies(0, 0):
        cp.wait()
    restabilise(0, qi, True)

    def visit(j, carry):
        slot = j % 2

        @pl.when(j + 1 < n_active)
        def _():
            for cp in tile_copies(j + 1, 1 - slot):
                cp.start()

        for cp in tile_copies(j, slot):
            cp.wait()
        kt = key_tile(j)
        var = jnp.where(kt < qi, 0, 1)
        off = -slope * jnp.abs(qi * tq - kt * tk).astype(jnp.float32)
        ka = k_buf[slot]
        vt = v_buf[slot]
        excess = None
        for c in range(2):
            s = _dot_nt(ka, qa_scr[c, var])
            m_eff = m_scr[c] - off
            over = jnp.max(jnp.max(s, axis=0, keepdims=True) - m_eff)
            excess = over if excess is None else jnp.maximum(excess, over)
            p = jnp.exp2((s - m_eff).astype(jnp.bfloat16))
            tmp_scr[c] = acc_scr[c] + _dot(vt, p)

        @pl.when(excess <= EXP2_HEADROOM)
        def _():
            acc_scr[...] = tmp_scr[...]

        @pl.when(excess > EXP2_HEADROOM)
        def _():
            restabilise(slot, kt, False)

        return carry

    lax.fori_loop(1, n_active, visit, 0)

    lam = (jnp.exp(jnp.sum(lq1_ref[...] * lk1_ref[...], axis=-1, keepdims=True))
           - jnp.exp(jnp.sum(lq2_ref[...] * lk2_ref[...], axis=-1, keepdims=True)) + LAM_INIT)
    a1, a2 = acc_scr[0], acc_scr[1]
    o = (a1[:DIFF_VDIM] / a1[DIFF_VDIM:DIFF_VDIM + 1]
         - lam * (a2[:DIFF_VDIM] / a2[DIFF_VDIM:DIFF_VDIM + 1]))
    o = o * lax.rsqrt(jnp.mean(o * o, axis=0, keepdims=True) + LN_EPS) * g_ref[...]
    o_ref[0] = (o * (1.0 - LAM_INIT)).T.astype(o_ref.dtype)


def _tile_reach(qn, kn, n_tiles):
    B = qn.shape[0]
    ng = 2 * N_HEADS_DIFF
    qmax = jnp.sqrt(qn[:, 0, :ng]) * NORM_SLACK
    kmax = jnp.sqrt(kn[:, 0, :ng]) * NORM_SLACK
    bound = jnp.max((2.0 * qmax * kmax).reshape(B, N_HEADS_DIFF, 2), axis=-1) + UNDERFLOW_LOG2
    far = bound / jnp.asarray(_diff_slopes_log2())[None, :]
    reach = jnp.floor((far - 1.0) / DIFF_TK) + 1.0
    reach = jnp.where(jnp.isfinite(reach), reach, n_tiles)
    return jnp.clip(reach, 0, n_tiles).astype(jnp.int32).reshape(-1)


def _diff_attention(reach, qd, ka, vt, qpos, dabs, lq1, lk1, lq2, lk2, g_col):
    B, H, S, _ = qd.shape
    tq, tk = DIFF_TQ, DIFF_TK
    slopes = tuple(float(s) for s in _diff_slopes_log2())
    nk = S // tk
    small = lambda a: pl.BlockSpec(a.shape, lambda b, h, qi, reach: (0, 0))
    grid_spec = pltpu.PrefetchScalarGridSpec(
        num_scalar_prefetch=1,
        grid=(B, H, S // tq),
        in_specs=[
            pl.BlockSpec((1, 1, tq, LANES), lambda b, h, qi, reach: (b, h, qi, 0)),
            pl.BlockSpec(memory_space=pl.ANY),
            pl.BlockSpec(memory_space=pl.ANY),
            pl.BlockSpec((1, tq, LANES), lambda b, h, qi, reach: (h, 0, 0)),
            small(dabs), small(lq1), small(lk1), small(lq2), small(lk2), small(g_col),
        ],
        out_specs=pl.BlockSpec((1, tq, LANES), lambda b, h, qi, reach: (b, qi, h)),
        scratch_shapes=[
            pltpu.VMEM((2, 3, tq, 2 * LANES), jnp.bfloat16),
            pltpu.VMEM((2, 1, tq), jnp.float32),
            pltpu.VMEM((2, V_ROWS, tq), jnp.float32),
            pltpu.VMEM((2, V_ROWS, tq), jnp.float32),
            pltpu.VMEM((2, tk, 2 * LANES), jnp.bfloat16),
            pltpu.VMEM((2, V_ROWS, tk), jnp.bfloat16),
            pltpu.SemaphoreType.DMA((2, 2)),
        ],
    )
    return pl.pallas_call(
        functools.partial(_diff_kernel, slopes=slopes, n_key_tiles=nk),
        grid_spec=grid_spec,
        out_shape=jax.ShapeDtypeStruct((B, S, H * LANES), jnp.bfloat16),
        compiler_params=_cparams(("parallel", "parallel", "arbitrary")),
        name="diff_attention",
    )(reach, qd, ka, vt, qpos, dabs, lq1, lk1, lq2, lk2, g_col)


def _diff_slopes_log2():
    return np.asarray([2.0 ** (-8.0 * (i + 1) / N_HEADS_DIFF) for i in range(N_HEADS_DIFF)],
                      np.float32) * np.float32(LOG2E)


def _position_columns():
    def bf16_round(v):
        u = np.asarray(v, np.float32).view(np.uint32)
        return ((u + 0x7FFF + ((u >> 16) & 1)) & np.uint32(0xFFFF0000)).view(np.float32)

    kpos = np.zeros((N_HEADS_DIFF, DIFF_TK, LANES), np.float32)
    qpos = np.zeros((N_HEADS_DIFF, DIFF_TQ, LANES), np.float32)
    pk, pq = np.arange(DIFF_TK), np.arange(DIFF_TQ)
    for h, a in enumerate(_diff_slopes_log2()):
        rest = np.float32(a)
        for n in range(3):
            a_n = np.float32(bf16_round(rest))
            rest = np.float32(rest - a_n)
            c = 4 * n
            kpos[h, :, c + 0] = (pk // POS_SPLIT) * POS_SPLIT
            kpos[h, :, c + 1] = pk % POS_SPLIT
            kpos[h, :, c + 2] = a_n
            kpos[h, :, c + 3] = a_n
            qpos[h, :, c + 0] = a_n
            qpos[h, :, c + 1] = a_n
            qpos[h, :, c + 2] = -((pq // POS_SPLIT) * POS_SPLIT)
            qpos[h, :, c + 3] = -(pq % POS_SPLIT)
    dabs = -np.abs(pq[:, None] - pk[None, :]).astype(np.float32)
    return jnp.asarray(kpos, jnp.bfloat16), jnp.asarray(qpos, jnp.bfloat16), jnp.asarray(dabs)


def _post_attn_kernel(o1_ref, l1_ref, o4_ref, l4_ref, o16_ref, l16_ref, od_ref, x_ref, wo_ref, g_ref, b_ref,
                      wr_ref, br_ref, tri_ref,
                      x1_ref, x1t_ref, xs0_hbm, idx_ref, gate_ref, rank_ref, cnt_ref, carry_scr, order_scr,
                      zero_scr, zero_sem, *, n_fill):
    step = pl.program_id(0)

    @pl.when(step == 0)
    def _():
        carry_scr[...] = jnp.zeros(carry_scr.shape, jnp.float32)
        zero_scr[...] = jnp.zeros(zero_scr.shape, zero_scr.dtype)

    zrows = zero_scr.shape[0]

    def fill_copy(j):
        start = pl.multiple_of((step * n_fill + j) * zrows, zrows)
        return pltpu.make_async_copy(zero_scr, xs0_hbm.at[pl.ds(start, zrows)], zero_sem)

    for j in range(n_fill):
        fill_copy(j).start()

    tm = x_ref.shape[0]
    nc = DIL_WIDTH // LANES
    for n, (o_ref, l_ref) in enumerate(((o4_ref, l4_ref), (o16_ref, l16_ref))):
        d = o_ref.shape[1]
        for r in range(d):
            o_r = o_ref[0, r].astype(jnp.float32)
            l_r = l_ref[0, r]
            for c in range(nc):
                cols = slice(c * LANES, (c + 1) * LANES)
                order_scr[2 * n, c, pl.ds(r, tm // d, stride=d), :] = o_r[:, cols]
                order_scr[2 * n + 1, c, pl.ds(r, tm // d, stride=d), :] = l_r[:, cols]
    in_order = lambda n: jnp.concatenate([order_scr[n, c] for c in range(nc)], axis=1)
    outs = (o1_ref[0, 0].astype(jnp.float32), in_order(0), in_order(2))
    lses = (l1_ref[0, 0], in_order(1), in_order(3))
    top = jnp.maximum(jnp.maximum(lses[0], lses[1]), lses[2])
    wts = [jnp.exp(l - top) for l in lses]
    oa = ((wts[0] * outs[0] + wts[1] * outs[1] + wts[2] * outs[2])
          / (wts[0] + wts[1] + wts[2])).astype(jnp.bfloat16)

    mix = _dot(oa, wo_ref[:DIL_WIDTH, :]) + _dot(od_ref[...], wo_ref[DIL_WIDTH:, :])
    x1 = _layer_norm(DEEPNORM_ALPHA * x_ref[...] + mix, g_ref[...], b_ref[...])
    x1_ref[...] = x1
    for s in range(ROW_SUBLANES):
        x1t_ref[pl.ds(s, x1.shape[0], stride=ROW_SUBLANES), :] = x1[:, s * LANES:(s + 1) * LANES]

    tm = x1.shape[0]
    lg = _dot_nt(wr_ref[...], x1, precision=lax.Precision.HIGHEST) + br_ref[...]
    eidx = lax.broadcasted_iota(jnp.int32, (N_EXPERTS, tm), 0)
    vals, sels = [], []
    for k in range(TOP_K):
        mx = jnp.max(lg, axis=0, keepdims=True)
        idx = jnp.min(jnp.where(lg == mx, eidx, N_EXPERTS), axis=0, keepdims=True)
        sel = eidx == idx
        vals.append(mx)
        sels.append(sel)
        idx_ref[k:k + 1, :] = idx
        lg = jnp.where(sel, -jnp.inf, lg)
    ex = [jnp.exp(v - vals[0]) for v in vals]
    den = ex[0] + ex[1] + ex[2] + ex[3]
    for k in range(TOP_K):
        gate_ref[k:k + 1, :] = ex[k] / den

    chosen = (sels[0] | sels[1] | sels[2] | sels[3])
    onehot = jnp.where(chosen, 1.0, 0.0)
    before = _dot(onehot.astype(jnp.bfloat16), tri_ref[...]) + carry_scr[...]
    for k in range(TOP_K):
        rank_ref[k:k + 1, :] = jnp.sum(jnp.where(sels[k], before, 0.0), axis=0,
                                       keepdims=True).astype(jnp.int32)
    carry_scr[...] = carry_scr[...] + jnp.sum(onehot, axis=1, keepdims=True)
    pad = jnp.zeros((8 - TOP_K, tm), jnp.int32)
    idx_ref[TOP_K:, :] = pad
    rank_ref[TOP_K:, :] = pad
    gate_ref[TOP_K:, :] = pad.astype(jnp.float32)
    cnt_ref[...] = jnp.broadcast_to(carry_scr[...], cnt_ref.shape).astype(jnp.int32)
    for j in range(n_fill):
        fill_copy(j).wait()


def _post_attn(dil, od, x, wo_bf, g, b, wr_t, br, tri, n_rows):
    T, D = x.shape
    assert D == ROW_SUBLANES * LANES
    tm = ROW_TILE
    nb = dil[0][0].shape[2] // tm
    row = lambda w: pl.BlockSpec((tm, w), lambda i: (i, 0))
    full = lambda a: pl.BlockSpec(a.shape, lambda i: (0, 0))
    col = pl.BlockSpec((8, tm), lambda i: (0, i))
    dil_specs, dil_args = [], []
    for o_d, lse_d in dil:
        d = o_d.shape[1]
        spec = pl.BlockSpec((1, d, tm // d, DIL_WIDTH), lambda i: (i // nb, 0, i % nb, 0))
        dil_specs += [spec, spec]
        dil_args += [o_d, lse_d]
    steps = T // tm
    fill_rows, rem = divmod(n_rows * ROW_SUBLANES, steps * ZERO_FILL_COPIES)
    assert rem == 0 and fill_rows % ROW_SUBLANES == 0
    return pl.pallas_call(
        functools.partial(_post_attn_kernel, n_fill=ZERO_FILL_COPIES),
        grid=(steps,),
        in_specs=dil_specs + [row(DIFF_WIDTH), row(D), full(wo_bf), full(g), full(b), full(wr_t),
                              full(br), full(tri)],
        out_specs=[row(D), pl.BlockSpec((tm * ROW_SUBLANES, LANES), lambda i: (i, 0)),
                   pl.BlockSpec(memory_space=pl.ANY), col, col, col,
                   pl.BlockSpec((N_EXPERTS, LANES), lambda i: (0, 0))],
        out_shape=[
            jax.ShapeDtypeStruct((T, D), jnp.float32),
            jax.ShapeDtypeStruct((T * ROW_SUBLANES, LANES), jnp.float32),
            jax.ShapeDtypeStruct((n_rows * ROW_SUBLANES, LANES), jnp.float32),
            jax.ShapeDtypeStruct((8, T), jnp.int32),
            jax.ShapeDtypeStruct((8, T), jnp.float32),
            jax.ShapeDtypeStruct((8, T), jnp.int32),
            jax.ShapeDtypeStruct((N_EXPERTS, LANES), jnp.int32),
        ],
        scratch_shapes=[pltpu.VMEM((N_EXPERTS, 1), jnp.float32),
                        pltpu.VMEM((4, DIL_WIDTH // LANES, tm, LANES), jnp.float32),
                        pltpu.VMEM((fill_rows, LANES), jnp.float32), pltpu.SemaphoreType.DMA(())],
        compiler_params=_cparams(("arbitrary",)),
        name="post_attn_router",
    )(*dil_args, od, x, wo_bf, g, b, wr_t, br, tri)


def _fetch_rows_index(dest_hbm, step, idx_smem, sem):
    cp = pltpu.make_async_copy(dest_hbm.at[step], idx_smem, sem)
    cp.start()
    cp.wait()


def _row_tile(ref, row):
    return ref.at[pl.ds(pl.multiple_of(row * ROW_SUBLANES, ROW_SUBLANES), ROW_SUBLANES)]


def _rows_from_tiles(ref, n):
    return jnp.concatenate([ref[pl.ds(s, n, stride=ROW_SUBLANES), :] for s in range(ROW_SUBLANES)], axis=1)


def _rows_to_tiles(ref, rows):
    for s in range(ROW_SUBLANES):
        ref[pl.ds(s, rows.shape[0], stride=ROW_SUBLANES), :] = rows[:, s * LANES:(s + 1) * LANES]


def _dispatch_kernel(dest_hbm, x_ref, xs_in_hbm, xs_hbm, idx_smem, idx_sem, row_sem):
    del xs_in_hbm
    tm = x_ref.shape[0] // ROW_SUBLANES
    _fetch_rows_index(dest_hbm, pl.program_id(0), idx_smem, idx_sem)

    def issue(t, carry):
        for k in range(TOP_K):
            pltpu.make_async_copy(_row_tile(x_ref, t), _row_tile(xs_hbm, idx_smem[k * tm + t]),
                                  row_sem).start(priority=k % 2)
        return carry

    lax.fori_loop(0, tm, issue, 0, unroll=8)
    for k in range(TOP_K):
        pltpu.make_async_copy(x_ref, xs_hbm.at[pl.ds(0, tm * ROW_SUBLANES)], row_sem).wait()


def _dispatch(dest_steps, x1t, xs0):
    n_steps, per_step = dest_steps.shape
    tm = per_step // TOP_K
    return pl.pallas_call(
        _dispatch_kernel,
        grid=(n_steps,),
        in_specs=[
            pl.BlockSpec(memory_space=pl.ANY),
            pl.BlockSpec((tm * ROW_SUBLANES, LANES), lambda i: (i, 0)),
            pl.BlockSpec(memory_space=pl.ANY),
        ],
        out_specs=pl.BlockSpec(memory_space=pl.ANY),
        out_shape=jax.ShapeDtypeStruct(xs0.shape, xs0.dtype),
        scratch_shapes=[pltpu.SMEM((per_step,), jnp.int32), pltpu.SemaphoreType.DMA(()),
                        pltpu.SemaphoreType.DMA(())],
        input_output_aliases={2: 0},
        compiler_params=_cparams(("arbitrary",)),
        name="dispatch_rows",
    )(dest_steps, x1t, xs0)


def _expert_kernel(tile_e_ref, n_used_ref, xs_ref, wu_ref, bu_ref, wd_ref, bd_ref, y_ref, wu_scr, wd_scr):
    i = pl.program_id(0)
    used = i < n_used_ref[0]
    new_expert = (i == 0) | (tile_e_ref[i] != tile_e_ref[jnp.maximum(i - 1, 0)])
    tm = xs_ref.shape[0] // ROW_SUBLANES

    @pl.when(used & new_expert)
    def _():
        wu_scr[...] = wu_ref[0].astype(jnp.bfloat16)
        wd_scr[...] = wd_ref[0].astype(jnp.bfloat16)

    @pl.when(used)
    def _():
        xs = _rows_from_tiles(xs_ref, tm).astype(jnp.bfloat16)
        hu = _dot(xs, wu_scr[...]) + bu_ref[0]
        g = jnp.minimum(hu[:, :D_FF], SWIGLU_LIMIT)
        u = jnp.clip(hu[:, D_FF:], -SWIGLU_LIMIT, SWIGLU_LIMIT)
        act = g * (1.0 / (1.0 + jnp.exp(-SWIGLU_ALPHA * g))) * (u + 1.0)
        _rows_to_tiles(y_ref, _dot(act.astype(jnp.bfloat16), wd_scr[...]) + bd_ref[0])

    @pl.when(jnp.logical_not(used))
    def _():
        y_ref[...] = jnp.zeros(y_ref.shape, y_ref.dtype)


def _experts(tile_e, n_used, xs, wu, bu, wd, bd):
    D = ROW_SUBLANES * LANES
    n_rows = xs.shape[0] // ROW_SUBLANES
    tm = MOE_TILE
    tile_spec = pl.BlockSpec((tm * ROW_SUBLANES, LANES), lambda i, te, nu: (i, 0))
    grid_spec = pltpu.PrefetchScalarGridSpec(
        num_scalar_prefetch=2,
        grid=(n_rows // tm,),
        in_specs=[
            tile_spec,
            pl.BlockSpec((1, D, 2 * D_FF), lambda i, te, nu: (te[i], 0, 0)),
            pl.BlockSpec((1, 1, 2 * D_FF), lambda i, te, nu: (te[i], 0, 0)),
            pl.BlockSpec((1, D_FF, D), lambda i, te, nu: (te[i], 0, 0)),
            pl.BlockSpec((1, 1, D), lambda i, te, nu: (te[i], 0, 0)),
        ],
        out_specs=tile_spec,
        scratch_shapes=[pltpu.VMEM((D, 2 * D_FF), jnp.bfloat16), pltpu.VMEM((D_FF, D), jnp.bfloat16)],
    )
    return pl.pallas_call(
        _expert_kernel,
        grid_spec=grid_spec,
        out_shape=jax.ShapeDtypeStruct(xs.shape, jnp.float32),
        compiler_params=_cparams(("arbitrary",)),
        name="experts",
    )(tile_e, n_used, xs, wu, bu, wd, bd)


def _combine_kernel(dest_hbm, y_hbm, x1_ref, gate_ref, g_ref, b_ref, o_ref, buf, idx_smem, idx_sem, row_sems):
    i = pl.program_id(0)
    n = pl.num_programs(0)
    tm = x1_ref.shape[0]

    def issue(step, slot):
        _fetch_rows_index(dest_hbm, step, idx_smem, idx_sem)

        def body(t, carry):
            for k in range(TOP_K):
                pltpu.make_async_copy(_row_tile(y_hbm, idx_smem[k * tm + t]), _row_tile(buf.at[slot, k], t),
                                      row_sems.at[slot]).start(priority=k % 2)
            return carry

        lax.fori_loop(0, tm, body, 0, unroll=8)

    @pl.when(i == 0)
    def _():
        issue(0, 0)

    @pl.when(i + 1 < n)
    def _():
        issue(i + 1, (i + 1) % 2)

    slot = i % 2
    for k in range(TOP_K):
        pltpu.make_async_copy(y_hbm.at[pl.ds(0, tm * ROW_SUBLANES)], buf.at[slot, k], row_sems.at[slot]).wait()
    ffn = _rows_from_tiles(buf.at[slot, 0], tm) * gate_ref[:, 0:1]
    for k in range(1, TOP_K):
        ffn = ffn + _rows_from_tiles(buf.at[slot, k], tm) * gate_ref[:, k:k + 1]
    o_ref[...] = _layer_norm(DEEPNORM_ALPHA * x1_ref[...] + ffn, g_ref[...], b_ref[...])


def _combine(dest_steps, y, x1, gates_t, g, b):
    T, D = x1.shape
    n_steps, per_step = dest_steps.shape
    tm = per_step // TOP_K
    full = lambda a: pl.BlockSpec(a.shape, lambda i: (0, 0))
    return pl.pallas_call(
        _combine_kernel,
        grid=(n_steps,),
        in_specs=[
            pl.BlockSpec(memory_space=pl.ANY),
            pl.BlockSpec(memory_space=pl.ANY),
            pl.BlockSpec((tm, D), lambda i: (i, 0)),
            pl.BlockSpec((tm, 8), lambda i: (i, 0)),
            full(g), full(b),
        ],
        out_specs=pl.BlockSpec((tm, D), lambda i: (i, 0)),
        out_shape=jax.ShapeDtypeStruct((T, D), jnp.float32),
        scratch_shapes=[pltpu.VMEM((2, TOP_K, tm * ROW_SUBLANES, LANES), jnp.float32),
                        pltpu.SMEM((per_step,), jnp.int32),
                        pltpu.SemaphoreType.DMA(()), pltpu.SemaphoreType.DMA((2,))],
        compiler_params=_cparams(("arbitrary",)),
        name="combine_ln2",
    )(dest_steps, y, x1, gates_t, g, b)


def kernel(x, w_in, w_out, lambda_q1, lambda_k1, lambda_q2, lambda_k2, diff_norm_g, ln1_g, ln1_b,
           w_router, b_router, w_up, b_up, w_down, b_down, ln2_g, ln2_b):
    B, S, D = x.shape
    T = B * S
    assert w_in.shape[0] == 1, "single layer"
    assert S % DIFF_TQ == 0 and DIFF_TK == DIFF_TQ and DIFF_TK % ROW_TILE == 0 and S % (16 * DIL_TQ) == 0

    na = 3 * DIL_WIDTH
    colscale = np.ones((3 * D,), np.float32)
    colscale[:DIL_WIDTH] = HEAD_DIM ** -0.5
    colscale[na:na + DIFF_WIDTH] = HEAD_DIM ** -0.5 * LOG2E
    w_in_bf = (w_in[0] * colscale).astype(jnp.bfloat16)
    nv = na + 2 * DIFF_WIDTH

    kpos, qpos, dabs = _position_columns()
    pa, pa4, pa16, qd, ka, vt, qn, kn = _in_proj(x, w_in_bf[:, :nv], w_in_bf[:, nv:].T, kpos)
    dil = [_dilated_stage(p, d) for p, (_, d) in zip((pa[:, None], pa4, pa16), DIL_PATTERNS)]

    r2 = lambda a: a.reshape(1, -1).astype(jnp.float32)
    od = _diff_attention(_tile_reach(qn, kn, S // DIFF_TK), qd, ka, vt, qpos, dabs, r2(lambda_q1[0]), r2(lambda_k1[0]), r2(lambda_q2[0]),
                         r2(lambda_k2[0]), diff_norm_g[0].reshape(-1, 1).astype(jnp.float32))

    tri = jnp.asarray(np.triu(np.ones((ROW_TILE, ROW_TILE), np.float32), k=1), jnp.bfloat16)
    n_rows = T * TOP_K + N_EXPERTS * MOE_TILE
    n_tiles = n_rows // MOE_TILE
    x1, x1t, xs0, top_idx, gates, rank, counts = _post_attn(
        dil, od.reshape(T, DIFF_WIDTH), x.reshape(T, D), w_out[0].astype(jnp.bfloat16),
        r2(ln1_g[0]), r2(ln1_b[0]), w_router[0].T, b_router[0].reshape(N_EXPERTS, 1), tri, n_rows)

    counts = counts[:, 0]
    pcounts = ((counts + MOE_TILE - 1) // MOE_TILE) * MOE_TILE
    pends = jnp.cumsum(pcounts)
    pstarts = pends - pcounts
    dest = rank[:TOP_K]
    for e in range(N_EXPERTS):
        dest = dest + jnp.where(top_idx[:TOP_K] == e, pstarts[e], 0)
    tile_starts = jnp.arange(n_tiles, dtype=jnp.int32) * MOE_TILE
    tile_e = jnp.minimum(jnp.sum(pends[None, :] <= tile_starts[:, None], axis=1), N_EXPERTS - 1).astype(jnp.int32)
    n_used = (pends[-1] // MOE_TILE).astype(jnp.int32).reshape(1)

    def per_step(tm):
        return dest.reshape(TOP_K, T // tm, tm).transpose(1, 0, 2).reshape(T // tm, TOP_K * tm)

    xs = _dispatch(per_step(ROW_TILE), x1t, xs0)
    y = _experts(tile_e, n_used, xs, w_up[0], b_up[0][:, None, :], w_down[0], b_down[0][:, None, :])
    out = _combine(per_step(COMBINE_TILE), y, x1, gates.T, r2(ln2_g[0]), r2(ln2_b[0]))
    return out.reshape(B, S, D)
```

```python
import functools
import math

import jax
import jax.numpy as jnp
import numpy as np
from jax import lax
from jax.experimental import pallas as pl
from jax.experimental.pallas import tpu as pltpu

D_MODEL = 1024
HEAD_DIM = 64
DIL_WIDTH = 512
N_HEADS_DIL = 8
DIL_PATTERNS = ((128, 1), (512, 4), (2048, 16))
DIL_SIDE = 64
DIFF_WIDTH = 512
N_HEADS_DIFF = 4
DIFF_VDIM = 2 * HEAD_DIM
N_EXPERTS = 32
TOP_K = 4
D_FF = D_MODEL
SWIGLU_ALPHA = 1.702
SWIGLU_LIMIT = 7.0
DEEPNORM_ALPHA = 2.0 ** 0.25
LN_EPS = 1e-5
NEG_INF = -1e30
LAM_INIT = 0.8 - 0.6 * math.exp(-0.3 * 0)
LOG2E = math.log2(math.e)

LANES = 128
VMEM_LIMIT = 56 * 1024 * 1024

ROW_TILE = 512
DIFF_TQ = 1024
DIFF_TK = 1024
V_ROWS = DIFF_VDIM + 16
EXP2_HEADROOM = 4.0
UNDERFLOW_LOG2 = 160.0
NORM_SLACK = 1.01
DIL_TQ = 128
DIL_QB = 2
MOE_TILE = 512
COMBINE_TILE = 256
ROW_SUBLANES = 8
ZERO_FILL_COPIES = 8
POS_SPLIT = 16


def _cparams(sem):
    return pltpu.CompilerParams(dimension_semantics=sem, vmem_limit_bytes=VMEM_LIMIT)


def _dot_nt(a, b, **kw):
    return lax.dot_general(a, b, (((1,), (1,)), ((), ())), preferred_element_type=jnp.float32, **kw)


def _dot(a, b, **kw):
    return jnp.dot(a, b, preferred_element_type=jnp.float32, **kw)


def _layer_norm(y, g, b):
    mu = jnp.mean(y, axis=-1, keepdims=True)
    yc = y - mu
    var = jnp.mean(yc * yc, axis=-1, keepdims=True)
    return yc * lax.rsqrt(var + LN_EPS) * g + b


def _in_proj_kernel(x_ref, w_ref, wvt_ref, kpos_ref, grp_ref, pa_ref, pa4_ref, pa16_ref, qd_ref, ka_ref,
                    vt_ref, qn_ref, kn_ref, pa_scr):
    xb = x_ref[0].astype(jnp.bfloat16)
    na = 3 * DIL_WIDTH
    tm = xb.shape[0]
    paf = _dot(xb, w_ref[:, :na])
    pa_ref[0] = paf.astype(jnp.bfloat16)
    for c in range(na // LANES):
        cols = slice(c * LANES, (c + 1) * LANES)
        pa_scr[c] = paf[:, cols]
        for d, ref in ((DIL_PATTERNS[1][1], pa4_ref), (DIL_PATTERNS[2][1], pa16_ref)):
            for r in range(d):
                ref[0, r, :, cols] = pa_scr[c, pl.ds(r, tm // d, stride=d), :].astype(jnp.bfloat16)
    q = _dot(xb, w_ref[:, na:na + DIFF_WIDTH]).astype(jnp.bfloat16)
    k = _dot(xb, w_ref[:, na + DIFF_WIDTH:]).astype(jnp.bfloat16)
    vt = _dot_nt(wvt_ref[...], xb).astype(jnp.bfloat16)
    ones = jnp.ones((V_ROWS - DIFF_VDIM, tm), jnp.bfloat16)

    @pl.when(pl.program_id(1) == 0)
    def _():
        qn_ref[...] = jnp.zeros(qn_ref.shape, jnp.float32)
        kn_ref[...] = jnp.zeros(kn_ref.shape, jnp.float32)

    for src, dst in ((q, qn_ref), (k, kn_ref)):
        f = src.astype(jnp.float32)
        sq = f * f
        sq_hi = sq.astype(jnp.bfloat16)
        sq_lo = (sq - sq_hi.astype(jnp.float32)).astype(jnp.bfloat16)
        gs = _dot(sq_hi, grp_ref[...]) + _dot(sq_lo, grp_ref[...])
        dst[0] = jnp.maximum(dst[0], jnp.max(gs, axis=0, keepdims=True))
    for h in range(N_HEADS_DIFF):
        sl = slice(h * LANES, (h + 1) * LANES)
        qd_ref[0, h] = q[:, sl]
        ka_ref[0, h, :, :LANES] = k[:, sl]
        ka_ref[0, h, :, LANES:] = kpos_ref[h]
        vt_ref[0, h, :DIFF_VDIM, :] = vt[sl, :]
        vt_ref[0, h, DIFF_VDIM:, :] = ones


def _in_proj(x, w_bf, wvt_bf, kpos):
    B, S, D = x.shape
    tm = ROW_TILE
    na = 3 * DIL_WIDTH
    grp = np.zeros((DIFF_WIDTH, LANES), np.float32)
    grp[np.arange(DIFF_WIDTH), np.arange(DIFF_WIDTH) // HEAD_DIM] = 1.0
    grp = jnp.asarray(grp, jnp.bfloat16)
    norm_spec = pl.BlockSpec((1, 1, LANES), lambda b, i: (b, 0, 0))
    d4, d16 = DIL_PATTERNS[1][1], DIL_PATTERNS[2][1]
    return pl.pallas_call(
        _in_proj_kernel,
        grid=(B, S // tm),
        in_specs=[
            pl.BlockSpec((1, tm, D), lambda b, i: (b, i, 0)),
            pl.BlockSpec(w_bf.shape, lambda b, i: (0, 0)),
            pl.BlockSpec(wvt_bf.shape, lambda b, i: (0, 0)),
            pl.BlockSpec((N_HEADS_DIFF, tm, LANES), lambda b, i: (0, i % (DIFF_TK // tm), 0)),
            pl.BlockSpec(grp.shape, lambda b, i: (0, 0)),
        ],
        out_specs=[
            pl.BlockSpec((1, tm, na), lambda b, i: (b, i, 0)),
            pl.BlockSpec((1, d4, tm // d4, na), lambda b, i: (b, 0, i, 0)),
            pl.BlockSpec((1, d16, tm // d16, na), lambda b, i: (b, 0, i, 0)),
            pl.BlockSpec((1, N_HEADS_DIFF, tm, LANES), lambda b, i: (b, 0, i, 0)),
            pl.BlockSpec((1, N_HEADS_DIFF, tm, 2 * LANES), lambda b, i: (b, 0, i, 0)),
            pl.BlockSpec((1, N_HEADS_DIFF, V_ROWS, tm), lambda b, i: (b, 0, 0, i)),
            norm_spec, norm_spec,
        ],
        out_shape=[
            jax.ShapeDtypeStruct((B, S, na), jnp.bfloat16),
            jax.ShapeDtypeStruct((B, d4, S // d4, na), jnp.bfloat16),
            jax.ShapeDtypeStruct((B, d16, S // d16, na), jnp.bfloat16),
            jax.ShapeDtypeStruct((B, N_HEADS_DIFF, S, LANES), jnp.bfloat16),
            jax.ShapeDtypeStruct((B, N_HEADS_DIFF, S, 2 * LANES), jnp.bfloat16),
            jax.ShapeDtypeStruct((B, N_HEADS_DIFF, V_ROWS, S), jnp.bfloat16),
            jax.ShapeDtypeStruct((B, 1, LANES), jnp.float32),
            jax.ShapeDtypeStruct((B, 1, LANES), jnp.float32),
        ],
        scratch_shapes=[pltpu.VMEM((na // LANES, tm, LANES), jnp.float32)],
        compiler_params=_cparams(("parallel", "arbitrary")),
        name="in_proj",
    )(x, w_bf, wvt_bf, kpos, grp)


def _dilated_kernel(*refs, seq_len):
    n_halo = 2 * DIL_QB + 2
    q_ref, k_refs, v_refs = refs[0], refs[1:1 + n_halo], refs[1 + n_halo:1 + 2 * n_halo]
    bias_ref, o_ref, lse_ref = refs[1 + 2 * n_halo:]
    tq = DIL_TQ
    nk = tq + 2 * DIL_SIDE
    i = pl.program_id(2)
    q = q_ref[0, 0]
    kwin = jnp.concatenate([r[0, 0] for r in k_refs], axis=0)
    vwin = jnp.concatenate([r[0, 0] for r in v_refs], axis=0)
    low_half = lax.broadcasted_iota(jnp.int32, (tq, LANES), 1) < HEAD_DIM
    ss = []
    for u in range(DIL_QB):
        key_pos = (i * DIL_QB + u) * tq - DIL_SIDE + lax.broadcasted_iota(jnp.int32, (1, nk), 1)
        edge = jnp.where((key_pos >= 0) & (key_pos < seq_len), 0.0, NEG_INF)
        for h in range(N_HEADS_DIL):
            sl = slice((h // 2) * LANES, (h // 2 + 1) * LANES)
            qu = q[u * tq:(u + 1) * tq, sl]
            qm = jnp.where(low_half == (h % 2 == 0), qu, jnp.zeros((tq, LANES), q.dtype))
            ss.append(_dot_nt(qm, kwin[u * tq:u * tq + nk, sl]) + bias_ref[h] + edge)
    s = jnp.concatenate(ss, axis=0)
    m = jnp.max(s, axis=-1, keepdims=True)
    p = jnp.exp(s - m)
    l = jnp.sum(p, axis=-1, keepdims=True)
    lse = m + jnp.log(l)
    inv_l = 1.0 / l
    pb = p.astype(jnp.bfloat16)
    for u in range(DIL_QB):
        for pair in range(N_HEADS_DIL // 2):
            sl = slice(pair * LANES, (pair + 1) * LANES)
            base = (u * N_HEADS_DIL + 2 * pair) * tq
            r0 = slice(base, base + tq)
            r1 = slice(base + tq, base + 2 * tq)
            vu = vwin[u * tq:u * tq + nk, sl]
            a0 = _dot(pb[r0], vu) * inv_l[r0]
            a1 = _dot(pb[r1], vu) * inv_l[r1]
            rows = slice(u * tq, (u + 1) * tq)
            o_ref[0, 0, rows, sl] = jnp.where(low_half, a0, a1).astype(o_ref.dtype)
            lse_ref[0, 0, rows, sl] = jnp.where(low_half, lse[r0], lse[r1])


def _dilated_stage(pa_d, dilation):
    B, d, L, _ = pa_d.shape
    tb = DIL_QB * DIL_TQ
    w = DIL_WIDTH
    halo = DIL_SIDE
    n_halo = L // halo
    per_step = tb // halo + 2
    bias = _dilated_bias(dilation)
    assert L % tb == 0

    def halo_spec(col, j):
        def imap(b, r, i):
            return (b, r, jnp.clip(i * (tb // halo) - 1 + j, 0, n_halo - 1), col)
        return pl.BlockSpec((1, 1, halo, w), imap)

    out_spec = pl.BlockSpec((1, 1, tb, w), lambda b, r, i: (b, r, i, 0))
    in_specs = ([pl.BlockSpec((1, 1, tb, w), lambda b, r, i: (b, r, i, 0))]
                + [halo_spec(1, j) for j in range(per_step)] + [halo_spec(2, j) for j in range(per_step)]
                + [pl.BlockSpec(bias.shape, lambda b, r, i: (0, 0, 0))])
    return pl.pallas_call(
        functools.partial(_dilated_kernel, seq_len=L),
        grid=(B, d, L // tb),
        in_specs=in_specs,
        out_specs=[out_spec, out_spec],
        out_shape=[jax.ShapeDtypeStruct((B, d, L, w), jnp.bfloat16),
                   jax.ShapeDtypeStruct((B, d, L, w), jnp.float32)],
        compiler_params=_cparams(("parallel", "parallel", "parallel")),
        name=f"dilated_d{d}",
    )(*([pa_d] * (1 + 2 * per_step)), bias)


def _dilated_bias(dilation):
    tq = DIL_TQ
    slopes = np.asarray([2.0 ** (-8.0 * (i + 1) / N_HEADS_DIL) for i in range(N_HEADS_DIL)], np.float32)
    rel = (np.arange(tq + 2 * DIL_SIDE)[None, :] - DIL_SIDE) - np.arange(tq)[:, None]
    band = np.abs(rel) <= DIL_SIDE
    pen = -(slopes * dilation)[:, None, None] * np.abs(rel).astype(np.float32)[None]
    return jnp.asarray(np.where(band[None], pen, np.float32(NEG_INF)), jnp.float32)


def _diff_kernel(reach_ref, q_ref, ka_hbm, vt_hbm, qpos_ref, dabs_ref, lq1_ref, lk1_ref, lq2_ref, lk2_ref,
                 g_ref, o_ref, qa_scr, m_scr, acc_scr, tmp_scr, k_buf, v_buf, sems, *, slopes, n_key_tiles):
    b = pl.program_id(0)
    h = pl.program_id(1)
    qi = pl.program_id(2)
    tq = q_ref.shape[2]
    tk = k_buf.shape[1]

    slope = jnp.float32(slopes[0])
    for hh in range(1, N_HEADS_DIFF):
        slope = jnp.where(h == hh, jnp.float32(slopes[hh]), slope)

    reach = reach_ref[b * N_HEADS_DIFF + h]
    first = jnp.maximum(qi - reach, 0)
    n_active = jnp.minimum(qi + reach, n_key_tiles - 1) - first + 1

    def key_tile(j):
        t = first + j - 1
        return jnp.where(j == 0, qi, jnp.where(t < qi, t, t + 1))

    def tile_copies(j, slot):
        start = pl.multiple_of(key_tile(j) * tk, tk)
        return (pltpu.make_async_copy(ka_hbm.at[b, h, pl.ds(start, tk), :], k_buf.at[slot], sems.at[0, slot]),
                pltpu.make_async_copy(vt_hbm.at[b, h, :, pl.ds(start, tk)], v_buf.at[slot], sems.at[1, slot]))

    for cp in tile_copies(0, 0):
        cp.start()

    @pl.when(n_active > 1)
    def _():
        for cp in tile_copies(1, 1):
            cp.start()

    q = q_ref[0, 0]
    low_half = lax.broadcasted_iota(jnp.int32, (tq, LANES), 1) < HEAD_DIM
    zero = jnp.zeros_like(q)
    qpos = qpos_ref[0]
    for c in range(2):
        qc = jnp.where(low_half == (c == 0), q, zero)
        for var, pos in enumerate((qpos, -qpos, jnp.zeros_like(qpos))):
            qa_scr[c, var, :, :LANES] = qc
            qa_scr[c, var, :, LANES:] = pos
    m_scr[...] = jnp.full(m_scr.shape, NEG_INF, jnp.float32)
    acc_scr[...] = jnp.zeros(acc_scr.shape, jnp.float32)


    def restabilise(slot, kt, diag):
        var = 2 if diag else jnp.where(kt < qi, 0, 1)
        off = -slope * jnp.abs(qi * tq - kt * tk).astype(jnp.float32)
        ka = k_buf[slot]
        vt = v_buf[slot]
        ss = []
        for c in range(2):
            s = _dot_nt(ka, qa_scr[c, var])
            if diag:
                s = s + dabs_ref[...] * slope
            ss.append(s)
        for c in range(2):
            s = ss[c]
            m_old = m_scr[c]
            m_new = jnp.maximum(m_old, jnp.max(s, axis=0, keepdims=True) + off)
            alpha = jnp.exp2(m_old - m_new)
            p = jnp.exp2((s - (m_new - off)).astype(jnp.bfloat16))
            acc_scr[c] = alpha * acc_scr[c] + _dot(vt, p)
            m_scr[c] = m_new

    for cp in tile_copies(0, 0):
        cp.wait()
    restabilise(0, qi, True)

    def visit(j, carry):
        slot = j % 2

        @pl.when(j + 1 < n_active)
        def _():
            for cp in tile_copies(j + 1, 1 - slot):
                cp.start()

        for cp in tile_copies(j, slot):
            cp.wait()
        kt = key_tile(j)
        var = jnp.where(kt < qi, 0, 1)
        off = -slope * jnp.abs(qi * tq - kt * tk).astype(jnp.float32)
        ka = k_buf[slot]
        vt = v_buf[slot]
        excess = None
        for c in range(2):
            s = _dot_nt(ka, qa_scr[c, var])
            m_eff = m_scr[c] - off
            over = jnp.max(jnp.max(s, axis=0, keepdims=True) - m_eff)
            excess = over if excess is None else jnp.maximum(excess, over)
            p = jnp.exp2((s - m_eff).astype(jnp.bfloat16))
            tmp_scr[c] = acc_scr[c] + _dot(vt, p)

        @pl.when(excess <= EXP2_HEADROOM)
        def _():
            acc_scr[...] = tmp_scr[...]

        @pl.when(excess > EXP2_HEADROOM)
        def _():
            restabilise(slot, kt, False)

        return carry

    lax.fori_loop(1, n_active, visit, 0)

    lam = (jnp.exp(jnp.sum(lq1_ref[...] * lk1_ref[...], axis=-1, keepdims=True))
           - jnp.exp(jnp.sum(lq2_ref[...] * lk2_ref[...], axis=-1, keepdims=True)) + LAM_INIT)
    a1, a2 = acc_scr[0], acc_scr[1]
    o = (a1[:DIFF_VDIM] / a1[DIFF_VDIM:DIFF_VDIM + 1]
         - lam * (a2[:DIFF_VDIM] / a2[DIFF_VDIM:DIFF_VDIM + 1]))
    o = o * lax.rsqrt(jnp.mean(o * o, axis=0, keepdims=True) + LN_EPS) * g_ref[...]
    o_ref[0] = (o * (1.0 - LAM_INIT)).T.astype(o_ref.dtype)


def _tile_reach(qn, kn, n_tiles):
    B = qn.shape[0]
    ng = 2 * N_HEADS_DIFF
    qmax = jnp.sqrt(qn[:, 0, :ng]) * NORM_SLACK
    kmax = jnp.sqrt(kn[:, 0, :ng]) * NORM_SLACK
    bound = jnp.max((2.0 * qmax * kmax).reshape(B, N_HEADS_DIFF, 2), axis=-1) + UNDERFLOW_LOG2
    far = bound / jnp.asarray(_diff_slopes_log2())[None, :]
    reach = jnp.floor((far - 1.0) / DIFF_TK) + 1.0
    reach = jnp.where(jnp.isfinite(reach), reach, n_tiles)
    return jnp.clip(reach, 0, n_tiles).astype(jnp.int32).reshape(-1)


def _diff_attention(reach, qd, ka, vt, qpos, dabs, lq1, lk1, lq2, lk2, g_col):
    B, H, S, _ = qd.shape
    tq, tk = DIFF_TQ, DIFF_TK
    slopes = tuple(float(s) for s in _diff_slopes_log2())
    nk = S // tk
    small = lambda a: pl.BlockSpec(a.shape, lambda b, h, qi, reach: (0, 0))
    grid_spec = pltpu.PrefetchScalarGridSpec(
        num_scalar_prefetch=1,
        grid=(B, H, S // tq),
        in_specs=[
            pl.BlockSpec((1, 1, tq, LANES), lambda b, h, qi, reach: (b, h, qi, 0)),
            pl.BlockSpec(memory_space=pl.ANY),
            pl.BlockSpec(memory_space=pl.ANY),
            pl.BlockSpec((1, tq, LANES), lambda b, h, qi, reach: (h, 0, 0)),
            small(dabs), small(lq1), small(lk1), small(lq2), small(lk2), small(g_col),
        ],
        out_specs=pl.BlockSpec((1, tq, LANES), lambda b, h, qi, reach: (b, qi, h)),
        scratch_shapes=[
            pltpu.VMEM((2, 3, tq, 2 * LANES), jnp.bfloat16),
            pltpu.VMEM((2, 1, tq), jnp.float32),
            pltpu.VMEM((2, V_ROWS, tq), jnp.float32),
            pltpu.VMEM((2, V_ROWS, tq), jnp.float32),
            pltpu.VMEM((2, tk, 2 * LANES), jnp.bfloat16),
            pltpu.VMEM((2, V_ROWS, tk), jnp.bfloat16),
            pltpu.SemaphoreType.DMA((2, 2)),
        ],
    )
    return pl.pallas_call(
        functools.partial(_diff_kernel, slopes=slopes, n_key_tiles=nk),
        grid_spec=grid_spec,
        out_shape=jax.ShapeDtypeStruct((B, S, H * LANES), jnp.bfloat16),
        compiler_params=_cparams(("parallel", "parallel", "arbitrary")),
        name="diff_attention",
    )(reach, qd, ka, vt, qpos, dabs, lq1, lk1, lq2, lk2, g_col)


def _diff_slopes_log2():
    return np.asarray([2.0 ** (-8.0 * (i + 1) / N_HEADS_DIFF) for i in range(N_HEADS_DIFF)],
                      np.float32) * np.float32(LOG2E)


def _position_columns():
    def bf16_round(v):
        u = np.asarray(v, np.float32).view(np.uint32)
        return ((u + 0x7FFF + ((u >> 16) & 1)) & np.uint32(0xFFFF0000)).view(np.float32)

    kpos = np.zeros((N_HEADS_DIFF, DIFF_TK, LANES), np.float32)
    qpos = np.zeros((N_HEADS_DIFF, DIFF_TQ, LANES), np.float32)
    pk, pq = np.arange(DIFF_TK), np.arange(DIFF_TQ)
    for h, a in enumerate(_diff_slopes_log2()):
        rest = np.float32(a)
        for n in range(3):
            a_n = np.float32(bf16_round(rest))
            rest = np.float32(rest - a_n)
            c = 4 * n
            kpos[h, :, c + 0] = (pk // POS_SPLIT) * POS_SPLIT
            kpos[h, :, c + 1] = pk % POS_SPLIT
            kpos[h, :, c + 2] = a_n
            kpos[h, :, c + 3] = a_n
            qpos[h, :, c + 0] = a_n
            qpos[h, :, c + 1] = a_n
            qpos[h, :, c + 2] = -((pq // POS_SPLIT) * POS_SPLIT)
            qpos[h, :, c + 3] = -(pq % POS_SPLIT)
    dabs = -np.abs(pq[:, None] - pk[None, :]).astype(np.float32)
    return jnp.asarray(kpos, jnp.bfloat16), jnp.asarray(qpos, jnp.bfloat16), jnp.asarray(dabs)


def _post_attn_kernel(o1_ref, l1_ref, o4_ref, l4_ref, o16_ref, l16_ref, od_ref, x_ref, wo_ref, g_ref, b_ref,
                      wr_ref, br_ref, tri_ref,
                      x1_ref, x1t_ref, xs0_hbm, idx_ref, gate_ref, rank_ref, cnt_ref, carry_scr, order_scr,
                      zero_scr, zero_sem, *, n_fill):
    step = pl.program_id(0)

    @pl.when(step == 0)
    def _():
        carry_scr[...] = jnp.zeros(carry_scr.shape, jnp.float32)
        zero_scr[...] = jnp.zeros(zero_scr.shape, zero_scr.dtype)

    zrows = zero_scr.shape[0]

    def fill_copy(j):
        start = pl.multiple_of((step * n_fill + j) * zrows, zrows)
        return pltpu.make_async_copy(zero_scr, xs0_hbm.at[pl.ds(start, zrows)], zero_sem)

    for j in range(n_fill):
        fill_copy(j).start()

    tm = x_ref.shape[0]
    nc = DIL_WIDTH // LANES
    for n, (o_ref, l_ref) in enumerate(((o4_ref, l4_ref), (o16_ref, l16_ref))):
        d = o_ref.shape[1]
        for r in range(d):
            o_r = o_ref[0, r].astype(jnp.float32)
            l_r = l_ref[0, r]
            for c in range(nc):
                cols = slice(c * LANES, (c + 1) * LANES)
                order_scr[2 * n, c, pl.ds(r, tm // d, stride=d), :] = o_r[:, cols]
                order_scr[2 * n + 1, c, pl.ds(r, tm // d, stride=d), :] = l_r[:, cols]
    in_order = lambda n: jnp.concatenate([order_scr[n, c] for c in range(nc)], axis=1)
    outs = (o1_ref[0, 0].astype(jnp.float32), in_order(0), in_order(2))
    lses = (l1_ref[0, 0], in_order(1), in_order(3))
    top = jnp.maximum(jnp.maximum(lses[0], lses[1]), lses[2])
    wts = [jnp.exp(l - top) for l in lses]
    oa = ((wts[0] * outs[0] + wts[1] * outs[1] + wts[2] * outs[2])
          / (wts[0] + wts[1] + wts[2])).astype(jnp.bfloat16)

    mix = _dot(oa, wo_ref[:DIL_WIDTH, :]) + _dot(od_ref[...], wo_ref[DIL_WIDTH:, :])
    x1 = _layer_norm(DEEPNORM_ALPHA * x_ref[...] + mix, g_ref[...], b_ref[...])
    x1_ref[...] = x1
    for s in range(ROW_SUBLANES):
        x1t_ref[pl.ds(s, x1.shape[0], stride=ROW_SUBLANES), :] = x1[:, s * LANES:(s + 1) * LANES]

    tm = x1.shape[0]
    lg = _dot_nt(wr_ref[...], x1, precision=lax.Precision.HIGHEST) + br_ref[...]
    eidx = lax.broadcasted_iota(jnp.int32, (N_EXPERTS, tm), 0)
    vals, sels = [], []
    for k in range(TOP_K):
        mx = jnp.max(lg, axis=0, keepdims=True)
        idx = jnp.min(jnp.where(lg == mx, eidx, N_EXPERTS), axis=0, keepdims=True)
        sel = eidx == idx
        vals.append(mx)
        sels.append(sel)
        idx_ref[k:k + 1, :] = idx
        lg = jnp.where(sel, -jnp.inf, lg)
    ex = [jnp.exp(v - vals[0]) for v in vals]
    den = ex[0] + ex[1] + ex[2] + ex[3]
    for k in range(TOP_K):
        gate_ref[k:k + 1, :] = ex[k] / den

    chosen = (sels[0] | sels[1] | sels[2] | sels[3])
    onehot = jnp.where(chosen, 1.0, 0.0)
    before = _dot(onehot.astype(jnp.bfloat16), tri_ref[...]) + carry_scr[...]
    for k in range(TOP_K):
        rank_ref[k:k + 1, :] = jnp.sum(jnp.where(sels[k], before, 0.0), axis=0,
                                       keepdims=True).astype(jnp.int32)
    carry_scr[...] = carry_scr[...] + jnp.sum(onehot, axis=1, keepdims=True)
    pad = jnp.zeros((8 - TOP_K, tm), jnp.int32)
    idx_ref[TOP_K:, :] = pad
    rank_ref[TOP_K:, :] = pad
    gate_ref[TOP_K:, :] = pad.astype(jnp.float32)
    cnt_ref[...] = jnp.broadcast_to(carry_scr[...], cnt_ref.shape).astype(jnp.int32)
    for j in range(n_fill):
        fill_copy(j).wait()


def _post_attn(dil, od, x, wo_bf, g, b, wr_t, br, tri, n_rows):
    T, D = x.shape
    assert D == ROW_SUBLANES * LANES
    tm = ROW_TILE
    nb = dil[0][0].shape[2] // tm
    row = lambda w: pl.BlockSpec((tm, w), lambda i: (i, 0))
    full = lambda a: pl.BlockSpec(a.shape, lambda i: (0, 0))
    col = pl.BlockSpec((8, tm), lambda i: (0, i))
    dil_specs, dil_args = [], []
    for o_d, lse_d in dil:
        d = o_d.shape[1]
        spec = pl.BlockSpec((1, d, tm // d, DIL_WIDTH), lambda i: (i // nb, 0, i % nb, 0))
        dil_specs += [spec, spec]
        dil_args += [o_d, lse_d]
    steps = T // tm
    fill_rows, rem = divmod(n_rows * ROW_SUBLANES, steps * ZERO_FILL_COPIES)
    assert rem == 0 and fill_rows % ROW_SUBLANES == 0
    return pl.pallas_call(
        functools.partial(_post_attn_kernel, n_fill=ZERO_FILL_COPIES),
        grid=(steps,),
        in_specs=dil_specs + [row(DIFF_WIDTH), row(D), full(wo_bf), full(g), full(b), full(wr_t),
                              full(br), full(tri)],
        out_specs=[row(D), pl.BlockSpec((tm * ROW_SUBLANES, LANES), lambda i: (i, 0)),
                   pl.BlockSpec(memory_space=pl.ANY), col, col, col,
                   pl.BlockSpec((N_EXPERTS, LANES), lambda i: (0, 0))],
        out_shape=[
            jax.ShapeDtypeStruct((T, D), jnp.float32),
            jax.ShapeDtypeStruct((T * ROW_SUBLANES, LANES), jnp.float32),
            jax.ShapeDtypeStruct((n_rows * ROW_SUBLANES, LANES), jnp.float32),
            jax.ShapeDtypeStruct((8, T), jnp.int32),
            jax.ShapeDtypeStruct((8, T), jnp.float32),
            jax.ShapeDtypeStruct((8, T), jnp.int32),
            jax.ShapeDtypeStruct((N_EXPERTS, LANES), jnp.int32),
        ],
        scratch_shapes=[pltpu.VMEM((N_EXPERTS, 1), jnp.float32),
                        pltpu.VMEM((4, DIL_WIDTH // LANES, tm, LANES), jnp.float32),
                        pltpu.VMEM((fill_rows, LANES), jnp.float32), pltpu.SemaphoreType.DMA(())],
        compiler_params=_cparams(("arbitrary",)),
        name="post_attn_router",
    )(*dil_args, od, x, wo_bf, g, b, wr_t, br, tri)


def _index_copy(dest_hbm, step, idx_smem, sem):
    per_step = dest_hbm.shape[1]
    half = pl.multiple_of((step % 2) * per_step, per_step)
    return pltpu.make_async_copy(dest_hbm.at[step], idx_smem.at[pl.ds(half, per_step)], sem)


def _stage_index(dest_hbm, step, n_steps, idx_smem, sem):
    @pl.when(step == 0)
    def _():
        _index_copy(dest_hbm, 0, idx_smem, sem).start()

    _index_copy(dest_hbm, step, idx_smem, sem).wait()

    @pl.when(step + 1 < n_steps)
    def _():
        _index_copy(dest_hbm, step + 1, idx_smem, sem).start()

    return (step % 2) * dest_hbm.shape[1]


def _row_tile(ref, row):
    return ref.at[pl.ds(pl.multiple_of(row * ROW_SUBLANES, ROW_SUBLANES), ROW_SUBLANES)]


def _rows_from_tiles(ref, n):
    return jnp.concatenate([ref[pl.ds(s, n, stride=ROW_SUBLANES), :] for s in range(ROW_SUBLANES)], axis=1)


def _rows_to_tiles(ref, rows):
    for s in range(ROW_SUBLANES):
        ref[pl.ds(s, rows.shape[0], stride=ROW_SUBLANES), :] = rows[:, s * LANES:(s + 1) * LANES]


def _dispatch_kernel(dest_hbm, x_ref, xs_in_hbm, xs_hbm, idx_smem, idx_sem, row_sem):
    del xs_in_hbm
    tm = x_ref.shape[0] // ROW_SUBLANES
    base = _stage_index(dest_hbm, pl.program_id(0), pl.num_programs(0), idx_smem, idx_sem)

    def issue(t, carry):
        for k in range(TOP_K):
            pltpu.make_async_copy(_row_tile(x_ref, t), _row_tile(xs_hbm, idx_smem[base + k * tm + t]),
                                  row_sem).start(priority=k % 2)
        return carry

    lax.fori_loop(0, tm, issue, 0, unroll=8)
    for k in range(TOP_K):
        pltpu.make_async_copy(x_ref, xs_hbm.at[pl.ds(0, tm * ROW_SUBLANES)], row_sem).wait()


def _dispatch(dest_steps, x1t, xs0):
    n_steps, per_step = dest_steps.shape
    tm = per_step // TOP_K
    return pl.pallas_call(
        _dispatch_kernel,
        grid=(n_steps,),
        in_specs=[
            pl.BlockSpec(memory_space=pl.ANY),
            pl.BlockSpec((tm * ROW_SUBLANES, LANES), lambda i: (i, 0)),
            pl.BlockSpec(memory_space=pl.ANY),
        ],
        out_specs=pl.BlockSpec(memory_space=pl.ANY),
        out_shape=jax.ShapeDtypeStruct(xs0.shape, xs0.dtype),
        scratch_shapes=[pltpu.SMEM((2 * per_step,), jnp.int32), pltpu.SemaphoreType.DMA(()),
                        pltpu.SemaphoreType.DMA(())],
        input_output_aliases={2: 0},
        compiler_params=_cparams(("arbitrary",)),
        name="dispatch_rows",
    )(dest_steps, x1t, xs0)


def _expert_kernel(tile_e_ref, n_used_ref, xs_ref, wu_ref, bu_ref, wd_ref, bd_ref, y_ref, wu_scr, wd_scr):
    i = pl.program_id(0)
    used = i < n_used_ref[0]
    new_expert = (i == 0) | (tile_e_ref[i] != tile_e_ref[jnp.maximum(i - 1, 0)])
    tm = xs_ref.shape[0] // ROW_SUBLANES

    @pl.when(used & new_expert)
    def _():
        wu_scr[...] = wu_ref[0].astype(jnp.bfloat16)
        wd_scr[...] = wd_ref[0].astype(jnp.bfloat16)

    @pl.when(used)
    def _():
        xs = _rows_from_tiles(xs_ref, tm).astype(jnp.bfloat16)
        hu = _dot(xs, wu_scr[...]) + bu_ref[0]
        g = jnp.minimum(hu[:, :D_FF], SWIGLU_LIMIT)
        u = jnp.clip(hu[:, D_FF:], -SWIGLU_LIMIT, SWIGLU_LIMIT)
        act = g * (1.0 / (1.0 + jnp.exp(-SWIGLU_ALPHA * g))) * (u + 1.0)
        _rows_to_tiles(y_ref, _dot(act.astype(jnp.bfloat16), wd_scr[...]) + bd_ref[0])

    @pl.when(jnp.logical_not(used))
    def _():
        y_ref[...] = jnp.zeros(y_ref.shape, y_ref.dtype)


def _experts(tile_e, n_used, xs, wu, bu, wd, bd):
    D = ROW_SUBLANES * LANES
    n_rows = xs.shape[0] // ROW_SUBLANES
    tm = MOE_TILE
    tile_spec = pl.BlockSpec((tm * ROW_SUBLANES, LANES), lambda i, te, nu: (i, 0))
    grid_spec = pltpu.PrefetchScalarGridSpec(
        num_scalar_prefetch=2,
        grid=(n_rows // tm,),
        in_specs=[
            tile_spec,
            pl.BlockSpec((1, D, 2 * D_FF), lambda i, te, nu: (te[i], 0, 0)),
            pl.BlockSpec((1, 1, 2 * D_FF), lambda i, te, nu: (te[i], 0, 0)),
            pl.BlockSpec((1, D_FF, D), lambda i, te, nu: (te[i], 0, 0)),
            pl.BlockSpec((1, 1, D), lambda i, te, nu: (te[i], 0, 0)),
        ],
        out_specs=tile_spec,
        scratch_shapes=[pltpu.VMEM((D, 2 * D_FF), jnp.bfloat16), pltpu.VMEM((D_FF, D), jnp.bfloat16)],
    )
    return pl.pallas_call(
        _expert_kernel,
        grid_spec=grid_spec,
        out_shape=jax.ShapeDtypeStruct(xs.shape, jnp.float32),
        compiler_params=_cparams(("arbitrary",)),
        name="experts",
    )(tile_e, n_used, xs, wu, bu, wd, bd)


def _combine_kernel(dest_hbm, y_hbm, x1_ref, gate_ref, g_ref, b_ref, o_ref, buf, idx_smem, idx_sem, row_sems):
    i = pl.program_id(0)
    n = pl.num_programs(0)
    tm = x1_ref.shape[0]

    def issue(step, slot):
        base = _stage_index(dest_hbm, step, n, idx_smem, idx_sem)

        def body(t, carry):
            for k in range(TOP_K):
                pltpu.make_async_copy(_row_tile(y_hbm, idx_smem[base + k * tm + t]),
                                      _row_tile(buf.at[slot, k], t), row_sems.at[slot]).start(priority=k % 2)
            return carry

        lax.fori_loop(0, tm, body, 0, unroll=8)

    @pl.when(i == 0)
    def _():
        issue(i, 0)

    @pl.when(i + 1 < n)
    def _():
        issue(i + 1, (i + 1) % 2)

    slot = i % 2
    for k in range(TOP_K):
        pltpu.make_async_copy(y_hbm.at[pl.ds(0, tm * ROW_SUBLANES)], buf.at[slot, k], row_sems.at[slot]).wait()
    ffn = _rows_from_tiles(buf.at[slot, 0], tm) * gate_ref[:, 0:1]
    for k in range(1, TOP_K):
        ffn = ffn + _rows_from_tiles(buf.at[slot, k], tm) * gate_ref[:, k:k + 1]
    o_ref[...] = _layer_norm(DEEPNORM_ALPHA * x1_ref[...] + ffn, g_ref[...], b_ref[...])


def _combine(dest_steps, y, x1, gates_t, g, b):
    T, D = x1.shape
    n_steps, per_step = dest_steps.shape
    tm = per_step // TOP_K
    full = lambda a: pl.BlockSpec(a.shape, lambda i: (0, 0))
    return pl.pallas_call(
        _combine_kernel,
        grid=(n_steps,),
        in_specs=[
            pl.BlockSpec(memory_space=pl.ANY),
            pl.BlockSpec(memory_space=pl.ANY),
            pl.BlockSpec((tm, D), lambda i: (i, 0)),
            pl.BlockSpec((tm, 8), lambda i: (i, 0)),
            full(g), full(b),
        ],
        out_specs=pl.BlockSpec((tm, D), lambda i: (i, 0)),
        out_shape=jax.ShapeDtypeStruct((T, D), jnp.float32),
        scratch_shapes=[pltpu.VMEM((2, TOP_K, tm * ROW_SUBLANES, LANES), jnp.float32),
                        pltpu.SMEM((2 * per_step,), jnp.int32),
                        pltpu.SemaphoreType.DMA(()), pltpu.SemaphoreType.DMA((2,))],
        compiler_params=_cparams(("arbitrary",)),
        name="combine_ln2",
    )(dest_steps, y, x1, gates_t, g, b)


def kernel(x, w_in, w_out, lambda_q1, lambda_k1, lambda_q2, lambda_k2, diff_norm_g, ln1_g, ln1_b,
           w_router, b_router, w_up, b_up, w_down, b_down, ln2_g, ln2_b):
    B, S, D = x.shape
    T = B * S
    assert w_in.shape[0] == 1, "single layer"
    assert S % DIFF_TQ == 0 and DIFF_TK == DIFF_TQ and DIFF_TK % ROW_TILE == 0 and S % (16 * DIL_TQ) == 0

    na = 3 * DIL_WIDTH
    colscale = np.ones((3 * D,), np.float32)
    colscale[:DIL_WIDTH] = HEAD_DIM ** -0.5
    colscale[na:na + DIFF_WIDTH] = HEAD_DIM ** -0.5 * LOG2E
    w_in_bf = (w_in[0] * colscale).astype(jnp.bfloat16)
    nv = na + 2 * DIFF_WIDTH

    kpos, qpos, dabs = _position_columns()
    pa, pa4, pa16, qd, ka, vt, qn, kn = _in_proj(x, w_in_bf[:, :nv], w_in_bf[:, nv:].T, kpos)
    dil = [_dilated_stage(p, d) for p, (_, d) in zip((pa[:, None], pa4, pa16), DIL_PATTERNS)]

    r2 = lambda a: a.reshape(1, -1).astype(jnp.float32)
    od = _diff_attention(_tile_reach(qn, kn, S // DIFF_TK), qd, ka, vt, qpos, dabs, r2(lambda_q1[0]), r2(lambda_k1[0]), r2(lambda_q2[0]),
                         r2(lambda_k2[0]), diff_norm_g[0].reshape(-1, 1).astype(jnp.float32))

    tri = jnp.asarray(np.triu(np.ones((ROW_TILE, ROW_TILE), np.float32), k=1), jnp.bfloat16)
    n_rows = T * TOP_K + N_EXPERTS * MOE_TILE
    n_tiles = n_rows // MOE_TILE
    x1, x1t, xs0, top_idx, gates, rank, counts = _post_attn(
        dil, od.reshape(T, DIFF_WIDTH), x.reshape(T, D), w_out[0].astype(jnp.bfloat16),
        r2(ln1_g[0]), r2(ln1_b[0]), w_router[0].T, b_router[0].reshape(N_EXPERTS, 1), tri, n_rows)

    counts = counts[:, 0]
    pcounts = ((counts + MOE_TILE - 1) // MOE_TILE) * MOE_TILE
    pends = jnp.cumsum(pcounts)
    pstarts = pends - pcounts
    dest = rank[:TOP_K]
    for e in range(N_EXPERTS):
        dest = dest + jnp.where(top_idx[:TOP_K] == e, pstarts[e], 0)
    tile_starts = jnp.arange(n_tiles, dtype=jnp.int32) * MOE_TILE
    tile_e = jnp.minimum(jnp.sum(pends[None, :] <= tile_starts[:, None], axis=1), N_EXPERTS - 1).astype(jnp.int32)
    n_used = (pends[-1] // MOE_TILE).astype(jnp.int32).reshape(1)

    def per_step(tm):
        return dest.reshape(TOP_K, T // tm, tm).transpose(1, 0, 2).reshape(T // tm, TOP_K * tm)

    xs = _dispatch(per_step(ROW_TILE), x1t, xs0)
    y = _experts(tile_e, n_used, xs, w_up[0], b_up[0][:, None, :], w_down[0], b_down[0][:, None, :])
    out = _combine(per_step(COMBINE_TILE), y, x1, gates.T, r2(ln2_g[0]), r2(ln2_b[0]))
    return out.reshape(B, S, D)
```

```python
import functools
import math

import jax
import jax.numpy as jnp
import numpy as np
from jax import lax
from jax.experimental import pallas as pl
from jax.experimental.pallas import tpu as pltpu

D_MODEL = 1024
HEAD_DIM = 64
DIL_WIDTH = 512
N_HEADS_DIL = 8
DIL_PATTERNS = ((128, 1), (512, 4), (2048, 16))
DIL_SIDE = 64
DIFF_WIDTH = 512
N_HEADS_DIFF = 4
DIFF_VDIM = 2 * HEAD_DIM
N_EXPERTS = 32
TOP_K = 4
D_FF = D_MODEL
SWIGLU_ALPHA = 1.702
SWIGLU_LIMIT = 7.0
DEEPNORM_ALPHA = 2.0 ** 0.25
LN_EPS = 1e-5
NEG_INF = -1e30
LAM_INIT = 0.8 - 0.6 * math.exp(-0.3 * 0)
LOG2E = math.log2(math.e)

LANES = 128
VMEM_LIMIT = 56 * 1024 * 1024

ROW_TILE = 512
DIFF_TQ = 1024
DIFF_TK = 1024
V_ROWS = DIFF_VDIM + 16
EXP2_HEADROOM = 4.0
UNDERFLOW_LOG2 = 160.0
NORM_SLACK = 1.01
DIL_TQ = 128
DIL_QB = 2
MOE_TILE = 512
COMBINE_TILE = 256
ROW_SUBLANES = 8
ZERO_FILL_COPIES = 8
POS_SPLIT = 16


def _cparams(sem):
    return pltpu.CompilerParams(dimension_semantics=sem, vmem_limit_bytes=VMEM_LIMIT)


def _dot_nt(a, b, **kw):
    return lax.dot_general(a, b, (((1,), (1,)), ((), ())), preferred_element_type=jnp.float32, **kw)


def _dot(a, b, **kw):
    return jnp.dot(a, b, preferred_element_type=jnp.float32, **kw)


def _layer_norm(y, g, b):
    mu = jnp.mean(y, axis=-1, keepdims=True)
    yc = y - mu
    var = jnp.mean(yc * yc, axis=-1, keepdims=True)
    return yc * lax.rsqrt(var + LN_EPS) * g + b


def _in_proj_kernel(x_ref, w_ref, wvt_ref, kpos_ref, grp_ref, pa_ref, pa4_ref, pa16_ref, qd_ref, ka_ref,
                    vt_ref, qn_ref, kn_ref, pa_scr):
    xb = x_ref[0].astype(jnp.bfloat16)
    na = 3 * DIL_WIDTH
    tm = xb.shape[0]
    paf = _dot(xb, w_ref[:, :na])
    pa_ref[0] = paf.astype(jnp.bfloat16)
    for c in range(na // LANES):
        cols = slice(c * LANES, (c + 1) * LANES)
        pa_scr[c] = paf[:, cols]
        for d, ref in ((DIL_PATTERNS[1][1], pa4_ref), (DIL_PATTERNS[2][1], pa16_ref)):
            for r in range(d):
                ref[0, r, :, cols] = pa_scr[c, pl.ds(r, tm // d, stride=d), :].astype(jnp.bfloat16)
    q = _dot(xb, w_ref[:, na:na + DIFF_WIDTH]).astype(jnp.bfloat16)
    k = _dot(xb, w_ref[:, na + DIFF_WIDTH:]).astype(jnp.bfloat16)
    vt = _dot_nt(wvt_ref[...], xb).astype(jnp.bfloat16)
    ones = jnp.ones((V_ROWS - DIFF_VDIM, tm), jnp.bfloat16)

    @pl.when(pl.program_id(1) == 0)
    def _():
        qn_ref[...] = jnp.zeros(qn_ref.shape, jnp.float32)
        kn_ref[...] = jnp.zeros(kn_ref.shape, jnp.float32)

    for src, dst in ((q, qn_ref), (k, kn_ref)):
        f = src.astype(jnp.float32)
        sq = f * f
        sq_hi = sq.astype(jnp.bfloat16)
        sq_lo = (sq - sq_hi.astype(jnp.float32)).astype(jnp.bfloat16)
        gs = _dot(sq_hi, grp_ref[...]) + _dot(sq_lo, grp_ref[...])
        dst[0] = jnp.maximum(dst[0], jnp.max(gs, axis=0, keepdims=True))
    for h in range(N_HEADS_DIFF):
        sl = slice(h * LANES, (h + 1) * LANES)
        qd_ref[0, h] = q[:, sl]
        ka_ref[0, h, :, :LANES] = k[:, sl]
        ka_ref[0, h, :, LANES:] = kpos_ref[h]
        vt_ref[0, h, :DIFF_VDIM, :] = vt[sl, :]
        vt_ref[0, h, DIFF_VDIM:, :] = ones


def _in_proj(x, w_bf, wvt_bf, kpos):
    B, S, D = x.shape
    tm = ROW_TILE
    na = 3 * DIL_WIDTH
    grp = np.zeros((DIFF_WIDTH, LANES), np.float32)
    grp[np.arange(DIFF_WIDTH), np.arange(DIFF_WIDTH) // HEAD_DIM] = 1.0
    grp = jnp.asarray(grp, jnp.bfloat16)
    norm_spec = pl.BlockSpec((1, 1, LANES), lambda b, i: (b, 0, 0))
    d4, d16 = DIL_PATTERNS[1][1], DIL_PATTERNS[2][1]
    return pl.pallas_call(
        _in_proj_kernel,
        grid=(B, S // tm),
        in_specs=[
            pl.BlockSpec((1, tm, D), lambda b, i: (b, i, 0)),
            pl.BlockSpec(w_bf.shape, lambda b, i: (0, 0)),
            pl.BlockSpec(wvt_bf.shape, lambda b, i: (0, 0)),
            pl.BlockSpec((N_HEADS_DIFF, tm, LANES), lambda b, i: (0, i % (DIFF_TK // tm), 0)),
            pl.BlockSpec(grp.shape, lambda b, i: (0, 0)),
        ],
        out_specs=[
            pl.BlockSpec((1, tm, na), lambda b, i: (b, i, 0)),
            pl.BlockSpec((1, d4, tm // d4, na), lambda b, i: (b, 0, i, 0)),
            pl.BlockSpec((1, d16, tm // d16, na), lambda b, i: (b, 0, i, 0)),
            pl.BlockSpec((1, N_HEADS_DIFF, tm, LANES), lambda b, i: (b, 0, i, 0)),
            pl.BlockSpec((1, N_HEADS_DIFF, tm, 2 * LANES), lambda b, i: (b, 0, i, 0)),
            pl.BlockSpec((1, N_HEADS_DIFF, V_ROWS, tm), lambda b, i: (b, 0, 0, i)),
            norm_spec, norm_spec,
        ],
        out_shape=[
            jax.ShapeDtypeStruct((B, S, na), jnp.bfloat16),
            jax.ShapeDtypeStruct((B, d4, S // d4, na), jnp.bfloat16),
            jax.ShapeDtypeStruct((B, d16, S // d16, na), jnp.bfloat16),
            jax.ShapeDtypeStruct((B, N_HEADS_DIFF, S, LANES), jnp.bfloat16),
            jax.ShapeDtypeStruct((B, N_HEADS_DIFF, S, 2 * LANES), jnp.bfloat16),
            jax.ShapeDtypeStruct((B, N_HEADS_DIFF, V_ROWS, S), jnp.bfloat16),
            jax.ShapeDtypeStruct((B, 1, LANES), jnp.float32),
            jax.ShapeDtypeStruct((B, 1, LANES), jnp.float32),
        ],
        scratch_shapes=[pltpu.VMEM((na // LANES, tm, LANES), jnp.float32)],
        compiler_params=_cparams(("parallel", "arbitrary")),
        name="in_proj",
    )(x, w_bf, wvt_bf, kpos, grp)


def _dilated_kernel(*refs, seq_len):
    n_halo = 2 * DIL_QB + 2
    q_ref, k_refs, v_refs = refs[0], refs[1:1 + n_halo], refs[1 + n_halo:1 + 2 * n_halo]
    bias_ref, o_ref, lse_ref = refs[1 + 2 * n_halo:]
    tq = DIL_TQ
    nk = tq + 2 * DIL_SIDE
    i = pl.program_id(2)
    q = q_ref[0, 0]
    kwin = jnp.concatenate([r[0, 0] for r in k_refs], axis=0)
    vwin = jnp.concatenate([r[0, 0] for r in v_refs], axis=0)
    low_half = lax.broadcasted_iota(jnp.int32, (tq, LANES), 1) < HEAD_DIM
    ss = []
    for u in range(DIL_QB):
        key_pos = (i * DIL_QB + u) * tq - DIL_SIDE + lax.broadcasted_iota(jnp.int32, (1, nk), 1)
        edge = jnp.where((key_pos >= 0) & (key_pos < seq_len), 0.0, NEG_INF)
        for h in range(N_HEADS_DIL):
            sl = slice((h // 2) * LANES, (h // 2 + 1) * LANES)
            qu = q[u * tq:(u + 1) * tq, sl]
            qm = jnp.where(low_half == (h % 2 == 0), qu, jnp.zeros((tq, LANES), q.dtype))
            ss.append(_dot_nt(qm, kwin[u * tq:u * tq + nk, sl]) + bias_ref[h] + edge)
    s = jnp.concatenate(ss, axis=0)
    m = jnp.max(s, axis=-1, keepdims=True)
    p = jnp.exp(s - m)
    l = jnp.sum(p, axis=-1, keepdims=True)
    lse = m + jnp.log(l)
    inv_l = 1.0 / l
    pb = p.astype(jnp.bfloat16)
    for u in range(DIL_QB):
        for pair in range(N_HEADS_DIL // 2):
            sl = slice(pair * LANES, (pair + 1) * LANES)
            base = (u * N_HEADS_DIL + 2 * pair) * tq
            r0 = slice(base, base + tq)
            r1 = slice(base + tq, base + 2 * tq)
            vu = vwin[u * tq:u * tq + nk, sl]
            a0 = _dot(pb[r0], vu) * inv_l[r0]
            a1 = _dot(pb[r1], vu) * inv_l[r1]
            rows = slice(u * tq, (u + 1) * tq)
            o_ref[0, 0, rows, sl] = jnp.where(low_half, a0, a1).astype(o_ref.dtype)
            lse_ref[0, 0, rows, sl] = jnp.where(low_half, lse[r0], lse[r1])


def _dilated_stage(pa_d, dilation):
    B, d, L, _ = pa_d.shape
    tb = DIL_QB * DIL_TQ
    w = DIL_WIDTH
    halo = DIL_SIDE
    n_halo = L // halo
    per_step = tb // halo + 2
    bias = _dilated_bias(dilation)
    assert L % tb == 0

    def halo_spec(col, j):
        def imap(b, r, i):
            return (b, r, jnp.clip(i * (tb // halo) - 1 + j, 0, n_halo - 1), col)
        return pl.BlockSpec((1, 1, halo, w), imap)

    out_spec = pl.BlockSpec((1, 1, tb, w), lambda b, r, i: (b, r, i, 0))
    in_specs = ([pl.BlockSpec((1, 1, tb, w), lambda b, r, i: (b, r, i, 0))]
                + [halo_spec(1, j) for j in range(per_step)] + [halo_spec(2, j) for j in range(per_step)]
                + [pl.BlockSpec(bias.shape, lambda b, r, i: (0, 0, 0))])
    return pl.pallas_call(
        functools.partial(_dilated_kernel, seq_len=L),
        grid=(B, d, L // tb),
        in_specs=in_specs,
        out_specs=[out_spec, out_spec],
        out_shape=[jax.ShapeDtypeStruct((B, d, L, w), jnp.bfloat16),
                   jax.ShapeDtypeStruct((B, d, L, w), jnp.float32)],
        compiler_params=_cparams(("parallel", "parallel", "parallel")),
        name=f"dilated_d{d}",
    )(*([pa_d] * (1 + 2 * per_step)), bias)


def _dilated_bias(dilation):
    tq = DIL_TQ
    slopes = np.asarray([2.0 ** (-8.0 * (i + 1) / N_HEADS_DIL) for i in range(N_HEADS_DIL)], np.float32)
    rel = (np.arange(tq + 2 * DIL_SIDE)[None, :] - DIL_SIDE) - np.arange(tq)[:, None]
    band = np.abs(rel) <= DIL_SIDE
    pen = -(slopes * dilation)[:, None, None] * np.abs(rel).astype(np.float32)[None]
    return jnp.asarray(np.where(band[None], pen, np.float32(NEG_INF)), jnp.float32)


def _diff_kernel(reach_ref, q_ref, ka_hbm, vt_hbm, qpos_ref, dabs_ref, lq1_ref, lk1_ref, lq2_ref, lk2_ref,
                 g_ref, o_ref, qa_scr, m_scr, acc_scr, k_buf, v_buf, sems, *, slopes, n_key_tiles):
    b = pl.program_id(0)
    h = pl.program_id(1)
    qi = pl.program_id(2)
    tq = q_ref.shape[2]
    tk = k_buf.shape[1]

    slope = jnp.float32(slopes[0])
    for hh in range(1, N_HEADS_DIFF):
        slope = jnp.where(h == hh, jnp.float32(slopes[hh]), slope)

    reach = reach_ref[b * N_HEADS_DIFF + h]
    first = jnp.maximum(qi - reach, 0)
    n_active = jnp.minimum(qi + reach, n_key_tiles - 1) - first + 1

    def key_tile(j):
        t = first + j - 1
        return jnp.where(j == 0, qi, jnp.where(t < qi, t, t + 1))

    def tile_copies(j, slot):
        start = pl.multiple_of(key_tile(j) * tk, tk)
        return (pltpu.make_async_copy(ka_hbm.at[b, h, pl.ds(start, tk), :], k_buf.at[slot], sems.at[0, slot]),
                pltpu.make_async_copy(vt_hbm.at[b, h, :, pl.ds(start, tk)], v_buf.at[slot], sems.at[1, slot]))

    for cp in tile_copies(0, 0):
        cp.start()

    @pl.when(n_active > 1)
    def _():
        for cp in tile_copies(1, 1):
            cp.start()

    q = q_ref[0, 0]
    low_half = lax.broadcasted_iota(jnp.int32, (tq, LANES), 1) < HEAD_DIM
    zero = jnp.zeros_like(q)
    qpos = qpos_ref[0]
    for c in range(2):
        qc = jnp.where(low_half == (c == 0), q, zero)
        for var, pos in enumerate((qpos, -qpos, jnp.zeros_like(qpos))):
            qa_scr[c, var, :, :LANES] = qc
            qa_scr[c, var, :, LANES:] = pos
    m_scr[...] = jnp.full(m_scr.shape, NEG_INF, jnp.float32)
    acc_scr[0] = jnp.zeros(acc_scr.shape[1:], jnp.float32)


    def restabilise(slot, kt, diag, cur):
        var = 2 if diag else jnp.where(kt < qi, 0, 1)
        off = -slope * jnp.abs(qi * tq - kt * tk).astype(jnp.float32)
        ka = k_buf[slot]
        vt = v_buf[slot]
        ss = []
        for c in range(2):
            s = _dot_nt(ka, qa_scr[c, var])
            if diag:
                s = s + dabs_ref[...] * slope
            ss.append(s)
        for c in range(2):
            s = ss[c]
            m_old = m_scr[c]
            m_new = jnp.maximum(m_old, jnp.max(s, axis=0, keepdims=True) + off)
            alpha = jnp.exp2(m_old - m_new)
            p = jnp.exp2((s - (m_new - off)).astype(jnp.bfloat16))
            acc_scr[cur, c] = alpha * acc_scr[cur, c] + _dot(vt, p)
            m_scr[c] = m_new

    for cp in tile_copies(0, 0):
        cp.wait()
    restabilise(0, qi, True, 0)

    def visit(j, cur):
        slot = j % 2

        @pl.when(j + 1 < n_active)
        def _():
            for cp in tile_copies(j + 1, 1 - slot):
                cp.start()

        for cp in tile_copies(j, slot):
            cp.wait()
        kt = key_tile(j)
        var = jnp.where(kt < qi, 0, 1)
        off = -slope * jnp.abs(qi * tq - kt * tk).astype(jnp.float32)
        ka = k_buf[slot]
        vt = v_buf[slot]
        excess = None
        for c in range(2):
            s = _dot_nt(ka, qa_scr[c, var])
            m_eff = m_scr[c] - off
            over = jnp.max(jnp.max(s, axis=0, keepdims=True) - m_eff)
            excess = over if excess is None else jnp.maximum(excess, over)
            p = jnp.exp2((s - m_eff).astype(jnp.bfloat16))
            acc_scr[1 - cur, c] = acc_scr[cur, c] + _dot(vt, p)

        @pl.when(excess > EXP2_HEADROOM)
        def _():
            restabilise(slot, kt, False, cur)

        return jnp.where(excess <= EXP2_HEADROOM, 1 - cur, cur)

    cur = lax.fori_loop(1, n_active, visit, jnp.int32(0))

    lam = (jnp.exp(jnp.sum(lq1_ref[...] * lk1_ref[...], axis=-1, keepdims=True))
           - jnp.exp(jnp.sum(lq2_ref[...] * lk2_ref[...], axis=-1, keepdims=True)) + LAM_INIT)
    a1, a2 = acc_scr[cur, 0], acc_scr[cur, 1]
    o = (a1[:DIFF_VDIM] / a1[DIFF_VDIM:DIFF_VDIM + 1]
         - lam * (a2[:DIFF_VDIM] / a2[DIFF_VDIM:DIFF_VDIM + 1]))
    o = o * lax.rsqrt(jnp.mean(o * o, axis=0, keepdims=True) + LN_EPS) * g_ref[...]
    o_ref[0] = (o * (1.0 - LAM_INIT)).T.astype(o_ref.dtype)


def _tile_reach(qn, kn, n_tiles):
    B = qn.shape[0]
    ng = 2 * N_HEADS_DIFF
    qmax = jnp.sqrt(qn[:, 0, :ng]) * NORM_SLACK
    kmax = jnp.sqrt(kn[:, 0, :ng]) * NORM_SLACK
    bound = jnp.max((2.0 * qmax * kmax).reshape(B, N_HEADS_DIFF, 2), axis=-1) + UNDERFLOW_LOG2
    far = bound / jnp.asarray(_diff_slopes_log2())[None, :]
    reach = jnp.floor((far - 1.0) / DIFF_TK) + 1.0
    reach = jnp.where(jnp.isfinite(reach), reach, n_tiles)
    return jnp.clip(reach, 0, n_tiles).astype(jnp.int32).reshape(-1)


def _diff_attention(reach, qd, ka, vt, qpos, dabs, lq1, lk1, lq2, lk2, g_col):
    B, H, S, _ = qd.shape
    tq, tk = DIFF_TQ, DIFF_TK
    slopes = tuple(float(s) for s in _diff_slopes_log2())
    nk = S // tk
    small = lambda a: pl.BlockSpec(a.shape, lambda b, h, qi, reach: (0, 0))
    grid_spec = pltpu.PrefetchScalarGridSpec(
        num_scalar_prefetch=1,
        grid=(B, H, S // tq),
        in_specs=[
            pl.BlockSpec((1, 1, tq, LANES), lambda b, h, qi, reach: (b, h, qi, 0)),
            pl.BlockSpec(memory_space=pl.ANY),
            pl.BlockSpec(memory_space=pl.ANY),
            pl.BlockSpec((1, tq, LANES), lambda b, h, qi, reach: (h, 0, 0)),
            small(dabs), small(lq1), small(lk1), small(lq2), small(lk2), small(g_col),
        ],
        out_specs=pl.BlockSpec((1, tq, LANES), lambda b, h, qi, reach: (b, qi, h)),
        scratch_shapes=[
            pltpu.VMEM((2, 3, tq, 2 * LANES), jnp.bfloat16),
            pltpu.VMEM((2, 1, tq), jnp.float32),
            pltpu.VMEM((2, 2, V_ROWS, tq), jnp.float32),
            pltpu.VMEM((2, tk, 2 * LANES), jnp.bfloat16),
            pltpu.VMEM((2, V_ROWS, tk), jnp.bfloat16),
            pltpu.SemaphoreType.DMA((2, 2)),
        ],
    )
    return pl.pallas_call(
        functools.partial(_diff_kernel, slopes=slopes, n_key_tiles=nk),
        grid_spec=grid_spec,
        out_shape=jax.ShapeDtypeStruct((B, S, H * LANES), jnp.bfloat16),
        compiler_params=_cparams(("parallel", "parallel", "arbitrary")),
        name="diff_attention",
    )(reach, qd, ka, vt, qpos, dabs, lq1, lk1, lq2, lk2, g_col)


def _diff_slopes_log2():
    return np.asarray([2.0 ** (-8.0 * (i + 1) / N_HEADS_DIFF) for i in range(N_HEADS_DIFF)],
                      np.float32) * np.float32(LOG2E)


def _position_columns():
    def bf16_round(v):
        u = np.asarray(v, np.float32).view(np.uint32)
        return ((u + 0x7FFF + ((u >> 16) & 1)) & np.uint32(0xFFFF0000)).view(np.float32)

    kpos = np.zeros((N_HEADS_DIFF, DIFF_TK, LANES), np.float32)
    qpos = np.zeros((N_HEADS_DIFF, DIFF_TQ, LANES), np.float32)
    pk, pq = np.arange(DIFF_TK), np.arange(DIFF_TQ)
    for h, a in enumerate(_diff_slopes_log2()):
        rest = np.float32(a)
        for n in range(3):
            a_n = np.float32(bf16_round(rest))
            rest = np.float32(rest - a_n)
            c = 4 * n
            kpos[h, :, c + 0] = (pk // POS_SPLIT) * POS_SPLIT
            kpos[h, :, c + 1] = pk % POS_SPLIT
            kpos[h, :, c + 2] = a_n
            kpos[h, :, c + 3] = a_n
            qpos[h, :, c + 0] = a_n
            qpos[h, :, c + 1] = a_n
            qpos[h, :, c + 2] = -((pq // POS_SPLIT) * POS_SPLIT)
            qpos[h, :, c + 3] = -(pq % POS_SPLIT)
    dabs = -np.abs(pq[:, None] - pk[None, :]).astype(np.float32)
    return jnp.asarray(kpos, jnp.bfloat16), jnp.asarray(qpos, jnp.bfloat16), jnp.asarray(dabs)


def _post_attn_kernel(o1_ref, l1_ref, o4_ref, l4_ref, o16_ref, l16_ref, od_ref, x_ref, wo_ref, g_ref, b_ref,
                      wr_ref, br_ref, tri_ref,
                      x1_ref, x1t_ref, xs0_hbm, idx_ref, gate_ref, rank_ref, cnt_ref, carry_scr, order_scr,
                      zero_scr, zero_sem, *, n_fill):
    step = pl.program_id(0)

    @pl.when(step == 0)
    def _():
        carry_scr[...] = jnp.zeros(carry_scr.shape, jnp.float32)
        zero_scr[...] = jnp.zeros(zero_scr.shape, zero_scr.dtype)

    zrows = zero_scr.shape[0]

    def fill_copy(j):
        start = pl.multiple_of((step * n_fill + j) * zrows, zrows)
        return pltpu.make_async_copy(zero_scr, xs0_hbm.at[pl.ds(start, zrows)], zero_sem)

    for j in range(n_fill):
        fill_copy(j).start()

    tm = x_ref.shape[0]
    nc = DIL_WIDTH // LANES
    for n, (o_ref, l_ref) in enumerate(((o4_ref, l4_ref), (o16_ref, l16_ref))):
        d = o_ref.shape[1]
        for r in range(d):
            o_r = o_ref[0, r].astype(jnp.float32)
            l_r = l_ref[0, r]
            for c in range(nc):
                cols = slice(c * LANES, (c + 1) * LANES)
                order_scr[2 * n, c, pl.ds(r, tm // d, stride=d), :] = o_r[:, cols]
                order_scr[2 * n + 1, c, pl.ds(r, tm // d, stride=d), :] = l_r[:, cols]
    in_order = lambda n: jnp.concatenate([order_scr[n, c] for c in range(nc)], axis=1)
    outs = (o1_ref[0, 0].astype(jnp.float32), in_order(0), in_order(2))
    lses = (l1_ref[0, 0], in_order(1), in_order(3))
    top = jnp.maximum(jnp.maximum(lses[0], lses[1]), lses[2])
    wts = [jnp.exp(l - top) for l in lses]
    oa = ((wts[0] * outs[0] + wts[1] * outs[1] + wts[2] * outs[2])
          / (wts[0] + wts[1] + wts[2])).astype(jnp.bfloat16)

    mix = _dot(oa, wo_ref[:DIL_WIDTH, :]) + _dot(od_ref[...], wo_ref[DIL_WIDTH:, :])
    x1 = _layer_norm(DEEPNORM_ALPHA * x_ref[...] + mix, g_ref[...], b_ref[...])
    x1_ref[...] = x1
    for s in range(ROW_SUBLANES):
        x1t_ref[pl.ds(s, x1.shape[0], stride=ROW_SUBLANES), :] = x1[:, s * LANES:(s + 1) * LANES]

    tm = x1.shape[0]
    x_hi = x1.astype(jnp.bfloat16)
    x_lo = (x1 - x_hi.astype(jnp.float32)).astype(jnp.bfloat16)
    lg = (_dot_nt(wr_ref[0], x_hi) + _dot_nt(wr_ref[1], x_hi) + _dot_nt(wr_ref[0], x_lo)
          + br_ref[...])
    eidx = lax.broadcasted_iota(jnp.int32, (N_EXPERTS, tm), 0)
    vals, sels = [], []
    for k in range(TOP_K):
        mx = jnp.max(lg, axis=0, keepdims=True)
        idx = jnp.min(jnp.where(lg == mx, eidx, N_EXPERTS), axis=0, keepdims=True)
        sel = eidx == idx
        vals.append(mx)
        sels.append(sel)
        idx_ref[k:k + 1, :] = idx
        lg = jnp.where(sel, -jnp.inf, lg)
    ex = [jnp.exp(v - vals[0]) for v in vals]
    den = ex[0] + ex[1] + ex[2] + ex[3]
    for k in range(TOP_K):
        gate_ref[k:k + 1, :] = ex[k] / den

    chosen = (sels[0] | sels[1] | sels[2] | sels[3])
    onehot = jnp.where(chosen, 1.0, 0.0)
    before = _dot(onehot.astype(jnp.bfloat16), tri_ref[...]) + carry_scr[...]
    for k in range(TOP_K):
        rank_ref[k:k + 1, :] = jnp.sum(jnp.where(sels[k], before, 0.0), axis=0,
                                       keepdims=True).astype(jnp.int32)
    carry_scr[...] = carry_scr[...] + jnp.sum(onehot, axis=1, keepdims=True)
    pad = jnp.zeros((8 - TOP_K, tm), jnp.int32)
    idx_ref[TOP_K:, :] = pad
    rank_ref[TOP_K:, :] = pad
    gate_ref[TOP_K:, :] = pad.astype(jnp.float32)
    cnt_ref[...] = jnp.broadcast_to(carry_scr[...], cnt_ref.shape).astype(jnp.int32)
    for j in range(n_fill):
        fill_copy(j).wait()


def _post_attn(dil, od, x, wo_bf, g, b, wr_t, br, tri, n_rows):
    T, D = x.shape
    assert D == ROW_SUBLANES * LANES
    tm = ROW_TILE
    nb = dil[0][0].shape[2] // tm
    row = lambda w: pl.BlockSpec((tm, w), lambda i: (i, 0))
    full = lambda a: pl.BlockSpec(a.shape, lambda i: (0,) * a.ndim)
    col = pl.BlockSpec((8, tm), lambda i: (0, i))
    dil_specs, dil_args = [], []
    for o_d, lse_d in dil:
        d = o_d.shape[1]
        spec = pl.BlockSpec((1, d, tm // d, DIL_WIDTH), lambda i: (i // nb, 0, i % nb, 0))
        dil_specs += [spec, spec]
        dil_args += [o_d, lse_d]
    steps = T // tm
    fill_rows, rem = divmod(n_rows * ROW_SUBLANES, steps * ZERO_FILL_COPIES)
    assert rem == 0 and fill_rows % ROW_SUBLANES == 0
    return pl.pallas_call(
        functools.partial(_post_attn_kernel, n_fill=ZERO_FILL_COPIES),
        grid=(steps,),
        in_specs=dil_specs + [row(DIFF_WIDTH), row(D), full(wo_bf), full(g), full(b), full(wr_t),
                              full(br), full(tri)],
        out_specs=[row(D), pl.BlockSpec((tm * ROW_SUBLANES, LANES), lambda i: (i, 0)),
                   pl.BlockSpec(memory_space=pl.ANY), col, col, col,
                   pl.BlockSpec((N_EXPERTS, LANES), lambda i: (0, 0))],
        out_shape=[
            jax.ShapeDtypeStruct((T, D), jnp.float32),
            jax.ShapeDtypeStruct((T * ROW_SUBLANES, LANES), jnp.float32),
            jax.ShapeDtypeStruct((n_rows * ROW_SUBLANES, LANES), jnp.float32),
            jax.ShapeDtypeStruct((8, T), jnp.int32),
            jax.ShapeDtypeStruct((8, T), jnp.float32),
            jax.ShapeDtypeStruct((8, T), jnp.int32),
            jax.ShapeDtypeStruct((N_EXPERTS, LANES), jnp.int32),
        ],
        scratch_shapes=[pltpu.VMEM((N_EXPERTS, 1), jnp.float32),
                        pltpu.VMEM((4, DIL_WIDTH // LANES, tm, LANES), jnp.float32),
                        pltpu.VMEM((fill_rows, LANES), jnp.float32), pltpu.SemaphoreType.DMA(())],
        compiler_params=_cparams(("arbitrary",)),
        name="post_attn_router",
    )(*dil_args, od, x, wo_bf, g, b, wr_t, br, tri)


def _index_copy(dest_hbm, step, idx_smem, sem):
    per_step = dest_hbm.shape[1]
    half = pl.multiple_of((step % 2) * per_step, per_step)
    return pltpu.make_async_copy(dest_hbm.at[step], idx_smem.at[pl.ds(half, per_step)], sem)


def _stage_index(dest_hbm, step, n_steps, idx_smem, sem):
    @pl.when(step == 0)
    def _():
        _index_copy(dest_hbm, 0, idx_smem, sem).start()

    _index_copy(dest_hbm, step, idx_smem, sem).wait()

    @pl.when(step + 1 < n_steps)
    def _():
        _index_copy(dest_hbm, step + 1, idx_smem, sem).start()

    return (step % 2) * dest_hbm.shape[1]


def _row_tile(ref, row):
    return ref.at[pl.ds(pl.multiple_of(row * ROW_SUBLANES, ROW_SUBLANES), ROW_SUBLANES)]


def _rows_from_tiles(ref, n):
    return jnp.concatenate([ref[pl.ds(s, n, stride=ROW_SUBLANES), :] for s in range(ROW_SUBLANES)], axis=1)


def _rows_to_tiles(ref, rows):
    for s in range(ROW_SUBLANES):
        ref[pl.ds(s, rows.shape[0], stride=ROW_SUBLANES), :] = rows[:, s * LANES:(s + 1) * LANES]


def _dispatch_kernel(dest_hbm, x_ref, xs_in_hbm, xs_hbm, idx_smem, idx_sem, row_sem):
    del xs_in_hbm
    tm = x_ref.shape[0] // ROW_SUBLANES
    base = _stage_index(dest_hbm, pl.program_id(0), pl.num_programs(0), idx_smem, idx_sem)

    def issue(t, carry):
        for k in range(TOP_K):
            pltpu.make_async_copy(_row_tile(x_ref, t), _row_tile(xs_hbm, idx_smem[base + k * tm + t]),
                                  row_sem).start(priority=k % 2)
        return carry

    lax.fori_loop(0, tm, issue, 0, unroll=8)
    for k in range(TOP_K):
        pltpu.make_async_copy(x_ref, xs_hbm.at[pl.ds(0, tm * ROW_SUBLANES)], row_sem).wait()


def _dispatch(dest_steps, x1t, xs0):
    n_steps, per_step = dest_steps.shape
    tm = per_step // TOP_K
    return pl.pallas_call(
        _dispatch_kernel,
        grid=(n_steps,),
        in_specs=[
            pl.BlockSpec(memory_space=pl.ANY),
            pl.BlockSpec((tm * ROW_SUBLANES, LANES), lambda i: (i, 0)),
            pl.BlockSpec(memory_space=pl.ANY),
        ],
        out_specs=pl.BlockSpec(memory_space=pl.ANY),
        out_shape=jax.ShapeDtypeStruct(xs0.shape, xs0.dtype),
        scratch_shapes=[pltpu.SMEM((2 * per_step,), jnp.int32), pltpu.SemaphoreType.DMA(()),
                        pltpu.SemaphoreType.DMA(())],
        input_output_aliases={2: 0},
        compiler_params=_cparams(("arbitrary",)),
        name="dispatch_rows",
    )(dest_steps, x1t, xs0)


def _expert_kernel(tile_e_ref, n_used_ref, xs_ref, wu_ref, bu_ref, wd_ref, bd_ref, y_ref, wu_scr, wd_scr):
    i = pl.program_id(0)
    used = i < n_used_ref[0]
    new_expert = (i == 0) | (tile_e_ref[i] != tile_e_ref[jnp.maximum(i - 1, 0)])
    tm = xs_ref.shape[0] // ROW_SUBLANES

    @pl.when(used & new_expert)
    def _():
        wu_scr[...] = wu_ref[0].astype(jnp.bfloat16)
        wd_scr[...] = wd_ref[0].astype(jnp.bfloat16)

    @pl.when(used)
    def _():
        xs = _rows_from_tiles(xs_ref, tm).astype(jnp.bfloat16)
        hu = _dot(xs, wu_scr[...]) + bu_ref[0]
        g = jnp.minimum(hu[:, :D_FF], SWIGLU_LIMIT)
        u = jnp.clip(hu[:, D_FF:], -SWIGLU_LIMIT, SWIGLU_LIMIT)
        act = g * (1.0 / (1.0 + jnp.exp(-SWIGLU_ALPHA * g))) * (u + 1.0)
        _rows_to_tiles(y_ref, _dot(act.astype(jnp.bfloat16), wd_scr[...]) + bd_ref[0])

    @pl.when(jnp.logical_not(used))
    def _():
        y_ref[...] = jnp.zeros(y_ref.shape, y_ref.dtype)


def _experts(tile_e, n_used, xs, wu, bu, wd, bd):
    D = ROW_SUBLANES * LANES
    n_rows = xs.shape[0] // ROW_SUBLANES
    tm = MOE_TILE
    tile_spec = pl.BlockSpec((tm * ROW_SUBLANES, LANES), lambda i, te, nu: (i, 0))
    grid_spec = pltpu.PrefetchScalarGridSpec(
        num_scalar_prefetch=2,
        grid=(n_rows // tm,),
        in_specs=[
            tile_spec,
            pl.BlockSpec((1, D, 2 * D_FF), lambda i, te, nu: (te[i], 0, 0)),
            pl.BlockSpec((1, 1, 2 * D_FF), lambda i, te, nu: (te[i], 0, 0)),
            pl.BlockSpec((1, D_FF, D), lambda i, te, nu: (te[i], 0, 0)),
            pl.BlockSpec((1, 1, D), lambda i, te, nu: (te[i], 0, 0)),
        ],
        out_specs=tile_spec,
        scratch_shapes=[pltpu.VMEM((D, 2 * D_FF), jnp.bfloat16), pltpu.VMEM((D_FF, D), jnp.bfloat16)],
    )
    return pl.pallas_call(
        _expert_kernel,
        grid_spec=grid_spec,
        out_shape=jax.ShapeDtypeStruct(xs.shape, jnp.float32),
        compiler_params=_cparams(("arbitrary",)),
        name="experts",
    )(tile_e, n_used, xs, wu, bu, wd, bd)


def _combine_kernel(dest_hbm, y_hbm, x1_ref, gate_ref, g_ref, b_ref, o_ref, buf, idx_smem, idx_sem, row_sems):
    i = pl.program_id(0)
    n = pl.num_programs(0)
    tm = x1_ref.shape[0]

    def issue(step, slot):
        base = _stage_index(dest_hbm, step, n, idx_smem, idx_sem)

        def body(t, carry):
            for k in range(TOP_K):
                pltpu.make_async_copy(_row_tile(y_hbm, idx_smem[base + k * tm + t]),
                                      _row_tile(buf.at[slot, k], t), row_sems.at[slot]).start(priority=k % 2)
            return carry

        lax.fori_loop(0, tm, body, 0, unroll=8)

    @pl.when(i == 0)
    def _():
        issue(i, 0)

    @pl.when(i + 1 < n)
    def _():
        issue(i + 1, (i + 1) % 2)

    slot = i % 2
    for k in range(TOP_K):
        pltpu.make_async_copy(y_hbm.at[pl.ds(0, tm * ROW_SUBLANES)], buf.at[slot, k], row_sems.at[slot]).wait()
    ffn = _rows_from_tiles(buf.at[slot, 0], tm) * gate_ref[:, 0:1]
    for k in range(1, TOP_K):
        ffn = ffn + _rows_from_tiles(buf.at[slot, k], tm) * gate_ref[:, k:k + 1]
    o_ref[...] = _layer_norm(DEEPNORM_ALPHA * x1_ref[...] + ffn, g_ref[...], b_ref[...])


def _combine(dest_steps, y, x1, gates_t, g, b):
    T, D = x1.shape
    n_steps, per_step = dest_steps.shape
    tm = per_step // TOP_K
    full = lambda a: pl.BlockSpec(a.shape, lambda i: (0, 0))
    return pl.pallas_call(
        _combine_kernel,
        grid=(n_steps,),
        in_specs=[
            pl.BlockSpec(memory_space=pl.ANY),
            pl.BlockSpec(memory_space=pl.ANY),
            pl.BlockSpec((tm, D), lambda i: (i, 0)),
            pl.BlockSpec((tm, 8), lambda i: (i, 0)),
            full(g), full(b),
        ],
        out_specs=pl.BlockSpec((tm, D), lambda i: (i, 0)),
        out_shape=jax.ShapeDtypeStruct((T, D), jnp.float32),
        scratch_shapes=[pltpu.VMEM((2, TOP_K, tm * ROW_SUBLANES, LANES), jnp.float32),
                        pltpu.SMEM((2 * per_step,), jnp.int32),
                        pltpu.SemaphoreType.DMA(()), pltpu.SemaphoreType.DMA((2,))],
        compiler_params=_cparams(("arbitrary",)),
        name="combine_ln2",
    )(dest_steps, y, x1, gates_t, g, b)


def kernel(x, w_in, w_out, lambda_q1, lambda_k1, lambda_q2, lambda_k2, diff_norm_g, ln1_g, ln1_b,
           w_router, b_router, w_up, b_up, w_down, b_down, ln2_g, ln2_b):
    B, S, D = x.shape
    T = B * S
    assert w_in.shape[0] == 1, "single layer"
    assert S % DIFF_TQ == 0 and DIFF_TK == DIFF_TQ and DIFF_TK % ROW_TILE == 0 and S % (16 * DIL_TQ) == 0

    na = 3 * DIL_WIDTH
    colscale = np.ones((3 * D,), np.float32)
    colscale[:DIL_WIDTH] = HEAD_DIM ** -0.5
    colscale[na:na + DIFF_WIDTH] = HEAD_DIM ** -0.5 * LOG2E
    w_in_bf = (w_in[0] * colscale).astype(jnp.bfloat16)
    nv = na + 2 * DIFF_WIDTH

    kpos, qpos, dabs = _position_columns()
    pa, pa4, pa16, qd, ka, vt, qn, kn = _in_proj(x, w_in_bf[:, :nv], w_in_bf[:, nv:].T, kpos)
    dil = [_dilated_stage(p, d) for p, (_, d) in zip((pa[:, None], pa4, pa16), DIL_PATTERNS)]

    r2 = lambda a: a.reshape(1, -1).astype(jnp.float32)
    od = _diff_attention(_tile_reach(qn, kn, S // DIFF_TK), qd, ka, vt, qpos, dabs, r2(lambda_q1[0]), r2(lambda_k1[0]), r2(lambda_q2[0]),
                         r2(lambda_k2[0]), diff_norm_g[0].reshape(-1, 1).astype(jnp.float32))

    tri = jnp.asarray(np.triu(np.ones((ROW_TILE, ROW_TILE), np.float32), k=1), jnp.bfloat16)
    n_rows = T * TOP_K + N_EXPERTS * MOE_TILE
    n_tiles = n_rows // MOE_TILE
    wr_t = w_router[0].T.astype(jnp.float32)
    wr_hi = wr_t.astype(jnp.bfloat16)
    wr_split = jnp.stack([wr_hi, (wr_t - wr_hi.astype(jnp.float32)).astype(jnp.bfloat16)])
    x1, x1t, xs0, top_idx, gates, rank, counts = _post_attn(
        dil, od.reshape(T, DIFF_WIDTH), x.reshape(T, D), w_out[0].astype(jnp.bfloat16),
        r2(ln1_g[0]), r2(ln1_b[0]), wr_split, b_router[0].reshape(N_EXPERTS, 1), tri, n_rows)

    counts = counts[:, 0]
    pcounts = ((counts + MOE_TILE - 1) // MOE_TILE) * MOE_TILE
    pends = jnp.cumsum(pcounts)
    pstarts = pends - pcounts
    dest = rank[:TOP_K]
    for e in range(N_EXPERTS):
        dest = dest + jnp.where(top_idx[:TOP_K] == e, pstarts[e], 0)
    tile_starts = jnp.arange(n_tiles, dtype=jnp.int32) * MOE_TILE
    tile_e = jnp.minimum(jnp.sum(pends[None, :] <= tile_starts[:, None], axis=1), N_EXPERTS - 1).astype(jnp.int32)
    n_used = (pends[-1] // MOE_TILE).astype(jnp.int32).reshape(1)

    def per_step(tm):
        return dest.reshape(TOP_K, T // tm, tm).transpose(1, 0, 2).reshape(T // tm, TOP_K * tm)

    xs = _dispatch(per_step(ROW_TILE), x1t, xs0)
    y = _experts(tile_e, n_used, xs, w_up[0], b_up[0][:, None, :], w_down[0], b_down[0][:, None, :])
    out = _combine(per_step(COMBINE_TILE), y, x1, gates.T, r2(ln2_g[0]), r2(ln2_b[0]))
    return out.reshape(B, S, D)
```

```python
import functools
import math

import jax
import jax.numpy as jnp
import numpy as np
from jax import lax
from jax.experimental import pallas as pl
from jax.experimental.pallas import tpu as pltpu

D_MODEL = 1024
HEAD_DIM = 64
DIL_WIDTH = 512
N_HEADS_DIL = 8
DIL_PATTERNS = ((128, 1), (512, 4), (2048, 16))
DIL_SIDE = 64
DIFF_WIDTH = 512
N_HEADS_DIFF = 4
DIFF_VDIM = 2 * HEAD_DIM
N_EXPERTS = 32
TOP_K = 4
D_FF = D_MODEL
SWIGLU_ALPHA = 1.702
SWIGLU_LIMIT = 7.0
DEEPNORM_ALPHA = 2.0 ** 0.25
LN_EPS = 1e-5
NEG_INF = -1e30
LAM_INIT = 0.8 - 0.6 * math.exp(-0.3 * 0)
LOG2E = math.log2(math.e)

LANES = 128
VMEM_LIMIT = 56 * 1024 * 1024

ROW_TILE = 512
DIFF_TQ = 1024
DIFF_TK = 1024
V_ROWS = DIFF_VDIM + 16
EXP2_HEADROOM = 4.0
UNDERFLOW_LOG2 = 160.0
NORM_SLACK = 1.01
DIL_TQ = 128
DIL_QB = 2
MOE_TILE = 512
COMBINE_TILE = 256
ROW_SUBLANES = 8
PAD_BITS = 9
POS_SPLIT = 16


def _cparams(sem):
    return pltpu.CompilerParams(dimension_semantics=sem, vmem_limit_bytes=VMEM_LIMIT)


def _dot_nt(a, b, **kw):
    return lax.dot_general(a, b, (((1,), (1,)), ((), ())), preferred_element_type=jnp.float32, **kw)


def _dot(a, b, **kw):
    return jnp.dot(a, b, preferred_element_type=jnp.float32, **kw)


def _layer_norm(y, g, b):
    mu = jnp.mean(y, axis=-1, keepdims=True)
    yc = y - mu
    var = jnp.mean(yc * yc, axis=-1, keepdims=True)
    return yc * lax.rsqrt(var + LN_EPS) * g + b


def _in_proj_kernel(x_ref, w_ref, wvt_ref, kpos_ref, grp_ref, pa_ref, pa4_ref, pa16_ref, qd_ref, ka_ref,
                    vt_ref, qn_ref, kn_ref, pa_scr):
    xb = x_ref[0].astype(jnp.bfloat16)
    na = 3 * DIL_WIDTH
    tm = xb.shape[0]
    paf = _dot(xb, w_ref[:, :na])
    pa_ref[0] = paf.astype(jnp.bfloat16)
    for c in range(na // LANES):
        cols = slice(c * LANES, (c + 1) * LANES)
        pa_scr[c] = paf[:, cols]
        for d, ref in ((DIL_PATTERNS[1][1], pa4_ref), (DIL_PATTERNS[2][1], pa16_ref)):
            for r in range(d):
                ref[0, r, :, cols] = pa_scr[c, pl.ds(r, tm // d, stride=d), :].astype(jnp.bfloat16)
    q = _dot(xb, w_ref[:, na:na + DIFF_WIDTH]).astype(jnp.bfloat16)
    k = _dot(xb, w_ref[:, na + DIFF_WIDTH:]).astype(jnp.bfloat16)
    vt = _dot_nt(wvt_ref[...], xb).astype(jnp.bfloat16)
    ones = jnp.ones((V_ROWS - DIFF_VDIM, tm), jnp.bfloat16)

    @pl.when(pl.program_id(1) == 0)
    def _():
        qn_ref[...] = jnp.zeros(qn_ref.shape, jnp.float32)
        kn_ref[...] = jnp.zeros(kn_ref.shape, jnp.float32)

    for src, dst in ((q, qn_ref), (k, kn_ref)):
        f = src.astype(jnp.float32)
        sq = f * f
        sq_hi = sq.astype(jnp.bfloat16)
        sq_lo = (sq - sq_hi.astype(jnp.float32)).astype(jnp.bfloat16)
        gs = _dot(sq_hi, grp_ref[...]) + _dot(sq_lo, grp_ref[...])
        dst[0] = jnp.maximum(dst[0], jnp.max(gs, axis=0, keepdims=True))
    for h in range(N_HEADS_DIFF):
        sl = slice(h * LANES, (h + 1) * LANES)
        qd_ref[0, h] = q[:, sl]
        ka_ref[0, h, :, :LANES] = k[:, sl]
        ka_ref[0, h, :, LANES:] = kpos_ref[h]
        vt_ref[0, h, :DIFF_VDIM, :] = vt[sl, :]
        vt_ref[0, h, DIFF_VDIM:, :] = ones


def _in_proj(x, w_bf, wvt_bf, kpos):
    B, S, D = x.shape
    tm = ROW_TILE
    na = 3 * DIL_WIDTH
    grp = np.zeros((DIFF_WIDTH, LANES), np.float32)
    grp[np.arange(DIFF_WIDTH), np.arange(DIFF_WIDTH) // HEAD_DIM] = 1.0
    grp = jnp.asarray(grp, jnp.bfloat16)
    norm_spec = pl.BlockSpec((1, 1, LANES), lambda b, i: (b, 0, 0))
    d4, d16 = DIL_PATTERNS[1][1], DIL_PATTERNS[2][1]
    return pl.pallas_call(
        _in_proj_kernel,
        grid=(B, S // tm),
        in_specs=[
            pl.BlockSpec((1, tm, D), lambda b, i: (b, i, 0)),
            pl.BlockSpec(w_bf.shape, lambda b, i: (0, 0)),
            pl.BlockSpec(wvt_bf.shape, lambda b, i: (0, 0)),
            pl.BlockSpec((N_HEADS_DIFF, tm, LANES), lambda b, i: (0, i % (DIFF_TK // tm), 0)),
            pl.BlockSpec(grp.shape, lambda b, i: (0, 0)),
        ],
        out_specs=[
            pl.BlockSpec((1, tm, na), lambda b, i: (b, i, 0)),
            pl.BlockSpec((1, d4, tm // d4, na), lambda b, i: (b, 0, i, 0)),
            pl.BlockSpec((1, d16, tm // d16, na), lambda b, i: (b, 0, i, 0)),
            pl.BlockSpec((1, N_HEADS_DIFF, tm, LANES), lambda b, i: (b, 0, i, 0)),
            pl.BlockSpec((1, N_HEADS_DIFF, tm, 2 * LANES), lambda b, i: (b, 0, i, 0)),
            pl.BlockSpec((1, N_HEADS_DIFF, V_ROWS, tm), lambda b, i: (b, 0, 0, i)),
            norm_spec, norm_spec,
        ],
        out_shape=[
            jax.ShapeDtypeStruct((B, S, na), jnp.bfloat16),
            jax.ShapeDtypeStruct((B, d4, S // d4, na), jnp.bfloat16),
            jax.ShapeDtypeStruct((B, d16, S // d16, na), jnp.bfloat16),
            jax.ShapeDtypeStruct((B, N_HEADS_DIFF, S, LANES), jnp.bfloat16),
            jax.ShapeDtypeStruct((B, N_HEADS_DIFF, S, 2 * LANES), jnp.bfloat16),
            jax.ShapeDtypeStruct((B, N_HEADS_DIFF, V_ROWS, S), jnp.bfloat16),
            jax.ShapeDtypeStruct((B, 1, LANES), jnp.float32),
            jax.ShapeDtypeStruct((B, 1, LANES), jnp.float32),
        ],
        scratch_shapes=[pltpu.VMEM((na // LANES, tm, LANES), jnp.float32)],
        compiler_params=_cparams(("parallel", "arbitrary")),
        name="in_proj",
    )(x, w_bf, wvt_bf, kpos, grp)


def _dilated_kernel(*refs, seq_len):
    n_halo = 2 * DIL_QB + 2
    q_ref, k_refs, v_refs = refs[0], refs[1:1 + n_halo], refs[1 + n_halo:1 + 2 * n_halo]
    bias_ref, o_ref, lse_ref = refs[1 + 2 * n_halo:]
    tq = DIL_TQ
    nk = tq + 2 * DIL_SIDE
    i = pl.program_id(2)
    q = q_ref[0, 0]
    kwin = jnp.concatenate([r[0, 0] for r in k_refs], axis=0)
    vwin = jnp.concatenate([r[0, 0] for r in v_refs], axis=0)
    low_half = lax.broadcasted_iota(jnp.int32, (tq, LANES), 1) < HEAD_DIM
    ss = []
    for u in range(DIL_QB):
        key_pos = (i * DIL_QB + u) * tq - DIL_SIDE + lax.broadcasted_iota(jnp.int32, (1, nk), 1)
        edge = jnp.where((key_pos >= 0) & (key_pos < seq_len), 0.0, NEG_INF)
        for h in range(N_HEADS_DIL):
            sl = slice((h // 2) * LANES, (h // 2 + 1) * LANES)
            qu = q[u * tq:(u + 1) * tq, sl]
            qm = jnp.where(low_half == (h % 2 == 0), qu, jnp.zeros((tq, LANES), q.dtype))
            ss.append(_dot_nt(qm, kwin[u * tq:u * tq + nk, sl]) + bias_ref[h] + edge)
    s = jnp.concatenate(ss, axis=0)
    m = jnp.max(s, axis=-1, keepdims=True)
    p = jnp.exp(s - m)
    l = jnp.sum(p, axis=-1, keepdims=True)
    lse = m + jnp.log(l)
    inv_l = 1.0 / l
    pb = p.astype(jnp.bfloat16)
    for u in range(DIL_QB):
        for pair in range(N_HEADS_DIL // 2):
            sl = slice(pair * LANES, (pair + 1) * LANES)
            base = (u * N_HEADS_DIL + 2 * pair) * tq
            r0 = slice(base, base + tq)
            r1 = slice(base + tq, base + 2 * tq)
            vu = vwin[u * tq:u * tq + nk, sl]
            a0 = _dot(pb[r0], vu) * inv_l[r0]
            a1 = _dot(pb[r1], vu) * inv_l[r1]
            rows = slice(u * tq, (u + 1) * tq)
            o_ref[0, 0, rows, sl] = jnp.where(low_half, a0, a1).astype(o_ref.dtype)
            lse_ref[0, 0, rows, sl] = jnp.where(low_half, lse[r0], lse[r1])


def _dilated_stage(pa_d, dilation):
    B, d, L, _ = pa_d.shape
    tb = DIL_QB * DIL_TQ
    w = DIL_WIDTH
    halo = DIL_SIDE
    n_halo = L // halo
    per_step = tb // halo + 2
    bias = _dilated_bias(dilation)
    assert L % tb == 0

    def halo_spec(col, j):
        def imap(b, r, i):
            return (b, r, jnp.clip(i * (tb // halo) - 1 + j, 0, n_halo - 1), col)
        return pl.BlockSpec((1, 1, halo, w), imap)

    out_spec = pl.BlockSpec((1, 1, tb, w), lambda b, r, i: (b, r, i, 0))
    in_specs = ([pl.BlockSpec((1, 1, tb, w), lambda b, r, i: (b, r, i, 0))]
                + [halo_spec(1, j) for j in range(per_step)] + [halo_spec(2, j) for j in range(per_step)]
                + [pl.BlockSpec(bias.shape, lambda b, r, i: (0, 0, 0))])
    return pl.pallas_call(
        functools.partial(_dilated_kernel, seq_len=L),
        grid=(B, d, L // tb),
        in_specs=in_specs,
        out_specs=[out_spec, out_spec],
        out_shape=[jax.ShapeDtypeStruct((B, d, L, w), jnp.bfloat16),
                   jax.ShapeDtypeStruct((B, d, L, w), jnp.float32)],
        compiler_params=_cparams(("parallel", "parallel", "parallel")),
        name=f"dilated_d{d}",
    )(*([pa_d] * (1 + 2 * per_step)), bias)


def _dilated_bias(dilation):
    tq = DIL_TQ
    slopes = np.asarray([2.0 ** (-8.0 * (i + 1) / N_HEADS_DIL) for i in range(N_HEADS_DIL)], np.float32)
    rel = (np.arange(tq + 2 * DIL_SIDE)[None, :] - DIL_SIDE) - np.arange(tq)[:, None]
    band = np.abs(rel) <= DIL_SIDE
    pen = -(slopes * dilation)[:, None, None] * np.abs(rel).astype(np.float32)[None]
    return jnp.asarray(np.where(band[None], pen, np.float32(NEG_INF)), jnp.float32)


def _diff_kernel(reach_ref, q_ref, ka_hbm, vt_hbm, qpos_ref, dabs_ref, lq1_ref, lk1_ref, lq2_ref, lk2_ref,
                 g_ref, o_ref, qa_scr, m_scr, acc_scr, k_buf, v_buf, sems, *, slopes, n_key_tiles):
    b = pl.program_id(0)
    h = pl.program_id(1)
    qi = pl.program_id(2)
    tq = q_ref.shape[2]
    tk = k_buf.shape[1]

    slope = jnp.float32(slopes[0])
    for hh in range(1, N_HEADS_DIFF):
        slope = jnp.where(h == hh, jnp.float32(slopes[hh]), slope)

    reach = reach_ref[b * N_HEADS_DIFF + h]
    first = jnp.maximum(qi - reach, 0)
    n_active = jnp.minimum(qi + reach, n_key_tiles - 1) - first + 1

    def key_tile(j):
        t = first + j - 1
        return jnp.where(j == 0, qi, jnp.where(t < qi, t, t + 1))

    def tile_copies(j, slot):
        start = pl.multiple_of(key_tile(j) * tk, tk)
        return (pltpu.make_async_copy(ka_hbm.at[b, h, pl.ds(start, tk), :], k_buf.at[slot], sems.at[0, slot]),
                pltpu.make_async_copy(vt_hbm.at[b, h, :, pl.ds(start, tk)], v_buf.at[slot], sems.at[1, slot]))

    for cp in tile_copies(0, 0):
        cp.start()

    @pl.when(n_active > 1)
    def _():
        for cp in tile_copies(1, 1):
            cp.start()

    q = q_ref[0, 0]
    low_half = lax.broadcasted_iota(jnp.int32, (tq, LANES), 1) < HEAD_DIM
    zero = jnp.zeros_like(q)
    qpos = qpos_ref[0]
    for c in range(2):
        qc = jnp.where(low_half == (c == 0), q, zero)
        for var, pos in enumerate((qpos, -qpos, jnp.zeros_like(qpos))):
            qa_scr[c, var, :, :LANES] = qc
            qa_scr[c, var, :, LANES:] = pos
    m_scr[...] = jnp.full(m_scr.shape, NEG_INF, jnp.float32)
    acc_scr[0] = jnp.zeros(acc_scr.shape[1:], jnp.float32)


    def restabilise(slot, kt, diag, cur):
        var = 2 if diag else jnp.where(kt < qi, 0, 1)
        off = -slope * jnp.abs(qi * tq - kt * tk).astype(jnp.float32)
        ka = k_buf[slot]
        vt = v_buf[slot]
        ss = []
        for c in range(2):
            s = _dot_nt(ka, qa_scr[c, var])
            if diag:
                s = s + dabs_ref[...] * slope
            ss.append(s)
        for c in range(2):
            s = ss[c]
            m_old = m_scr[c]
            m_new = jnp.maximum(m_old, jnp.max(s, axis=0, keepdims=True) + off)
            alpha = jnp.exp2(m_old - m_new)
            p = jnp.exp2((s - (m_new - off)).astype(jnp.bfloat16))
            acc_scr[cur, c] = alpha * acc_scr[cur, c] + _dot(vt, p)
            m_scr[c] = m_new

    for cp in tile_copies(0, 0):
        cp.wait()
    restabilise(0, qi, True, 0)

    def visit(j, cur):
        slot = j % 2

        @pl.when(j + 1 < n_active)
        def _():
            for cp in tile_copies(j + 1, 1 - slot):
                cp.start()

        for cp in tile_copies(j, slot):
            cp.wait()
        kt = key_tile(j)
        var = jnp.where(kt < qi, 0, 1)
        off = -slope * jnp.abs(qi * tq - kt * tk).astype(jnp.float32)
        ka = k_buf[slot]
        vt = v_buf[slot]
        excess = None
        for c in range(2):
            s = _dot_nt(ka, qa_scr[c, var])
            m_eff = m_scr[c] - off
            over = jnp.max(jnp.max(s, axis=0, keepdims=True) - m_eff)
            excess = over if excess is None else jnp.maximum(excess, over)
            p = jnp.exp2((s - m_eff).astype(jnp.bfloat16))
            acc_scr[1 - cur, c] = acc_scr[cur, c] + _dot(vt, p)

        @pl.when(excess > EXP2_HEADROOM)
        def _():
            restabilise(slot, kt, False, cur)

        return jnp.where(excess <= EXP2_HEADROOM, 1 - cur, cur)

    cur = lax.fori_loop(1, n_active, visit, jnp.int32(0))

    lam = (jnp.exp(jnp.sum(lq1_ref[...] * lk1_ref[...], axis=-1, keepdims=True))
           - jnp.exp(jnp.sum(lq2_ref[...] * lk2_ref[...], axis=-1, keepdims=True)) + LAM_INIT)
    a1, a2 = acc_scr[cur, 0], acc_scr[cur, 1]
    o = (a1[:DIFF_VDIM] / a1[DIFF_VDIM:DIFF_VDIM + 1]
         - lam * (a2[:DIFF_VDIM] / a2[DIFF_VDIM:DIFF_VDIM + 1]))
    o = o * lax.rsqrt(jnp.mean(o * o, axis=0, keepdims=True) + LN_EPS) * g_ref[...]
    o_ref[0] = (o * (1.0 - LAM_INIT)).T.astype(o_ref.dtype)


def _tile_reach(qn, kn, n_tiles):
    B = qn.shape[0]
    ng = 2 * N_HEADS_DIFF
    qmax = jnp.sqrt(qn[:, 0, :ng]) * NORM_SLACK
    kmax = jnp.sqrt(kn[:, 0, :ng]) * NORM_SLACK
    bound = jnp.max((2.0 * qmax * kmax).reshape(B, N_HEADS_DIFF, 2), axis=-1) + UNDERFLOW_LOG2
    far = bound / jnp.asarray(_diff_slopes_log2())[None, :]
    reach = jnp.floor((far - 1.0) / DIFF_TK) + 1.0
    reach = jnp.where(jnp.isfinite(reach), reach, n_tiles)
    return jnp.clip(reach, 0, n_tiles).astype(jnp.int32).reshape(-1)


def _diff_attention(reach, qd, ka, vt, qpos, dabs, lq1, lk1, lq2, lk2, g_col):
    B, H, S, _ = qd.shape
    tq, tk = DIFF_TQ, DIFF_TK
    slopes = tuple(float(s) for s in _diff_slopes_log2())
    nk = S // tk
    small = lambda a: pl.BlockSpec(a.shape, lambda b, h, qi, reach: (0, 0))
    grid_spec = pltpu.PrefetchScalarGridSpec(
        num_scalar_prefetch=1,
        grid=(B, H, S // tq),
        in_specs=[
            pl.BlockSpec((1, 1, tq, LANES), lambda b, h, qi, reach: (b, h, qi, 0)),
            pl.BlockSpec(memory_space=pl.ANY),
            pl.BlockSpec(memory_space=pl.ANY),
            pl.BlockSpec((1, tq, LANES), lambda b, h, qi, reach: (h, 0, 0)),
            small(dabs), small(lq1), small(lk1), small(lq2), small(lk2), small(g_col),
        ],
        out_specs=pl.BlockSpec((1, tq, LANES), lambda b, h, qi, reach: (b, qi, h)),
        scratch_shapes=[
            pltpu.VMEM((2, 3, tq, 2 * LANES), jnp.bfloat16),
            pltpu.VMEM((2, 1, tq), jnp.float32),
            pltpu.VMEM((2, 2, V_ROWS, tq), jnp.float32),
            pltpu.VMEM((2, tk, 2 * LANES), jnp.bfloat16),
            pltpu.VMEM((2, V_ROWS, tk), jnp.bfloat16),
            pltpu.SemaphoreType.DMA((2, 2)),
        ],
    )
    return pl.pallas_call(
        functools.partial(_diff_kernel, slopes=slopes, n_key_tiles=nk),
        grid_spec=grid_spec,
        out_shape=jax.ShapeDtypeStruct((B, S, H * LANES), jnp.bfloat16),
        compiler_params=_cparams(("parallel", "parallel", "arbitrary")),
        name="diff_attention",
    )(reach, qd, ka, vt, qpos, dabs, lq1, lk1, lq2, lk2, g_col)


def _diff_slopes_log2():
    return np.asarray([2.0 ** (-8.0 * (i + 1) / N_HEADS_DIFF) for i in range(N_HEADS_DIFF)],
                      np.float32) * np.float32(LOG2E)


def _position_columns():
    def bf16_round(v):
        u = np.asarray(v, np.float32).view(np.uint32)
        return ((u + 0x7FFF + ((u >> 16) & 1)) & np.uint32(0xFFFF0000)).view(np.float32)

    kpos = np.zeros((N_HEADS_DIFF, DIFF_TK, LANES), np.float32)
    qpos = np.zeros((N_HEADS_DIFF, DIFF_TQ, LANES), np.float32)
    pk, pq = np.arange(DIFF_TK), np.arange(DIFF_TQ)
    for h, a in enumerate(_diff_slopes_log2()):
        rest = np.float32(a)
        for n in range(3):
            a_n = np.float32(bf16_round(rest))
            rest = np.float32(rest - a_n)
            c = 4 * n
            kpos[h, :, c + 0] = (pk // POS_SPLIT) * POS_SPLIT
            kpos[h, :, c + 1] = pk % POS_SPLIT
            kpos[h, :, c + 2] = a_n
            kpos[h, :, c + 3] = a_n
            qpos[h, :, c + 0] = a_n
            qpos[h, :, c + 1] = a_n
            qpos[h, :, c + 2] = -((pq // POS_SPLIT) * POS_SPLIT)
            qpos[h, :, c + 3] = -(pq % POS_SPLIT)
    dabs = -np.abs(pq[:, None] - pk[None, :]).astype(np.float32)
    return jnp.asarray(kpos, jnp.bfloat16), jnp.asarray(qpos, jnp.bfloat16), jnp.asarray(dabs)


def _post_attn_kernel(o1_ref, l1_ref, o4_ref, l4_ref, o16_ref, l16_ref, od_ref, x_ref, wo_ref, g_ref, b_ref,
                      wr_ref, br_ref, tri_ref,
                      x1_ref, x1t_ref, idx_ref, gate_ref, rank_ref, cnt_ref, carry_scr, order_scr):
    step = pl.program_id(0)

    @pl.when(step == 0)
    def _():
        carry_scr[...] = jnp.zeros(carry_scr.shape, jnp.float32)

    tm = x_ref.shape[0]
    nc = DIL_WIDTH // LANES
    for n, (o_ref, l_ref) in enumerate(((o4_ref, l4_ref), (o16_ref, l16_ref))):
        d = o_ref.shape[1]
        for r in range(d):
            o_r = o_ref[0, r].astype(jnp.float32)
            l_r = l_ref[0, r]
            for c in range(nc):
                cols = slice(c * LANES, (c + 1) * LANES)
                order_scr[2 * n, c, pl.ds(r, tm // d, stride=d), :] = o_r[:, cols]
                order_scr[2 * n + 1, c, pl.ds(r, tm // d, stride=d), :] = l_r[:, cols]
    in_order = lambda n: jnp.concatenate([order_scr[n, c] for c in range(nc)], axis=1)
    outs = (o1_ref[0, 0].astype(jnp.float32), in_order(0), in_order(2))
    lses = (l1_ref[0, 0], in_order(1), in_order(3))
    top = jnp.maximum(jnp.maximum(lses[0], lses[1]), lses[2])
    wts = [jnp.exp(l - top) for l in lses]
    oa = ((wts[0] * outs[0] + wts[1] * outs[1] + wts[2] * outs[2])
          / (wts[0] + wts[1] + wts[2])).astype(jnp.bfloat16)

    mix = _dot(oa, wo_ref[:DIL_WIDTH, :]) + _dot(od_ref[...], wo_ref[DIL_WIDTH:, :])
    x1 = _layer_norm(DEEPNORM_ALPHA * x_ref[...] + mix, g_ref[...], b_ref[...])
    x1_ref[...] = x1
    for s in range(ROW_SUBLANES):
        x1t_ref[pl.ds(s, x1.shape[0], stride=ROW_SUBLANES), :] = x1[:, s * LANES:(s + 1) * LANES]

    tm = x1.shape[0]
    x_hi = x1.astype(jnp.bfloat16)
    x_lo = (x1 - x_hi.astype(jnp.float32)).astype(jnp.bfloat16)
    lg = (_dot_nt(wr_ref[0], x_hi) + _dot_nt(wr_ref[1], x_hi) + _dot_nt(wr_ref[0], x_lo)
          + br_ref[...])
    eidx = lax.broadcasted_iota(jnp.int32, (N_EXPERTS, tm), 0)
    vals, sels = [], []
    for k in range(TOP_K):
        mx = jnp.max(lg, axis=0, keepdims=True)
        idx = jnp.min(jnp.where(lg == mx, eidx, N_EXPERTS), axis=0, keepdims=True)
        sel = eidx == idx
        vals.append(mx)
        sels.append(sel)
        idx_ref[k:k + 1, :] = idx
        lg = jnp.where(sel, -jnp.inf, lg)
    ex = [jnp.exp(v - vals[0]) for v in vals]
    den = ex[0] + ex[1] + ex[2] + ex[3]
    for k in range(TOP_K):
        gate_ref[k:k + 1, :] = ex[k] / den

    chosen = (sels[0] | sels[1] | sels[2] | sels[3])
    onehot = jnp.where(chosen, 1.0, 0.0)
    before = _dot(onehot.astype(jnp.bfloat16), tri_ref[...]) + carry_scr[...]
    for k in range(TOP_K):
        rank_ref[k:k + 1, :] = jnp.sum(jnp.where(sels[k], before, 0.0), axis=0,
                                       keepdims=True).astype(jnp.int32)
    carry_scr[...] = carry_scr[...] + jnp.sum(onehot, axis=1, keepdims=True)
    pad = jnp.zeros((8 - TOP_K, tm), jnp.int32)
    idx_ref[TOP_K:, :] = pad
    rank_ref[TOP_K:, :] = pad
    gate_ref[TOP_K:, :] = pad.astype(jnp.float32)
    cnt_ref[...] = jnp.broadcast_to(carry_scr[...], cnt_ref.shape).astype(jnp.int32)


def _post_attn(dil, od, x, wo_bf, g, b, wr_t, br, tri):
    T, D = x.shape
    assert D == ROW_SUBLANES * LANES
    tm = ROW_TILE
    nb = dil[0][0].shape[2] // tm
    row = lambda w: pl.BlockSpec((tm, w), lambda i: (i, 0))
    full = lambda a: pl.BlockSpec(a.shape, lambda i: (0,) * a.ndim)
    col = pl.BlockSpec((8, tm), lambda i: (0, i))
    dil_specs, dil_args = [], []
    for o_d, lse_d in dil:
        d = o_d.shape[1]
        spec = pl.BlockSpec((1, d, tm // d, DIL_WIDTH), lambda i: (i // nb, 0, i % nb, 0))
        dil_specs += [spec, spec]
        dil_args += [o_d, lse_d]
    return pl.pallas_call(
        _post_attn_kernel,
        grid=(T // tm,),
        in_specs=dil_specs + [row(DIFF_WIDTH), row(D), full(wo_bf), full(g), full(b), full(wr_t),
                              full(br), full(tri)],
        out_specs=[row(D), pl.BlockSpec((tm * ROW_SUBLANES, LANES), lambda i: (i, 0)), col, col, col,
                   pl.BlockSpec((N_EXPERTS, LANES), lambda i: (0, 0))],
        out_shape=[
            jax.ShapeDtypeStruct((T, D), jnp.float32),
            jax.ShapeDtypeStruct((T * ROW_SUBLANES, LANES), jnp.float32),
            jax.ShapeDtypeStruct((8, T), jnp.int32),
            jax.ShapeDtypeStruct((8, T), jnp.float32),
            jax.ShapeDtypeStruct((8, T), jnp.int32),
            jax.ShapeDtypeStruct((N_EXPERTS, LANES), jnp.int32),
        ],
        scratch_shapes=[pltpu.VMEM((N_EXPERTS, 1), jnp.float32),
                        pltpu.VMEM((4, DIL_WIDTH // LANES, tm, LANES), jnp.float32)],
        compiler_params=_cparams(("arbitrary",)),
        name="post_attn_router",
    )(*dil_args, od, x, wo_bf, g, b, wr_t, br, tri)


def _index_copy(dest_hbm, step, idx_smem, sem):
    per_step = dest_hbm.shape[1]
    half = pl.multiple_of((step % 2) * per_step, per_step)
    return pltpu.make_async_copy(dest_hbm.at[step], idx_smem.at[pl.ds(half, per_step)], sem)


def _stage_index(dest_hbm, step, n_steps, idx_smem, sem):
    @pl.when(step == 0)
    def _():
        _index_copy(dest_hbm, 0, idx_smem, sem).start()

    _index_copy(dest_hbm, step, idx_smem, sem).wait()

    @pl.when(step + 1 < n_steps)
    def _():
        _index_copy(dest_hbm, step + 1, idx_smem, sem).start()

    return (step % 2) * dest_hbm.shape[1]


def _row_tile(ref, row):
    return ref.at[pl.ds(pl.multiple_of(row * ROW_SUBLANES, ROW_SUBLANES), ROW_SUBLANES)]


def _rows_from_tiles(ref, n):
    return jnp.concatenate([ref[pl.ds(s, n, stride=ROW_SUBLANES), :] for s in range(ROW_SUBLANES)], axis=1)


def _rows_to_tiles(ref, rows):
    for s in range(ROW_SUBLANES):
        ref[pl.ds(s, rows.shape[0], stride=ROW_SUBLANES), :] = rows[:, s * LANES:(s + 1) * LANES]


def _dispatch_kernel(pad_start_ref, n_pad_ref, n_used_ref, dest_hbm, x_ref, xs_hbm, idx_smem, idx_sem, row_sem,
                     zero_scr, zero_sem):
    tm = x_ref.shape[0] // ROW_SUBLANES
    tile_rows = MOE_TILE * ROW_SUBLANES
    n_tiles = xs_hbm.shape[0] // tile_rows

    def pad_copies(start_not_wait):
        for j in range(N_EXPERTS):
            tile = n_used_ref[0] + j

            @pl.when(tile < n_tiles)
            def _():
                dst = xs_hbm.at[pl.ds(pl.multiple_of(tile * tile_rows, tile_rows), tile_rows)]
                cp = pltpu.make_async_copy(zero_scr, dst, zero_sem)
                cp.start() if start_not_wait else cp.wait()

        for e in range(N_EXPERTS):
            n_pad = n_pad_ref[e]
            for bit in reversed(range(PAD_BITS)):
                size = 1 << bit

                @pl.when((n_pad & size) != 0)
                def _():
                    row = pad_start_ref[e] + (n_pad & ~(2 * size - 1))
                    dst = xs_hbm.at[pl.ds(pl.multiple_of(row * ROW_SUBLANES, ROW_SUBLANES), size * ROW_SUBLANES)]
                    cp = pltpu.make_async_copy(zero_scr.at[pl.ds(0, size * ROW_SUBLANES)], dst, zero_sem)
                    cp.start() if start_not_wait else cp.wait()

    @pl.when(pl.program_id(0) == 0)
    def _():
        zero_scr[...] = jnp.zeros(zero_scr.shape, zero_scr.dtype)
        pad_copies(True)
        pad_copies(False)

    base = _stage_index(dest_hbm, pl.program_id(0), pl.num_programs(0), idx_smem, idx_sem)

    def issue(t, carry):
        for k in range(TOP_K):
            pltpu.make_async_copy(_row_tile(x_ref, t), _row_tile(xs_hbm, idx_smem[base + k * tm + t]),
                                  row_sem).start(priority=k % 2)
        return carry

    lax.fori_loop(0, tm, issue, 0, unroll=8)
    for k in range(TOP_K):
        pltpu.make_async_copy(x_ref, xs_hbm.at[pl.ds(0, tm * ROW_SUBLANES)], row_sem).wait()


def _dispatch(pad_start, n_pad, n_used, dest_steps, x1t, n_rows):
    n_steps, per_step = dest_steps.shape
    tm = per_step // TOP_K
    assert MOE_TILE == 1 << PAD_BITS
    grid_spec = pltpu.PrefetchScalarGridSpec(
        num_scalar_prefetch=3,
        grid=(n_steps,),
        in_specs=[
            pl.BlockSpec(memory_space=pl.ANY),
            pl.BlockSpec((tm * ROW_SUBLANES, LANES), lambda i, ps, npad, nu: (i, 0)),
        ],
        out_specs=pl.BlockSpec(memory_space=pl.ANY),
        scratch_shapes=[pltpu.SMEM((2 * per_step,), jnp.int32), pltpu.SemaphoreType.DMA(()),
                        pltpu.SemaphoreType.DMA(()),
                        pltpu.VMEM((MOE_TILE * ROW_SUBLANES, LANES), jnp.float32),
                        pltpu.SemaphoreType.DMA(())],
    )
    return pl.pallas_call(
        _dispatch_kernel,
        grid_spec=grid_spec,
        out_shape=jax.ShapeDtypeStruct((n_rows * ROW_SUBLANES, LANES), jnp.float32),
        compiler_params=_cparams(("arbitrary",)),
        name="dispatch_rows",
    )(pad_start, n_pad, n_used, dest_steps, x1t)


def _expert_kernel(tile_e_ref, n_used_ref, xs_ref, wu_ref, bu_ref, wd_ref, bd_ref, y_ref, wu_scr, wd_scr):
    i = pl.program_id(0)
    used = i < n_used_ref[0]
    new_expert = (i == 0) | (tile_e_ref[i] != tile_e_ref[jnp.maximum(i - 1, 0)])
    tm = xs_ref.shape[0] // ROW_SUBLANES

    @pl.when(used & new_expert)
    def _():
        wu_scr[...] = wu_ref[0].astype(jnp.bfloat16)
        wd_scr[...] = wd_ref[0].astype(jnp.bfloat16)

    @pl.when(used)
    def _():
        xs = _rows_from_tiles(xs_ref, tm).astype(jnp.bfloat16)
        hu = _dot(xs, wu_scr[...]) + bu_ref[0]
        g = jnp.minimum(hu[:, :D_FF], SWIGLU_LIMIT)
        u = jnp.clip(hu[:, D_FF:], -SWIGLU_LIMIT, SWIGLU_LIMIT)
        act = g * (1.0 / (1.0 + jnp.exp(-SWIGLU_ALPHA * g))) * (u + 1.0)
        _rows_to_tiles(y_ref, _dot(act.astype(jnp.bfloat16), wd_scr[...]) + bd_ref[0])

    @pl.when(jnp.logical_not(used))
    def _():
        y_ref[...] = jnp.zeros(y_ref.shape, y_ref.dtype)


def _experts(tile_e, n_used, xs, wu, bu, wd, bd):
    D = ROW_SUBLANES * LANES
    n_rows = xs.shape[0] // ROW_SUBLANES
    tm = MOE_TILE
    tile_spec = pl.BlockSpec((tm * ROW_SUBLANES, LANES), lambda i, te, nu: (i, 0))
    grid_spec = pltpu.PrefetchScalarGridSpec(
        num_scalar_prefetch=2,
        grid=(n_rows // tm,),
        in_specs=[
            pl.BlockSpec((tm * ROW_SUBLANES, LANES), lambda i, te, nu: (jnp.minimum(i, nu[0] - 1), 0)),
            pl.BlockSpec((1, D, 2 * D_FF), lambda i, te, nu: (te[i], 0, 0)),
            pl.BlockSpec((1, 1, 2 * D_FF), lambda i, te, nu: (te[i], 0, 0)),
            pl.BlockSpec((1, D_FF, D), lambda i, te, nu: (te[i], 0, 0)),
            pl.BlockSpec((1, 1, D), lambda i, te, nu: (te[i], 0, 0)),
        ],
        out_specs=tile_spec,
        scratch_shapes=[pltpu.VMEM((D, 2 * D_FF), jnp.bfloat16), pltpu.VMEM((D_FF, D), jnp.bfloat16)],
    )
    return pl.pallas_call(
        _expert_kernel,
        grid_spec=grid_spec,
        out_shape=jax.ShapeDtypeStruct(xs.shape, jnp.float32),
        compiler_params=_cparams(("arbitrary",)),
        name="experts",
    )(tile_e, n_used, xs, wu, bu, wd, bd)


def _combine_kernel(dest_hbm, y_hbm, x1_ref, gate_ref, g_ref, b_ref, o_ref, buf, idx_smem, idx_sem, row_sems):
    i = pl.program_id(0)
    n = pl.num_programs(0)
    tm = x1_ref.shape[0]

    def issue(step, slot):
        base = _stage_index(dest_hbm, step, n, idx_smem, idx_sem)

        def body(t, carry):
            for k in range(TOP_K):
                pltpu.make_async_copy(_row_tile(y_hbm, idx_smem[base + k * tm + t]),
                                      _row_tile(buf.at[slot, k], t), row_sems.at[slot]).start(priority=k % 2)
            return carry

        lax.fori_loop(0, tm, body, 0, unroll=8)

    @pl.when(i == 0)
    def _():
        issue(i, 0)

    @pl.when(i + 1 < n)
    def _():
        issue(i + 1, (i + 1) % 2)

    slot = i % 2
    for k in range(TOP_K):
        pltpu.make_async_copy(y_hbm.at[pl.ds(0, tm * ROW_SUBLANES)], buf.at[slot, k], row_sems.at[slot]).wait()
    ffn = _rows_from_tiles(buf.at[slot, 0], tm) * gate_ref[:, 0:1]
    for k in range(1, TOP_K):
        ffn = ffn + _rows_from_tiles(buf.at[slot, k], tm) * gate_ref[:, k:k + 1]
    o_ref[...] = _layer_norm(DEEPNORM_ALPHA * x1_ref[...] + ffn, g_ref[...], b_ref[...])


def _combine(dest_steps, y, x1, gates_t, g, b):
    T, D = x1.shape
    n_steps, per_step = dest_steps.shape
    tm = per_step // TOP_K
    full = lambda a: pl.BlockSpec(a.shape, lambda i: (0, 0))
    return pl.pallas_call(
        _combine_kernel,
        grid=(n_steps,),
        in_specs=[
            pl.BlockSpec(memory_space=pl.ANY),
            pl.BlockSpec(memory_space=pl.ANY),
            pl.BlockSpec((tm, D), lambda i: (i, 0)),
            pl.BlockSpec((tm, 8), lambda i: (i, 0)),
            full(g), full(b),
        ],
        out_specs=pl.BlockSpec((tm, D), lambda i: (i, 0)),
        out_shape=jax.ShapeDtypeStruct((T, D), jnp.float32),
        scratch_shapes=[pltpu.VMEM((2, TOP_K, tm * ROW_SUBLANES, LANES), jnp.float32),
                        pltpu.SMEM((2 * per_step,), jnp.int32),
                        pltpu.SemaphoreType.DMA(()), pltpu.SemaphoreType.DMA((2,))],
        compiler_params=_cparams(("arbitrary",)),
        name="combine_ln2",
    )(dest_steps, y, x1, gates_t, g, b)


def kernel(x, w_in, w_out, lambda_q1, lambda_k1, lambda_q2, lambda_k2, diff_norm_g, ln1_g, ln1_b,
           w_router, b_router, w_up, b_up, w_down, b_down, ln2_g, ln2_b):
    B, S, D = x.shape
    T = B * S
    assert w_in.shape[0] == 1, "single layer"
    assert S % DIFF_TQ == 0 and DIFF_TK == DIFF_TQ and DIFF_TK % ROW_TILE == 0 and S % (16 * DIL_TQ) == 0

    na = 3 * DIL_WIDTH
    colscale = np.ones((3 * D,), np.float32)
    colscale[:DIL_WIDTH] = HEAD_DIM ** -0.5
    colscale[na:na + DIFF_WIDTH] = HEAD_DIM ** -0.5 * LOG2E
    w_in_bf = (w_in[0] * colscale).astype(jnp.bfloat16)
    nv = na + 2 * DIFF_WIDTH

    kpos, qpos, dabs = _position_columns()
    pa, pa4, pa16, qd, ka, vt, qn, kn = _in_proj(x, w_in_bf[:, :nv], w_in_bf[:, nv:].T, kpos)
    dil = [_dilated_stage(p, d) for p, (_, d) in zip((pa[:, None], pa4, pa16), DIL_PATTERNS)]

    r2 = lambda a: a.reshape(1, -1).astype(jnp.float32)
    od = _diff_attention(_tile_reach(qn, kn, S // DIFF_TK), qd, ka, vt, qpos, dabs, r2(lambda_q1[0]), r2(lambda_k1[0]), r2(lambda_q2[0]),
                         r2(lambda_k2[0]), diff_norm_g[0].reshape(-1, 1).astype(jnp.float32))

    tri = jnp.asarray(np.triu(np.ones((ROW_TILE, ROW_TILE), np.float32), k=1), jnp.bfloat16)
    n_rows = T * TOP_K + N_EXPERTS * MOE_TILE
    n_tiles = n_rows // MOE_TILE
    wr_t = w_router[0].T.astype(jnp.float32)
    wr_hi = wr_t.astype(jnp.bfloat16)
    wr_split = jnp.stack([wr_hi, (wr_t - wr_hi.astype(jnp.float32)).astype(jnp.bfloat16)])
    x1, x1t, top_idx, gates, rank, counts = _post_attn(
        dil, od.reshape(T, DIFF_WIDTH), x.reshape(T, D), w_out[0].astype(jnp.bfloat16),
        r2(ln1_g[0]), r2(ln1_b[0]), wr_split, b_router[0].reshape(N_EXPERTS, 1), tri)

    counts = counts[:, 0]
    pcounts = ((counts + MOE_TILE - 1) // MOE_TILE) * MOE_TILE
    pends = jnp.cumsum(pcounts)
    pstarts = pends - pcounts
    dest = rank[:TOP_K]
    for e in range(N_EXPERTS):
        dest = dest + jnp.where(top_idx[:TOP_K] == e, pstarts[e], 0)
    tile_starts = jnp.arange(n_tiles, dtype=jnp.int32) * MOE_TILE
    tile_e = jnp.minimum(jnp.sum(pends[None, :] <= tile_starts[:, None], axis=1), N_EXPERTS - 1).astype(jnp.int32)
    n_used = (pends[-1] // MOE_TILE).astype(jnp.int32).reshape(1)

    def per_step(tm):
        return dest.reshape(TOP_K, T // tm, tm).transpose(1, 0, 2).reshape(T // tm, TOP_K * tm)

    xs = _dispatch((pstarts + counts).astype(jnp.int32), (pcounts - counts).astype(jnp.int32), n_used,
                   per_step(ROW_TILE), x1t, n_rows)
    y = _experts(tile_e, n_used, xs, w_up[0], b_up[0][:, None, :], w_down[0], b_down[0][:, None, :])
    out = _combine(per_step(COMBINE_TILE), y, x1, gates.T, r2(ln2_g[0]), r2(ln2_b[0]))
    return out.reshape(B, S, D)
```

```python
import functools
import math

import jax
import jax.numpy as jnp
import numpy as np
from jax import lax
from jax.experimental import pallas as pl
from jax.experimental.pallas import tpu as pltpu

D_MODEL = 1024
HEAD_DIM = 64
DIL_WIDTH = 512
N_HEADS_DIL = 8
DIL_PATTERNS = ((128, 1), (512, 4), (2048, 16))
DIL_SIDE = 64
DIFF_WIDTH = 512
N_HEADS_DIFF = 4
DIFF_VDIM = 2 * HEAD_DIM
N_EXPERTS = 32
TOP_K = 4
D_FF = D_MODEL
SWIGLU_ALPHA = 1.702
SWIGLU_LIMIT = 7.0
DEEPNORM_ALPHA = 2.0 ** 0.25
LN_EPS = 1e-5
NEG_INF = -1e30
LAM_INIT = 0.8 - 0.6 * math.exp(-0.3 * 0)
LOG2E = math.log2(math.e)

LANES = 128
SUBLANES = 8
BF16_SUBLANES = 16
VMEM_LIMIT = 56 * 1024 * 1024

ROW_TILE = 512
DIFF_TQ = 1024
DIFF_TK = 1024
V_ROWS = DIFF_VDIM + BF16_SUBLANES
EXP2_HEADROOM = 4.0
UNDERFLOW_LOG2 = 160.0
NORM_SLACK = 1.01
DIL_TQ = 128
DIL_QB = 4
MOE_TILE = 512
COMBINE_TILE = 256
ROW_SUBLANES = D_MODEL // LANES
PAD_BITS = MOE_TILE.bit_length() - 1
POS_SPLIT = 16


def _cparams(sem):
    return pltpu.CompilerParams(dimension_semantics=sem, vmem_limit_bytes=VMEM_LIMIT)


def _dot_nt(a, b, **kw):
    return lax.dot_general(a, b, (((1,), (1,)), ((), ())), preferred_element_type=jnp.float32, **kw)


def _dot(a, b, **kw):
    return jnp.dot(a, b, preferred_element_type=jnp.float32, **kw)


def _layer_norm(y, g, b):
    mu = jnp.mean(y, axis=-1, keepdims=True)
    yc = y - mu
    var = jnp.mean(yc * yc, axis=-1, keepdims=True)
    return yc * lax.rsqrt(var + LN_EPS) * g + b


def _in_proj_kernel(x_ref, w_ref, wvt_ref, kpos_ref, grp_ref, pa_ref, pa4_ref, pa16_ref, qd_ref, ka_ref,
                    vt_ref, qn_ref, kn_ref, pa_scr):
    xb = x_ref[0].astype(jnp.bfloat16)
    na = 3 * DIL_WIDTH
    tm = xb.shape[0]
    paf = _dot(xb, w_ref[:, :na])
    pa_ref[0] = paf.astype(jnp.bfloat16)
    for c in range(na // LANES):
        cols = slice(c * LANES, (c + 1) * LANES)
        pa_scr[c] = paf[:, cols]
        for d, ref in ((DIL_PATTERNS[1][1], pa4_ref), (DIL_PATTERNS[2][1], pa16_ref)):
            for r in range(d):
                ref[0, r, :, cols] = pa_scr[c, pl.ds(r, tm // d, stride=d), :].astype(jnp.bfloat16)
    q = _dot(xb, w_ref[:, na:na + DIFF_WIDTH]).astype(jnp.bfloat16)
    k = _dot(xb, w_ref[:, na + DIFF_WIDTH:]).astype(jnp.bfloat16)
    vt = _dot_nt(wvt_ref[...], xb).astype(jnp.bfloat16)
    ones = jnp.ones((V_ROWS - DIFF_VDIM, tm), jnp.bfloat16)

    @pl.when(pl.program_id(1) == 0)
    def _():
        qn_ref[...] = jnp.zeros(qn_ref.shape, jnp.float32)
        kn_ref[...] = jnp.zeros(kn_ref.shape, jnp.float32)

    for src, dst in ((q, qn_ref), (k, kn_ref)):
        f = src.astype(jnp.float32)
        sq = f * f
        sq_hi = sq.astype(jnp.bfloat16)
        sq_lo = (sq - sq_hi.astype(jnp.float32)).astype(jnp.bfloat16)
        gs = _dot(sq_hi, grp_ref[...]) + _dot(sq_lo, grp_ref[...])
        dst[0] = jnp.maximum(dst[0], jnp.max(gs, axis=0, keepdims=True))
    for h in range(N_HEADS_DIFF):
        sl = slice(h * LANES, (h + 1) * LANES)
        qd_ref[0, h] = q[:, sl]
        ka_ref[0, h, :, :LANES] = k[:, sl]
        ka_ref[0, h, :, LANES:] = kpos_ref[h]
        vt_ref[0, h, :DIFF_VDIM, :] = vt[sl, :]
        vt_ref[0, h, DIFF_VDIM:, :] = ones


def _in_proj(x, w_bf, wvt_bf, kpos):
    B, S, D = x.shape
    tm = ROW_TILE
    na = 3 * DIL_WIDTH
    grp = np.zeros((DIFF_WIDTH, LANES), np.float32)
    grp[np.arange(DIFF_WIDTH), np.arange(DIFF_WIDTH) // HEAD_DIM] = 1.0
    grp = jnp.asarray(grp, jnp.bfloat16)
    norm_spec = pl.BlockSpec((1, 1, LANES), lambda b, i: (b, 0, 0))
    d4, d16 = DIL_PATTERNS[1][1], DIL_PATTERNS[2][1]
    return pl.pallas_call(
        _in_proj_kernel,
        grid=(B, S // tm),
        in_specs=[
            pl.BlockSpec((1, tm, D), lambda b, i: (b, i, 0)),
            pl.BlockSpec(w_bf.shape, lambda b, i: (0, 0)),
            pl.BlockSpec(wvt_bf.shape, lambda b, i: (0, 0)),
            pl.BlockSpec((N_HEADS_DIFF, tm, LANES), lambda b, i: (0, i % (DIFF_TK // tm), 0)),
            pl.BlockSpec(grp.shape, lambda b, i: (0, 0)),
        ],
        out_specs=[
            pl.BlockSpec((1, tm, na), lambda b, i: (b, i, 0)),
            pl.BlockSpec((1, d4, tm // d4, na), lambda b, i: (b, 0, i, 0)),
            pl.BlockSpec((1, d16, tm // d16, na), lambda b, i: (b, 0, i, 0)),
            pl.BlockSpec((1, N_HEADS_DIFF, tm, LANES), lambda b, i: (b, 0, i, 0)),
            pl.BlockSpec((1, N_HEADS_DIFF, tm, 2 * LANES), lambda b, i: (b, 0, i, 0)),
            pl.BlockSpec((1, N_HEADS_DIFF, V_ROWS, tm), lambda b, i: (b, 0, 0, i)),
            norm_spec, norm_spec,
        ],
        out_shape=[
            jax.ShapeDtypeStruct((B, S, na), jnp.bfloat16),
            jax.ShapeDtypeStruct((B, d4, S // d4, na), jnp.bfloat16),
            jax.ShapeDtypeStruct((B, d16, S // d16, na), jnp.bfloat16),
            jax.ShapeDtypeStruct((B, N_HEADS_DIFF, S, LANES), jnp.bfloat16),
            jax.ShapeDtypeStruct((B, N_HEADS_DIFF, S, 2 * LANES), jnp.bfloat16),
            jax.ShapeDtypeStruct((B, N_HEADS_DIFF, V_ROWS, S), jnp.bfloat16),
            jax.ShapeDtypeStruct((B, 1, LANES), jnp.float32),
            jax.ShapeDtypeStruct((B, 1, LANES), jnp.float32),
        ],
        scratch_shapes=[pltpu.VMEM((na // LANES, tm, LANES), jnp.float32)],
        compiler_params=_cparams(("parallel", "arbitrary")),
        name="in_proj",
    )(x, w_bf, wvt_bf, kpos, grp)


def _dilated_kernel(*refs, seq_len):
    n_halo = 2 * DIL_QB + 2
    q_ref, k_refs, v_refs = refs[0], refs[1:1 + n_halo], refs[1 + n_halo:1 + 2 * n_halo]
    bias_ref, o_ref, lse_ref = refs[1 + 2 * n_halo:]
    tq = DIL_TQ
    nk = tq + 2 * DIL_SIDE
    i = pl.program_id(2)
    q = q_ref[0, 0]
    kwin = jnp.concatenate([r[0, 0] for r in k_refs], axis=0)
    vwin = jnp.concatenate([r[0, 0] for r in v_refs], axis=0)
    low_half = lax.broadcasted_iota(jnp.int32, (tq, LANES), 1) < HEAD_DIM
    ss = []
    for u in range(DIL_QB):
        key_pos = (i * DIL_QB + u) * tq - DIL_SIDE + lax.broadcasted_iota(jnp.int32, (1, nk), 1)
        edge = jnp.where((key_pos >= 0) & (key_pos < seq_len), 0.0, NEG_INF)
        for h in range(N_HEADS_DIL):
            sl = slice((h // 2) * LANES, (h // 2 + 1) * LANES)
            qu = q[u * tq:(u + 1) * tq, sl]
            qm = jnp.where(low_half == (h % 2 == 0), qu, jnp.zeros((tq, LANES), q.dtype))
            ss.append(_dot_nt(qm, kwin[u * tq:u * tq + nk, sl]) + bias_ref[h] + edge)
    s = jnp.concatenate(ss, axis=0)
    m = jnp.max(s, axis=-1, keepdims=True)
    p = jnp.exp(s - m)
    l = jnp.sum(p, axis=-1, keepdims=True)
    lse = m + jnp.log(l)
    inv_l = 1.0 / l
    pb = p.astype(jnp.bfloat16)
    for u in range(DIL_QB):
        for pair in range(N_HEADS_DIL // 2):
            sl = slice(pair * LANES, (pair + 1) * LANES)
            base = (u * N_HEADS_DIL + 2 * pair) * tq
            r0 = slice(base, base + tq)
            r1 = slice(base + tq, base + 2 * tq)
            vu = vwin[u * tq:u * tq + nk, sl]
            a0 = _dot(pb[r0], vu) * inv_l[r0]
            a1 = _dot(pb[r1], vu) * inv_l[r1]
            rows = slice(u * tq, (u + 1) * tq)
            o_ref[0, 0, rows, sl] = jnp.where(low_half, a0, a1).astype(o_ref.dtype)
            lse_ref[0, 0, rows, sl] = jnp.where(low_half, lse[r0], lse[r1])


def _dilated_stage(pa_d, dilation):
    B, d, L, _ = pa_d.shape
    tb = DIL_QB * DIL_TQ
    w = DIL_WIDTH
    halo = DIL_SIDE
    n_halo = L // halo
    per_step = tb // halo + 2
    bias = _dilated_bias(dilation)
    assert L % tb == 0

    def halo_spec(col, j):
        def imap(b, r, i):
            return (b, r, jnp.clip(i * (tb // halo) - 1 + j, 0, n_halo - 1), col)
        return pl.BlockSpec((1, 1, halo, w), imap)

    out_spec = pl.BlockSpec((1, 1, tb, w), lambda b, r, i: (b, r, i, 0))
    in_specs = ([pl.BlockSpec((1, 1, tb, w), lambda b, r, i: (b, r, i, 0))]
                + [halo_spec(1, j) for j in range(per_step)] + [halo_spec(2, j) for j in range(per_step)]
                + [pl.BlockSpec(bias.shape, lambda b, r, i: (0, 0, 0))])
    return pl.pallas_call(
        functools.partial(_dilated_kernel, seq_len=L),
        grid=(B, d, L // tb),
        in_specs=in_specs,
        out_specs=[out_spec, out_spec],
        out_shape=[jax.ShapeDtypeStruct((B, d, L, w), jnp.bfloat16),
                   jax.ShapeDtypeStruct((B, d, L, w), jnp.float32)],
        compiler_params=_cparams(("parallel", "parallel", "parallel")),
        name=f"dilated_d{d}",
    )(*([pa_d] * (1 + 2 * per_step)), bias)


def _dilated_bias(dilation):
    tq = DIL_TQ
    slopes = np.asarray([2.0 ** (-8.0 * (i + 1) / N_HEADS_DIL) for i in range(N_HEADS_DIL)], np.float32)
    rel = (np.arange(tq + 2 * DIL_SIDE)[None, :] - DIL_SIDE) - np.arange(tq)[:, None]
    band = np.abs(rel) <= DIL_SIDE
    pen = -(slopes * dilation)[:, None, None] * np.abs(rel).astype(np.float32)[None]
    return jnp.asarray(np.where(band[None], pen, np.float32(NEG_INF)), jnp.float32)


def _diff_kernel(reach_ref, q_ref, ka_hbm, vt_hbm, qpos_ref, dabs_ref, lq1_ref, lk1_ref, lq2_ref, lk2_ref,
                 g_ref, o_ref, qa_scr, m_scr, acc_scr, k_buf, v_buf, sems, *, slopes, n_key_tiles):
    b = pl.program_id(0)
    h = pl.program_id(1)
    qi = pl.program_id(2)
    tq = q_ref.shape[2]
    tk = k_buf.shape[1]

    slope = jnp.float32(slopes[0])
    for hh in range(1, N_HEADS_DIFF):
        slope = jnp.where(h == hh, jnp.float32(slopes[hh]), slope)

    reach = reach_ref[b * N_HEADS_DIFF + h]
    first = jnp.maximum(qi - reach, 0)
    n_active = jnp.minimum(qi + reach, n_key_tiles - 1) - first + 1

    def key_tile(j):
        t = first + j - 1
        return jnp.where(j == 0, qi, jnp.where(t < qi, t, t + 1))

    def tile_copies(j, slot):
        start = pl.multiple_of(key_tile(j) * tk, tk)
        return (pltpu.make_async_copy(ka_hbm.at[b, h, pl.ds(start, tk), :], k_buf.at[slot], sems.at[0, slot]),
                pltpu.make_async_copy(vt_hbm.at[b, h, :, pl.ds(start, tk)], v_buf.at[slot], sems.at[1, slot]))

    for cp in tile_copies(0, 0):
        cp.start()

    @pl.when(n_active > 1)
    def _():
        for cp in tile_copies(1, 1):
            cp.start()

    q = q_ref[0, 0]
    low_half = lax.broadcasted_iota(jnp.int32, (tq, LANES), 1) < HEAD_DIM
    zero = jnp.zeros_like(q)
    qpos = qpos_ref[0]
    for c in range(2):
        qc = jnp.where(low_half == (c == 0), q, zero)
        for var, pos in enumerate((qpos, -qpos, jnp.zeros_like(qpos))):
            qa_scr[c, var, :, :LANES] = qc
            qa_scr[c, var, :, LANES:] = pos
    m_scr[...] = jnp.full(m_scr.shape, NEG_INF, jnp.float32)
    acc_scr[0] = jnp.zeros(acc_scr.shape[1:], jnp.float32)


    def restabilise(slot, kt, diag, cur):
        var = 2 if diag else jnp.where(kt < qi, 0, 1)
        off = -slope * jnp.abs(qi * tq - kt * tk).astype(jnp.float32)
        ka = k_buf[slot]
        vt = v_buf[slot]
        ss = []
        for c in range(2):
            s = _dot_nt(ka, qa_scr[c, var])
            if diag:
                s = s + dabs_ref[...] * slope
            ss.append(s)
        for c in range(2):
            s = ss[c]
            m_old = m_scr[c]
            m_new = jnp.maximum(m_old, jnp.max(s, axis=0, keepdims=True) + off)
            alpha = jnp.exp2(m_old - m_new)
            p = jnp.exp2((s - (m_new - off)).astype(jnp.bfloat16))
            acc_scr[cur, c] = alpha * acc_scr[cur, c] + _dot(vt, p)
            m_scr[c] = m_new

    for cp in tile_copies(0, 0):
        cp.wait()
    restabilise(0, qi, True, 0)

    def visit(j, cur):
        slot = j % 2

        @pl.when(j + 1 < n_active)
        def _():
            for cp in tile_copies(j + 1, 1 - slot):
                cp.start()

        for cp in tile_copies(j, slot):
            cp.wait()
        kt = key_tile(j)
        var = jnp.where(kt < qi, 0, 1)
        off = -slope * jnp.abs(qi * tq - kt * tk).astype(jnp.float32)
        ka = k_buf[slot]
        vt = v_buf[slot]
        excess = None
        for c in range(2):
            s = _dot_nt(ka, qa_scr[c, var])
            m_eff = m_scr[c] - off
            over = jnp.max(jnp.max(s, axis=0, keepdims=True) - m_eff)
            excess = over if excess is None else jnp.maximum(excess, over)
            p = jnp.exp2((s - m_eff).astype(jnp.bfloat16))
            acc_scr[1 - cur, c] = acc_scr[cur, c] + _dot(vt, p)

        @pl.when(excess > EXP2_HEADROOM)
        def _():
            restabilise(slot, kt, False, cur)

        return jnp.where(excess <= EXP2_HEADROOM, 1 - cur, cur)

    cur = lax.fori_loop(1, n_active, visit, jnp.int32(0))

    lam = (jnp.exp(jnp.sum(lq1_ref[...] * lk1_ref[...], axis=-1, keepdims=True))
           - jnp.exp(jnp.sum(lq2_ref[...] * lk2_ref[...], axis=-1, keepdims=True)) + LAM_INIT)
    a1, a2 = acc_scr[cur, 0], acc_scr[cur, 1]
    o = (a1[:DIFF_VDIM] / a1[DIFF_VDIM:DIFF_VDIM + 1]
         - lam * (a2[:DIFF_VDIM] / a2[DIFF_VDIM:DIFF_VDIM + 1]))
    o = o * lax.rsqrt(jnp.mean(o * o, axis=0, keepdims=True) + LN_EPS) * g_ref[...]
    o_ref[0] = (o * (1.0 - LAM_INIT)).T.astype(o_ref.dtype)


def _tile_reach(qn, kn, n_tiles):
    B = qn.shape[0]
    ng = 2 * N_HEADS_DIFF
    qmax = jnp.sqrt(qn[:, 0, :ng]) * NORM_SLACK
    kmax = jnp.sqrt(kn[:, 0, :ng]) * NORM_SLACK
    bound = jnp.max((2.0 * qmax * kmax).reshape(B, N_HEADS_DIFF, 2), axis=-1) + UNDERFLOW_LOG2
    far = bound / jnp.asarray(_diff_slopes_log2())[None, :]
    reach = jnp.floor((far - 1.0) / DIFF_TK) + 1.0
    reach = jnp.where(jnp.isfinite(reach), reach, n_tiles)
    return jnp.clip(reach, 0, n_tiles).astype(jnp.int32).reshape(-1)


def _diff_attention(reach, qd, ka, vt, qpos, dabs, lq1, lk1, lq2, lk2, g_col):
    B, H, S, _ = qd.shape
    tq, tk = DIFF_TQ, DIFF_TK
    slopes = tuple(float(s) for s in _diff_slopes_log2())
    nk = S // tk
    small = lambda a: pl.BlockSpec(a.shape, lambda b, h, qi, reach: (0, 0))
    grid_spec = pltpu.PrefetchScalarGridSpec(
        num_scalar_prefetch=1,
        grid=(B, H, S // tq),
        in_specs=[
            pl.BlockSpec((1, 1, tq, LANES), lambda b, h, qi, reach: (b, h, qi, 0)),
            pl.BlockSpec(memory_space=pl.ANY),
            pl.BlockSpec(memory_space=pl.ANY),
            pl.BlockSpec((1, tq, LANES), lambda b, h, qi, reach: (h, 0, 0)),
            small(dabs), small(lq1), small(lk1), small(lq2), small(lk2), small(g_col),
        ],
        out_specs=pl.BlockSpec((1, tq, LANES), lambda b, h, qi, reach: (b, qi, h)),
        scratch_shapes=[
            pltpu.VMEM((2, 3, tq, 2 * LANES), jnp.bfloat16),
            pltpu.VMEM((2, 1, tq), jnp.float32),
            pltpu.VMEM((2, 2, V_ROWS, tq), jnp.float32),
            pltpu.VMEM((2, tk, 2 * LANES), jnp.bfloat16),
            pltpu.VMEM((2, V_ROWS, tk), jnp.bfloat16),
            pltpu.SemaphoreType.DMA((2, 2)),
        ],
    )
    return pl.pallas_call(
        functools.partial(_diff_kernel, slopes=slopes, n_key_tiles=nk),
        grid_spec=grid_spec,
        out_shape=jax.ShapeDtypeStruct((B, S, H * LANES), jnp.bfloat16),
        compiler_params=_cparams(("parallel", "parallel", "arbitrary")),
        name="diff_attention",
    )(reach, qd, ka, vt, qpos, dabs, lq1, lk1, lq2, lk2, g_col)


def _diff_slopes_log2():
    return np.asarray([2.0 ** (-8.0 * (i + 1) / N_HEADS_DIFF) for i in range(N_HEADS_DIFF)],
                      np.float32) * np.float32(LOG2E)


def _position_columns():
    def bf16_round(v):
        u = np.asarray(v, np.float32).view(np.uint32)
        return ((u + 0x7FFF + ((u >> 16) & 1)) & np.uint32(0xFFFF0000)).view(np.float32)

    kpos = np.zeros((N_HEADS_DIFF, DIFF_TK, LANES), np.float32)
    qpos = np.zeros((N_HEADS_DIFF, DIFF_TQ, LANES), np.float32)
    pk, pq = np.arange(DIFF_TK), np.arange(DIFF_TQ)
    for h, a in enumerate(_diff_slopes_log2()):
        rest = np.float32(a)
        for n in range(3):
            a_n = np.float32(bf16_round(rest))
            rest = np.float32(rest - a_n)
            c = 4 * n
            kpos[h, :, c + 0] = (pk // POS_SPLIT) * POS_SPLIT
            kpos[h, :, c + 1] = pk % POS_SPLIT
            kpos[h, :, c + 2] = a_n
            kpos[h, :, c + 3] = a_n
            qpos[h, :, c + 0] = a_n
            qpos[h, :, c + 1] = a_n
            qpos[h, :, c + 2] = -((pq // POS_SPLIT) * POS_SPLIT)
            qpos[h, :, c + 3] = -(pq % POS_SPLIT)
    dabs = -np.abs(pq[:, None] - pk[None, :]).astype(np.float32)
    return jnp.asarray(kpos, jnp.bfloat16), jnp.asarray(qpos, jnp.bfloat16), jnp.asarray(dabs)


def _post_attn_kernel(o1_ref, l1_ref, o4_ref, l4_ref, o16_ref, l16_ref, od_ref, x_ref, wo_ref, g_ref, b_ref,
                      wr_ref, br_ref, tri_ref,
                      x1_ref, x1t_ref, idx_ref, gate_ref, rank_ref, cnt_ref, carry_scr, order_scr):
    step = pl.program_id(0)

    @pl.when(step == 0)
    def _():
        carry_scr[...] = jnp.zeros(carry_scr.shape, jnp.float32)

    tm = x_ref.shape[0]
    nc = DIL_WIDTH // LANES
    for n, (o_ref, l_ref) in enumerate(((o4_ref, l4_ref), (o16_ref, l16_ref))):
        d = o_ref.shape[1]
        for r in range(d):
            o_r = o_ref[0, r].astype(jnp.float32)
            l_r = l_ref[0, r]
            for c in range(nc):
                cols = slice(c * LANES, (c + 1) * LANES)
                order_scr[2 * n, c, pl.ds(r, tm // d, stride=d), :] = o_r[:, cols]
                order_scr[2 * n + 1, c, pl.ds(r, tm // d, stride=d), :] = l_r[:, cols]
    in_order = lambda n: jnp.concatenate([order_scr[n, c] for c in range(nc)], axis=1)
    outs = (o1_ref[0, 0].astype(jnp.float32), in_order(0), in_order(2))
    lses = (l1_ref[0, 0], in_order(1), in_order(3))
    top = jnp.maximum(jnp.maximum(lses[0], lses[1]), lses[2])
    wts = [jnp.exp(l - top) for l in lses]
    oa = ((wts[0] * outs[0] + wts[1] * outs[1] + wts[2] * outs[2])
          / (wts[0] + wts[1] + wts[2])).astype(jnp.bfloat16)

    mix = _dot(oa, wo_ref[:DIL_WIDTH, :]) + _dot(od_ref[...], wo_ref[DIL_WIDTH:, :])
    x1 = _layer_norm(DEEPNORM_ALPHA * x_ref[...] + mix, g_ref[...], b_ref[...])
    x1_ref[...] = x1
    for s in range(ROW_SUBLANES):
        x1t_ref[pl.ds(s, x1.shape[0], stride=ROW_SUBLANES), :] = x1[:, s * LANES:(s + 1) * LANES]

    tm = x1.shape[0]
    x_hi = x1.astype(jnp.bfloat16)
    x_lo = (x1 - x_hi.astype(jnp.float32)).astype(jnp.bfloat16)
    lg = (_dot_nt(wr_ref[0], x_hi) + _dot_nt(wr_ref[1], x_hi) + _dot_nt(wr_ref[0], x_lo)
          + br_ref[...])
    eidx = lax.broadcasted_iota(jnp.int32, (N_EXPERTS, tm), 0)
    vals, sels = [], []
    for k in range(TOP_K):
        mx = jnp.max(lg, axis=0, keepdims=True)
        idx = jnp.min(jnp.where(lg == mx, eidx, N_EXPERTS), axis=0, keepdims=True)
        sel = eidx == idx
        vals.append(mx)
        sels.append(sel)
        idx_ref[k:k + 1, :] = idx
        lg = jnp.where(sel, -jnp.inf, lg)
    ex = [jnp.exp(v - vals[0]) for v in vals]
    den = ex[0] + ex[1] + ex[2] + ex[3]
    for k in range(TOP_K):
        gate_ref[k:k + 1, :] = ex[k] / den

    chosen = (sels[0] | sels[1] | sels[2] | sels[3])
    onehot = jnp.where(chosen, 1.0, 0.0)
    before = _dot(onehot.astype(jnp.bfloat16), tri_ref[...]) + carry_scr[...]
    for k in range(TOP_K):
        rank_ref[k:k + 1, :] = jnp.sum(jnp.where(sels[k], before, 0.0), axis=0,
                                       keepdims=True).astype(jnp.int32)
    carry_scr[...] = carry_scr[...] + jnp.sum(onehot, axis=1, keepdims=True)
    pad = jnp.zeros((SUBLANES - TOP_K, tm), jnp.int32)
    idx_ref[TOP_K:, :] = pad
    rank_ref[TOP_K:, :] = pad
    gate_ref[TOP_K:, :] = pad.astype(jnp.float32)
    cnt_ref[...] = jnp.broadcast_to(carry_scr[...], cnt_ref.shape).astype(jnp.int32)


def _post_attn(dil, od, x, wo_bf, g, b, wr_t, br, tri):
    T, D = x.shape
    assert D == ROW_SUBLANES * LANES
    tm = ROW_TILE
    nb = dil[0][0].shape[2] // tm
    row = lambda w: pl.BlockSpec((tm, w), lambda i: (i, 0))
    full = lambda a: pl.BlockSpec(a.shape, lambda i: (0,) * a.ndim)
    col = pl.BlockSpec((SUBLANES, tm), lambda i: (0, i))
    dil_specs, dil_args = [], []
    for o_d, lse_d in dil:
        d = o_d.shape[1]
        spec = pl.BlockSpec((1, d, tm // d, DIL_WIDTH), lambda i: (i // nb, 0, i % nb, 0))
        dil_specs += [spec, spec]
        dil_args += [o_d, lse_d]
    return pl.pallas_call(
        _post_attn_kernel,
        grid=(T // tm,),
        in_specs=dil_specs + [row(DIFF_WIDTH), row(D), full(wo_bf), full(g), full(b), full(wr_t),
                              full(br), full(tri)],
        out_specs=[row(D), pl.BlockSpec((tm * ROW_SUBLANES, LANES), lambda i: (i, 0)), col, col, col,
                   pl.BlockSpec((N_EXPERTS, LANES), lambda i: (0, 0))],
        out_shape=[
            jax.ShapeDtypeStruct((T, D), jnp.float32),
            jax.ShapeDtypeStruct((T * ROW_SUBLANES, LANES), jnp.float32),
            jax.ShapeDtypeStruct((SUBLANES, T), jnp.int32),
            jax.ShapeDtypeStruct((SUBLANES, T), jnp.float32),
            jax.ShapeDtypeStruct((SUBLANES, T), jnp.int32),
            jax.ShapeDtypeStruct((N_EXPERTS, LANES), jnp.int32),
        ],
        scratch_shapes=[pltpu.VMEM((N_EXPERTS, 1), jnp.float32),
                        pltpu.VMEM((4, DIL_WIDTH // LANES, tm, LANES), jnp.float32)],
        compiler_params=_cparams(("arbitrary",)),
        name="post_attn_router",
    )(*dil_args, od, x, wo_bf, g, b, wr_t, br, tri)


def _index_copy(dest_hbm, step, idx_smem, sem):
    per_step = dest_hbm.shape[1]
    half = pl.multiple_of((step % 2) * per_step, per_step)
    return pltpu.make_async_copy(dest_hbm.at[step], idx_smem.at[pl.ds(half, per_step)], sem)


def _stage_index(dest_hbm, step, n_steps, idx_smem, sem):
    @pl.when(step == 0)
    def _():
        _index_copy(dest_hbm, 0, idx_smem, sem).start()

    _index_copy(dest_hbm, step, idx_smem, sem).wait()

    @pl.when(step + 1 < n_steps)
    def _():
        _index_copy(dest_hbm, step + 1, idx_smem, sem).start()

    return (step % 2) * dest_hbm.shape[1]


def _row_tile(ref, row):
    return ref.at[pl.ds(pl.multiple_of(row * ROW_SUBLANES, ROW_SUBLANES), ROW_SUBLANES)]


def _rows_from_tiles(ref, n):
    return jnp.concatenate([ref[pl.ds(s, n, stride=ROW_SUBLANES), :] for s in range(ROW_SUBLANES)], axis=1)


def _rows_to_tiles(ref, rows):
    for s in range(ROW_SUBLANES):
        ref[pl.ds(s, rows.shape[0], stride=ROW_SUBLANES), :] = rows[:, s * LANES:(s + 1) * LANES]


def _dispatch_kernel(pad_start_ref, n_pad_ref, n_used_ref, dest_hbm, x_ref, xs_hbm, idx_smem, idx_sem, row_sem,
                     zero_scr, zero_sem):
    tm = x_ref.shape[0] // ROW_SUBLANES
    tile_rows = MOE_TILE * ROW_SUBLANES
    n_tiles = xs_hbm.shape[0] // tile_rows

    def pad_copies(start_not_wait):
        for j in range(N_EXPERTS):
            tile = n_used_ref[0] + j

            @pl.when(tile < n_tiles)
            def _():
                dst = xs_hbm.at[pl.ds(pl.multiple_of(tile * tile_rows, tile_rows), tile_rows)]
                cp = pltpu.make_async_copy(zero_scr, dst, zero_sem)
                cp.start() if start_not_wait else cp.wait()

        for e in range(N_EXPERTS):
            n_pad = n_pad_ref[e]
            for bit in reversed(range(PAD_BITS)):
                size = 1 << bit

                @pl.when((n_pad & size) != 0)
                def _():
                    row = pad_start_ref[e] + (n_pad & ~(2 * size - 1))
                    dst = xs_hbm.at[pl.ds(pl.multiple_of(row * ROW_SUBLANES, ROW_SUBLANES), size * ROW_SUBLANES)]
                    cp = pltpu.make_async_copy(zero_scr.at[pl.ds(0, size * ROW_SUBLANES)], dst, zero_sem)
                    cp.start() if start_not_wait else cp.wait()

    @pl.when(pl.program_id(0) == 0)
    def _():
        zero_scr[...] = jnp.zeros(zero_scr.shape, zero_scr.dtype)
        pad_copies(True)
        pad_copies(False)

    base = _stage_index(dest_hbm, pl.program_id(0), pl.num_programs(0), idx_smem, idx_sem)

    def issue(t, carry):
        for k in range(TOP_K):
            pltpu.make_async_copy(_row_tile(x_ref, t), _row_tile(xs_hbm, idx_smem[base + k * tm + t]),
                                  row_sem).start(priority=k % 2)
        return carry

    lax.fori_loop(0, tm, issue, 0, unroll=8)
    for k in range(TOP_K):
        pltpu.make_async_copy(x_ref, xs_hbm.at[pl.ds(0, tm * ROW_SUBLANES)], row_sem).wait()


def _dispatch(pad_start, n_pad, n_used, dest_steps, x1t, n_rows):
    n_steps, per_step = dest_steps.shape
    tm = per_step // TOP_K
    assert MOE_TILE == 1 << PAD_BITS
    grid_spec = pltpu.PrefetchScalarGridSpec(
        num_scalar_prefetch=3,
        grid=(n_steps,),
        in_specs=[
            pl.BlockSpec(memory_space=pl.ANY),
            pl.BlockSpec((tm * ROW_SUBLANES, LANES), lambda i, ps, npad, nu: (i, 0)),
        ],
        out_specs=pl.BlockSpec(memory_space=pl.ANY),
        scratch_shapes=[pltpu.SMEM((2 * per_step,), jnp.int32), pltpu.SemaphoreType.DMA(()),
                        pltpu.SemaphoreType.DMA(()),
                        pltpu.VMEM((MOE_TILE * ROW_SUBLANES, LANES), jnp.float32),
                        pltpu.SemaphoreType.DMA(())],
    )
    return pl.pallas_call(
        _dispatch_kernel,
        grid_spec=grid_spec,
        out_shape=jax.ShapeDtypeStruct((n_rows * ROW_SUBLANES, LANES), jnp.float32),
        compiler_params=_cparams(("arbitrary",)),
        name="dispatch_rows",
    )(pad_start, n_pad, n_used, dest_steps, x1t)


def _expert_kernel(tile_e_ref, n_used_ref, xs_ref, wu_ref, bu_ref, wd_ref, bd_ref, y_ref, wu_scr, wd_scr):
    i = pl.program_id(0)
    used = i < n_used_ref[0]
    new_expert = (i == 0) | (tile_e_ref[i] != tile_e_ref[jnp.maximum(i - 1, 0)])
    tm = xs_ref.shape[0] // ROW_SUBLANES

    @pl.when(used & new_expert)
    def _():
        wu_scr[...] = wu_ref[0].astype(jnp.bfloat16)
        wd_scr[...] = wd_ref[0].astype(jnp.bfloat16)

    @pl.when(used)
    def _():
        xs = _rows_from_tiles(xs_ref, tm).astype(jnp.bfloat16)
        hu = _dot(xs, wu_scr[...]) + bu_ref[0]
        g = jnp.minimum(hu[:, :D_FF], SWIGLU_LIMIT)
        u = jnp.clip(hu[:, D_FF:], -SWIGLU_LIMIT, SWIGLU_LIMIT)
        act = g * (1.0 / (1.0 + jnp.exp(-SWIGLU_ALPHA * g))) * (u + 1.0)
        _rows_to_tiles(y_ref, _dot(act.astype(jnp.bfloat16), wd_scr[...]) + bd_ref[0])

    @pl.when(jnp.logical_not(used))
    def _():
        y_ref[...] = jnp.zeros(y_ref.shape, y_ref.dtype)


def _experts(tile_e, n_used, xs, wu, bu, wd, bd):
    D = ROW_SUBLANES * LANES
    n_rows = xs.shape[0] // ROW_SUBLANES
    tm = MOE_TILE
    tile_spec = pl.BlockSpec((tm * ROW_SUBLANES, LANES), lambda i, te, nu: (i, 0))
    grid_spec = pltpu.PrefetchScalarGridSpec(
        num_scalar_prefetch=2,
        grid=(n_rows // tm,),
        in_specs=[
            pl.BlockSpec((tm * ROW_SUBLANES, LANES), lambda i, te, nu: (jnp.minimum(i, nu[0] - 1), 0)),
            pl.BlockSpec((1, D, 2 * D_FF), lambda i, te, nu: (te[i], 0, 0)),
            pl.BlockSpec((1, 1, 2 * D_FF), lambda i, te, nu: (te[i], 0, 0)),
            pl.BlockSpec((1, D_FF, D), lambda i, te, nu: (te[i], 0, 0)),
            pl.BlockSpec((1, 1, D), lambda i, te, nu: (te[i], 0, 0)),
        ],
        out_specs=tile_spec,
        scratch_shapes=[pltpu.VMEM((D, 2 * D_FF), jnp.bfloat16), pltpu.VMEM((D_FF, D), jnp.bfloat16)],
    )
    return pl.pallas_call(
        _expert_kernel,
        grid_spec=grid_spec,
        out_shape=jax.ShapeDtypeStruct(xs.shape, jnp.float32),
        compiler_params=_cparams(("arbitrary",)),
        name="experts",
    )(tile_e, n_used, xs, wu, bu, wd, bd)


def _combine_kernel(dest_hbm, y_hbm, x1_ref, gate_ref, g_ref, b_ref, o_ref, buf, idx_smem, idx_sem, row_sems):
    i = pl.program_id(0)
    n = pl.num_programs(0)
    tm = x1_ref.shape[0]

    def issue(step, slot):
        base = _stage_index(dest_hbm, step, n, idx_smem, idx_sem)

        def body(t, carry):
            for k in range(TOP_K):
                pltpu.make_async_copy(_row_tile(y_hbm, idx_smem[base + k * tm + t]),
                                      _row_tile(buf.at[slot, k], t), row_sems.at[slot]).start(priority=k % 2)
            return carry

        lax.fori_loop(0, tm, body, 0, unroll=8)

    @pl.when(i == 0)
    def _():
        issue(i, 0)

    @pl.when(i + 1 < n)
    def _():
        issue(i + 1, (i + 1) % 2)

    slot = i % 2
    for k in range(TOP_K):
        pltpu.make_async_copy(y_hbm.at[pl.ds(0, tm * ROW_SUBLANES)], buf.at[slot, k], row_sems.at[slot]).wait()
    ffn = _rows_from_tiles(buf.at[slot, 0], tm) * gate_ref[:, 0:1]
    for k in range(1, TOP_K):
        ffn = ffn + _rows_from_tiles(buf.at[slot, k], tm) * gate_ref[:, k:k + 1]
    o_ref[...] = _layer_norm(DEEPNORM_ALPHA * x1_ref[...] + ffn, g_ref[...], b_ref[...])


def _combine(dest_steps, y, x1, gates_t, g, b):
    T, D = x1.shape
    n_steps, per_step = dest_steps.shape
    tm = per_step // TOP_K
    full = lambda a: pl.BlockSpec(a.shape, lambda i: (0, 0))
    return pl.pallas_call(
        _combine_kernel,
        grid=(n_steps,),
        in_specs=[
            pl.BlockSpec(memory_space=pl.ANY),
            pl.BlockSpec(memory_space=pl.ANY),
            pl.BlockSpec((tm, D), lambda i: (i, 0)),
            pl.BlockSpec((tm, SUBLANES), lambda i: (i, 0)),
            full(g), full(b),
        ],
        out_specs=pl.BlockSpec((tm, D), lambda i: (i, 0)),
        out_shape=jax.ShapeDtypeStruct((T, D), jnp.float32),
        scratch_shapes=[pltpu.VMEM((2, TOP_K, tm * ROW_SUBLANES, LANES), jnp.float32),
                        pltpu.SMEM((2 * per_step,), jnp.int32),
                        pltpu.SemaphoreType.DMA(()), pltpu.SemaphoreType.DMA((2,))],
        compiler_params=_cparams(("arbitrary",)),
        name="combine_ln2",
    )(dest_steps, y, x1, gates_t, g, b)


def kernel(x, w_in, w_out, lambda_q1, lambda_k1, lambda_q2, lambda_k2, diff_norm_g, ln1_g, ln1_b,
           w_router, b_router, w_up, b_up, w_down, b_down, ln2_g, ln2_b):
    B, S, D = x.shape
    T = B * S
    assert w_in.shape[0] == 1, "single layer"
    assert S % DIFF_TQ == 0 and DIFF_TK == DIFF_TQ and DIFF_TK % ROW_TILE == 0 and S % (16 * DIL_TQ) == 0

    na = 3 * DIL_WIDTH
    colscale = np.ones((3 * D,), np.float32)
    colscale[:DIL_WIDTH] = HEAD_DIM ** -0.5
    colscale[na:na + DIFF_WIDTH] = HEAD_DIM ** -0.5 * LOG2E
    w_in_bf = (w_in[0] * colscale).astype(jnp.bfloat16)
    nv = na + 2 * DIFF_WIDTH

    kpos, qpos, dabs = _position_columns()
    pa, pa4, pa16, qd, ka, vt, qn, kn = _in_proj(x, w_in_bf[:, :nv], w_in_bf[:, nv:].T, kpos)
    dil = [_dilated_stage(p, d) for p, (_, d) in zip((pa[:, None], pa4, pa16), DIL_PATTERNS)]

    r2 = lambda a: a.reshape(1, -1).astype(jnp.float32)
    od = _diff_attention(_tile_reach(qn, kn, S // DIFF_TK), qd, ka, vt, qpos, dabs, r2(lambda_q1[0]), r2(lambda_k1[0]), r2(lambda_q2[0]),
                         r2(lambda_k2[0]), diff_norm_g[0].reshape(-1, 1).astype(jnp.float32))

    tri = jnp.asarray(np.triu(np.ones((ROW_TILE, ROW_TILE), np.float32), k=1), jnp.bfloat16)
    n_rows = T * TOP_K + N_EXPERTS * MOE_TILE
    n_tiles = n_rows // MOE_TILE
    wr_t = w_router[0].T.astype(jnp.float32)
    wr_hi = wr_t.astype(jnp.bfloat16)
    wr_split = jnp.stack([wr_hi, (wr_t - wr_hi.astype(jnp.float32)).astype(jnp.bfloat16)])
    x1, x1t, top_idx, gates, rank, counts = _post_attn(
        dil, od.reshape(T, DIFF_WIDTH), x.reshape(T, D), w_out[0].astype(jnp.bfloat16),
        r2(ln1_g[0]), r2(ln1_b[0]), wr_split, b_router[0].reshape(N_EXPERTS, 1), tri)

    counts = counts[:, 0]
    pcounts = ((counts + MOE_TILE - 1) // MOE_TILE) * MOE_TILE
    pends = jnp.cumsum(pcounts)
    pstarts = pends - pcounts
    dest = rank[:TOP_K]
    for e in range(N_EXPERTS):
        dest = dest + jnp.where(top_idx[:TOP_K] == e, pstarts[e], 0)
    tile_starts = jnp.arange(n_tiles, dtype=jnp.int32) * MOE_TILE
    tile_e = jnp.minimum(jnp.sum(pends[None, :] <= tile_starts[:, None], axis=1), N_EXPERTS - 1).astype(jnp.int32)
    n_used = (pends[-1] // MOE_TILE).astype(jnp.int32).reshape(1)

    def per_step(tm):
        return dest.reshape(TOP_K, T // tm, tm).transpose(1, 0, 2).reshape(T // tm, TOP_K * tm)

    xs = _dispatch((pstarts + counts).astype(jnp.int32), (pcounts - counts).astype(jnp.int32), n_used,
                   per_step(ROW_TILE), x1t, n_rows)
    y = _experts(tile_e, n_used, xs, w_up[0], b_up[0][:, None, :], w_down[0], b_down[0][:, None, :])
    out = _combine(per_step(COMBINE_TILE), y, x1, gates.T, r2(ln2_g[0]), r2(ln2_b[0]))
    return out.reshape(B, S, D)
```

```python
import functools
import math

import jax
import jax.numpy as jnp
import numpy as np
from jax import lax
from jax.experimental import pallas as pl
from jax.experimental.pallas import tpu as pltpu

D_MODEL = 1024
HEAD_DIM = 64
DIL_WIDTH = 512
N_HEADS_DIL = 8
DIL_PATTERNS = ((128, 1), (512, 4), (2048, 16))
DIL_SIDE = 64
DIFF_WIDTH = 512
N_HEADS_DIFF = 4
DIFF_VDIM = 2 * HEAD_DIM
N_EXPERTS = 32
TOP_K = 4
D_FF = D_MODEL
SWIGLU_ALPHA = 1.702
SWIGLU_LIMIT = 7.0
DEEPNORM_ALPHA = 2.0 ** 0.25
LN_EPS = 1e-5
NEG_INF = -1e30
LAM_INIT = 0.8 - 0.6 * math.exp(-0.3 * 0)
LOG2E = math.log2(math.e)

LANES = 128
SUBLANES = 8
BF16_SUBLANES = 16
VMEM_LIMIT = 56 * 1024 * 1024

ROW_TILE = 512
DIFF_TQ = 1024
DIFF_TK = 1024
V_ROWS = DIFF_VDIM + BF16_SUBLANES
EXP2_HEADROOM = 4.0
UNDERFLOW_LOG2 = 160.0
NORM_SLACK = 1.01
DIL_TQ = 128
DIL_QB = 8
MOE_TILE = 512
COMBINE_TILE = 512
ROW_SUBLANES = D_MODEL // LANES
PAD_BITS = MOE_TILE.bit_length() - 1
POS_SPLIT = 16


def _cparams(sem):
    return pltpu.CompilerParams(dimension_semantics=sem, vmem_limit_bytes=VMEM_LIMIT)


def _dot_nt(a, b, **kw):
    return lax.dot_general(a, b, (((1,), (1,)), ((), ())), preferred_element_type=jnp.float32, **kw)


def _dot(a, b, **kw):
    return jnp.dot(a, b, preferred_element_type=jnp.float32, **kw)


def _layer_norm(y, g, b):
    mu = jnp.mean(y, axis=-1, keepdims=True)
    yc = y - mu
    var = jnp.mean(yc * yc, axis=-1, keepdims=True)
    return yc * lax.rsqrt(var + LN_EPS) * g + b


def _in_proj_kernel(x_ref, w_ref, wvt_ref, kpos_ref, grp_ref, pa_ref, pa4_ref, pa16_ref, qd_ref, ka_ref,
                    vt_ref, qn_ref, kn_ref, pa_scr):
    xb = x_ref[0].astype(jnp.bfloat16)
    na = 3 * DIL_WIDTH
    tm = xb.shape[0]
    paf = _dot(xb, w_ref[:, :na])
    pa_ref[0] = paf.astype(jnp.bfloat16)
    for c in range(na // LANES):
        cols = slice(c * LANES, (c + 1) * LANES)
        pa_scr[c] = paf[:, cols]
        for d, ref in ((DIL_PATTERNS[1][1], pa4_ref), (DIL_PATTERNS[2][1], pa16_ref)):
            for r in range(d):
                ref[0, r, :, cols] = pa_scr[c, pl.ds(r, tm // d, stride=d), :].astype(jnp.bfloat16)
    q = _dot(xb, w_ref[:, na:na + DIFF_WIDTH]).astype(jnp.bfloat16)
    k = _dot(xb, w_ref[:, na + DIFF_WIDTH:]).astype(jnp.bfloat16)
    vt = _dot_nt(wvt_ref[...], xb).astype(jnp.bfloat16)
    ones = jnp.ones((V_ROWS - DIFF_VDIM, tm), jnp.bfloat16)

    @pl.when(pl.program_id(1) == 0)
    def _():
        qn_ref[...] = jnp.zeros(qn_ref.shape, jnp.float32)
        kn_ref[...] = jnp.zeros(kn_ref.shape, jnp.float32)

    for src, dst in ((q, qn_ref), (k, kn_ref)):
        f = src.astype(jnp.float32)
        gs = _dot((f * f).astype(jnp.bfloat16), grp_ref[...])
        dst[0] = jnp.maximum(dst[0], jnp.max(gs, axis=0, keepdims=True))
    for h in range(N_HEADS_DIFF):
        sl = slice(h * LANES, (h + 1) * LANES)
        qd_ref[0, h] = q[:, sl]
        ka_ref[0, h, :, :LANES] = k[:, sl]
        ka_ref[0, h, :, LANES:] = kpos_ref[h]
        vt_ref[0, h, :DIFF_VDIM, :] = vt[sl, :]
        vt_ref[0, h, DIFF_VDIM:, :] = ones


def _in_proj(x, w_bf, wvt_bf, kpos):
    B, S, D = x.shape
    tm = ROW_TILE
    na = 3 * DIL_WIDTH
    grp = np.zeros((DIFF_WIDTH, LANES), np.float32)
    grp[np.arange(DIFF_WIDTH), np.arange(DIFF_WIDTH) // HEAD_DIM] = 1.0
    grp = jnp.asarray(grp, jnp.bfloat16)
    norm_spec = pl.BlockSpec((1, 1, LANES), lambda b, i: (b, 0, 0))
    d4, d16 = DIL_PATTERNS[1][1], DIL_PATTERNS[2][1]
    return pl.pallas_call(
        _in_proj_kernel,
        grid=(B, S // tm),
        in_specs=[
            pl.BlockSpec((1, tm, D), lambda b, i: (b, i, 0)),
            pl.BlockSpec(w_bf.shape, lambda b, i: (0, 0)),
            pl.BlockSpec(wvt_bf.shape, lambda b, i: (0, 0)),
            pl.BlockSpec((N_HEADS_DIFF, tm, LANES), lambda b, i: (0, i % (DIFF_TK // tm), 0)),
            pl.BlockSpec(grp.shape, lambda b, i: (0, 0)),
        ],
        out_specs=[
            pl.BlockSpec((1, tm, na), lambda b, i: (b, i, 0)),
            pl.BlockSpec((1, d4, tm // d4, na), lambda b, i: (b, 0, i, 0)),
            pl.BlockSpec((1, d16, tm // d16, na), lambda b, i: (b, 0, i, 0)),
            pl.BlockSpec((1, N_HEADS_DIFF, tm, LANES), lambda b, i: (b, 0, i, 0)),
            pl.BlockSpec((1, N_HEADS_DIFF, tm, 2 * LANES), lambda b, i: (b, 0, i, 0)),
            pl.BlockSpec((1, N_HEADS_DIFF, V_ROWS, tm), lambda b, i: (b, 0, 0, i)),
            norm_spec, norm_spec,
        ],
        out_shape=[
            jax.ShapeDtypeStruct((B, S, na), jnp.bfloat16),
            jax.ShapeDtypeStruct((B, d4, S // d4, na), jnp.bfloat16),
            jax.ShapeDtypeStruct((B, d16, S // d16, na), jnp.bfloat16),
            jax.ShapeDtypeStruct((B, N_HEADS_DIFF, S, LANES), jnp.bfloat16),
            jax.ShapeDtypeStruct((B, N_HEADS_DIFF, S, 2 * LANES), jnp.bfloat16),
            jax.ShapeDtypeStruct((B, N_HEADS_DIFF, V_ROWS, S), jnp.bfloat16),
            jax.ShapeDtypeStruct((B, 1, LANES), jnp.float32),
            jax.ShapeDtypeStruct((B, 1, LANES), jnp.float32),
        ],
        scratch_shapes=[pltpu.VMEM((na // LANES, tm, LANES), jnp.float32)],
        compiler_params=_cparams(("parallel", "arbitrary")),
        name="in_proj",
    )(x, w_bf, wvt_bf, kpos, grp)


def _dilated_kernel(*refs, seq_len):
    n_halo = 2 * DIL_QB + 2
    q_ref, k_refs, v_refs = refs[0], refs[1:1 + n_halo], refs[1 + n_halo:1 + 2 * n_halo]
    bias_ref, o_ref, lse_ref = refs[1 + 2 * n_halo:]
    tq = DIL_TQ
    nk = tq + 2 * DIL_SIDE
    i = pl.program_id(2)
    q = q_ref[0, 0]
    kwin = jnp.concatenate([r[0, 0] for r in k_refs], axis=0)
    vwin = jnp.concatenate([r[0, 0] for r in v_refs], axis=0)
    low_half = lax.broadcasted_iota(jnp.int32, (tq, LANES), 1) < HEAD_DIM
    ss = []
    for u in range(DIL_QB):
        key_pos = (i * DIL_QB + u) * tq - DIL_SIDE + lax.broadcasted_iota(jnp.int32, (1, nk), 1)
        edge = jnp.where((key_pos >= 0) & (key_pos < seq_len), 0.0, NEG_INF)
        for h in range(N_HEADS_DIL):
            sl = slice((h // 2) * LANES, (h // 2 + 1) * LANES)
            qu = q[u * tq:(u + 1) * tq, sl]
            qm = jnp.where(low_half == (h % 2 == 0), qu, jnp.zeros((tq, LANES), q.dtype))
            ss.append(_dot_nt(qm, kwin[u * tq:u * tq + nk, sl]) + bias_ref[h] + edge)
    s = jnp.concatenate(ss, axis=0)
    m = jnp.max(s, axis=-1, keepdims=True)
    p = jnp.exp(s - m)
    l = jnp.sum(p, axis=-1, keepdims=True)
    lse = m + jnp.log(l)
    inv_l = 1.0 / l
    pb = p.astype(jnp.bfloat16)
    for u in range(DIL_QB):
        for pair in range(N_HEADS_DIL // 2):
            sl = slice(pair * LANES, (pair + 1) * LANES)
            base = (u * N_HEADS_DIL + 2 * pair) * tq
            r0 = slice(base, base + tq)
            r1 = slice(base + tq, base + 2 * tq)
            vu = vwin[u * tq:u * tq + nk, sl]
            a0 = _dot(pb[r0], vu) * inv_l[r0]
            a1 = _dot(pb[r1], vu) * inv_l[r1]
            rows = slice(u * tq, (u + 1) * tq)
            o_ref[0, 0, rows, sl] = jnp.where(low_half, a0, a1).astype(o_ref.dtype)
            lse_ref[0, 0, rows, sl] = jnp.where(low_half, lse[r0], lse[r1])


def _dilated_stage(pa_d, dilation):
    B, d, L, _ = pa_d.shape
    tb = DIL_QB * DIL_TQ
    w = DIL_WIDTH
    halo = DIL_SIDE
    n_halo = L // halo
    per_step = tb // halo + 2
    bias = _dilated_bias(dilation)
    assert L % tb == 0

    def halo_spec(col, j):
        def imap(b, r, i):
            return (b, r, jnp.clip(i * (tb // halo) - 1 + j, 0, n_halo - 1), col)
        return pl.BlockSpec((1, 1, halo, w), imap)

    out_spec = pl.BlockSpec((1, 1, tb, w), lambda b, r, i: (b, r, i, 0))
    in_specs = ([pl.BlockSpec((1, 1, tb, w), lambda b, r, i: (b, r, i, 0))]
                + [halo_spec(1, j) for j in range(per_step)] + [halo_spec(2, j) for j in range(per_step)]
                + [pl.BlockSpec(bias.shape, lambda b, r, i: (0, 0, 0))])
    return pl.pallas_call(
        functools.partial(_dilated_kernel, seq_len=L),
        grid=(B, d, L // tb),
        in_specs=in_specs,
        out_specs=[out_spec, out_spec],
        out_shape=[jax.ShapeDtypeStruct((B, d, L, w), jnp.bfloat16),
                   jax.ShapeDtypeStruct((B, d, L, w), jnp.float32)],
        compiler_params=_cparams(("parallel", "parallel", "parallel")),
        name=f"dilated_d{d}",
    )(*([pa_d] * (1 + 2 * per_step)), bias)


def _dilated_bias(dilation):
    tq = DIL_TQ
    slopes = np.asarray([2.0 ** (-8.0 * (i + 1) / N_HEADS_DIL) for i in range(N_HEADS_DIL)], np.float32)
    rel = (np.arange(tq + 2 * DIL_SIDE)[None, :] - DIL_SIDE) - np.arange(tq)[:, None]
    band = np.abs(rel) <= DIL_SIDE
    pen = -(slopes * dilation)[:, None, None] * np.abs(rel).astype(np.float32)[None]
    return jnp.asarray(np.where(band[None], pen, np.float32(NEG_INF)), jnp.float32)


def _diff_kernel(reach_ref, q_ref, ka_hbm, vt_hbm, qpos_ref, dabs_ref, lq1_ref, lk1_ref, lq2_ref, lk2_ref,
                 g_ref, o_ref, qa_scr, m_scr, acc_scr, k_buf, v_buf, sems, *, slopes, n_key_tiles):
    b = pl.program_id(0)
    h = pl.program_id(1)
    qi = pl.program_id(2)
    tq = q_ref.shape[2]
    tk = k_buf.shape[1]

    slope = jnp.float32(slopes[0])
    for hh in range(1, N_HEADS_DIFF):
        slope = jnp.where(h == hh, jnp.float32(slopes[hh]), slope)

    reach = reach_ref[b * N_HEADS_DIFF + h]
    first = jnp.maximum(qi - reach, 0)
    n_active = jnp.minimum(qi + reach, n_key_tiles - 1) - first + 1

    def key_tile(j):
        t = first + j - 1
        return jnp.where(j == 0, qi, jnp.where(t < qi, t, t + 1))

    def tile_copies(j, slot):
        start = pl.multiple_of(key_tile(j) * tk, tk)
        return (pltpu.make_async_copy(ka_hbm.at[b, h, pl.ds(start, tk), :], k_buf.at[slot], sems.at[0, slot]),
                pltpu.make_async_copy(vt_hbm.at[b, h, :, pl.ds(start, tk)], v_buf.at[slot], sems.at[1, slot]))

    for cp in tile_copies(0, 0):
        cp.start()

    @pl.when(n_active > 1)
    def _():
        for cp in tile_copies(1, 1):
            cp.start()

    q = q_ref[0, 0]
    low_half = lax.broadcasted_iota(jnp.int32, (tq, LANES), 1) < HEAD_DIM
    zero = jnp.zeros_like(q)
    qpos = qpos_ref[0]
    for c in range(2):
        qc = jnp.where(low_half == (c == 0), q, zero)
        for var, pos in enumerate((qpos, -qpos, jnp.zeros_like(qpos))):
            qa_scr[c, var, :, :LANES] = qc
            qa_scr[c, var, :, LANES:] = pos
    m_scr[...] = jnp.full(m_scr.shape, NEG_INF, jnp.float32)
    acc_scr[0] = jnp.zeros(acc_scr.shape[1:], jnp.float32)


    def restabilise(slot, kt, diag, cur):
        var = 2 if diag else jnp.where(kt < qi, 0, 1)
        off = -slope * jnp.abs(qi * tq - kt * tk).astype(jnp.float32)
        ka = k_buf[slot]
        vt = v_buf[slot]
        ss = []
        for c in range(2):
            s = _dot_nt(ka, qa_scr[c, var])
            if diag:
                s = s + dabs_ref[...] * slope
            ss.append(s)
        for c in range(2):
            s = ss[c]
            m_old = m_scr[c]
            m_new = jnp.maximum(m_old, jnp.max(s, axis=0, keepdims=True) + off)
            alpha = jnp.exp2(m_old - m_new)
            p = jnp.exp2((s - (m_new - off)).astype(jnp.bfloat16))
            acc_scr[cur, c] = alpha * acc_scr[cur, c] + _dot(vt, p)
            m_scr[c] = m_new

    for cp in tile_copies(0, 0):
        cp.wait()
    restabilise(0, qi, True, 0)

    def visit(j, cur):
        slot = j % 2

        @pl.when(j + 1 < n_active)
        def _():
            for cp in tile_copies(j + 1, 1 - slot):
                cp.start()

        for cp in tile_copies(j, slot):
            cp.wait()
        kt = key_tile(j)
        var = jnp.where(kt < qi, 0, 1)
        off = -slope * jnp.abs(qi * tq - kt * tk).astype(jnp.float32)
        ka = k_buf[slot]
        vt = v_buf[slot]
        excess = None
        for c in range(2):
            s = _dot_nt(ka, qa_scr[c, var])
            m_eff = m_scr[c] - off
            over = jnp.max(jnp.max(s, axis=0, keepdims=True) - m_eff)
            excess = over if excess is None else jnp.maximum(excess, over)
            p = jnp.exp2((s - m_eff).astype(jnp.bfloat16))
            acc_scr[1 - cur, c] = acc_scr[cur, c] + _dot(vt, p)

        @pl.when(excess > EXP2_HEADROOM)
        def _():
            restabilise(slot, kt, False, cur)

        return jnp.where(excess <= EXP2_HEADROOM, 1 - cur, cur)

    cur = lax.fori_loop(1, n_active, visit, jnp.int32(0))

    lam = (jnp.exp(jnp.sum(lq1_ref[...] * lk1_ref[...], axis=-1, keepdims=True))
           - jnp.exp(jnp.sum(lq2_ref[...] * lk2_ref[...], axis=-1, keepdims=True)) + LAM_INIT)
    a1, a2 = acc_scr[cur, 0], acc_scr[cur, 1]
    o = (a1[:DIFF_VDIM] / a1[DIFF_VDIM:DIFF_VDIM + 1]
         - lam * (a2[:DIFF_VDIM] / a2[DIFF_VDIM:DIFF_VDIM + 1]))
    o = o * lax.rsqrt(jnp.mean(o * o, axis=0, keepdims=True) + LN_EPS) * g_ref[...]
    o_ref[0] = (o * (1.0 - LAM_INIT)).T.astype(o_ref.dtype)


def _tile_reach(qn, kn, n_tiles):
    B = qn.shape[0]
    ng = 2 * N_HEADS_DIFF
    qmax = jnp.sqrt(qn[:, 0, :ng]) * NORM_SLACK
    kmax = jnp.sqrt(kn[:, 0, :ng]) * NORM_SLACK
    bound = jnp.max((2.0 * qmax * kmax).reshape(B, N_HEADS_DIFF, 2), axis=-1) + UNDERFLOW_LOG2
    far = bound / jnp.asarray(_diff_slopes_log2())[None, :]
    reach = jnp.floor((far - 1.0) / DIFF_TK) + 1.0
    reach = jnp.where(jnp.isfinite(reach), reach, n_tiles)
    return jnp.clip(reach, 0, n_tiles).astype(jnp.int32).reshape(-1)


def _diff_attention(reach, qd, ka, vt, qpos, dabs, lq1, lk1, lq2, lk2, g_col):
    B, H, S, _ = qd.shape
    tq, tk = DIFF_TQ, DIFF_TK
    slopes = tuple(float(s) for s in _diff_slopes_log2())
    nk = S // tk
    small = lambda a: pl.BlockSpec(a.shape, lambda b, h, qi, reach: (0, 0))
    grid_spec = pltpu.PrefetchScalarGridSpec(
        num_scalar_prefetch=1,
        grid=(B, H, S // tq),
        in_specs=[
            pl.BlockSpec((1, 1, tq, LANES), lambda b, h, qi, reach: (b, h, qi, 0)),
            pl.BlockSpec(memory_space=pl.ANY),
            pl.BlockSpec(memory_space=pl.ANY),
            pl.BlockSpec((1, tq, LANES), lambda b, h, qi, reach: (h, 0, 0)),
            small(dabs), small(lq1), small(lk1), small(lq2), small(lk2), small(g_col),
        ],
        out_specs=pl.BlockSpec((1, tq, LANES), lambda b, h, qi, reach: (b, qi, h)),
        scratch_shapes=[
            pltpu.VMEM((2, 3, tq, 2 * LANES), jnp.bfloat16),
            pltpu.VMEM((2, 1, tq), jnp.float32),
            pltpu.VMEM((2, 2, V_ROWS, tq), jnp.float32),
            pltpu.VMEM((2, tk, 2 * LANES), jnp.bfloat16),
            pltpu.VMEM((2, V_ROWS, tk), jnp.bfloat16),
            pltpu.SemaphoreType.DMA((2, 2)),
        ],
    )
    return pl.pallas_call(
        functools.partial(_diff_kernel, slopes=slopes, n_key_tiles=nk),
        grid_spec=grid_spec,
        out_shape=jax.ShapeDtypeStruct((B, S, H * LANES), jnp.bfloat16),
        compiler_params=_cparams(("parallel", "parallel", "arbitrary")),
        name="diff_attention",
    )(reach, qd, ka, vt, qpos, dabs, lq1, lk1, lq2, lk2, g_col)


def _diff_slopes_log2():
    return np.asarray([2.0 ** (-8.0 * (i + 1) / N_HEADS_DIFF) for i in range(N_HEADS_DIFF)],
                      np.float32) * np.float32(LOG2E)


def _position_columns():
    def bf16_round(v):
        u = np.asarray(v, np.float32).view(np.uint32)
        return ((u + 0x7FFF + ((u >> 16) & 1)) & np.uint32(0xFFFF0000)).view(np.float32)

    kpos = np.zeros((N_HEADS_DIFF, DIFF_TK, LANES), np.float32)
    qpos = np.zeros((N_HEADS_DIFF, DIFF_TQ, LANES), np.float32)
    pk, pq = np.arange(DIFF_TK), np.arange(DIFF_TQ)
    for h, a in enumerate(_diff_slopes_log2()):
        rest = np.float32(a)
        for n in range(3):
            a_n = np.float32(bf16_round(rest))
            rest = np.float32(rest - a_n)
            c = 4 * n
            kpos[h, :, c + 0] = (pk // POS_SPLIT) * POS_SPLIT
            kpos[h, :, c + 1] = pk % POS_SPLIT
            kpos[h, :, c + 2] = a_n
            kpos[h, :, c + 3] = a_n
            qpos[h, :, c + 0] = a_n
            qpos[h, :, c + 1] = a_n
            qpos[h, :, c + 2] = -((pq // POS_SPLIT) * POS_SPLIT)
            qpos[h, :, c + 3] = -(pq % POS_SPLIT)
    dabs = -np.abs(pq[:, None] - pk[None, :]).astype(np.float32)
    return jnp.asarray(kpos, jnp.bfloat16), jnp.asarray(qpos, jnp.bfloat16), jnp.asarray(dabs)


def _post_attn_kernel(o1_ref, l1_ref, o4_ref, l4_ref, o16_ref, l16_ref, od_ref, x_ref, wo_ref, g_ref, b_ref,
                      wr_ref, br_ref, tri_ref,
                      x1_ref, x1t_ref, idx_ref, gate_ref, rank_ref, cnt_ref, carry_scr, order_scr):
    step = pl.program_id(0)

    @pl.when(step == 0)
    def _():
        carry_scr[...] = jnp.zeros(carry_scr.shape, jnp.float32)

    tm = x_ref.shape[0]
    nc = DIL_WIDTH // LANES
    for n, (o_ref, l_ref) in enumerate(((o4_ref, l4_ref), (o16_ref, l16_ref))):
        d = o_ref.shape[1]
        for r in range(d):
            o_r = o_ref[0, r].astype(jnp.float32)
            l_r = l_ref[0, r]
            for c in range(nc):
                cols = slice(c * LANES, (c + 1) * LANES)
                order_scr[2 * n, c, pl.ds(r, tm // d, stride=d), :] = o_r[:, cols]
                order_scr[2 * n + 1, c, pl.ds(r, tm // d, stride=d), :] = l_r[:, cols]
    in_order = lambda n: jnp.concatenate([order_scr[n, c] for c in range(nc)], axis=1)
    outs = (o1_ref[0, 0].astype(jnp.float32), in_order(0), in_order(2))
    lses = (l1_ref[0, 0], in_order(1), in_order(3))
    top = jnp.maximum(jnp.maximum(lses[0], lses[1]), lses[2])
    wts = [jnp.exp(l - top) for l in lses]
    oa = ((wts[0] * outs[0] + wts[1] * outs[1] + wts[2] * outs[2])
          / (wts[0] + wts[1] + wts[2])).astype(jnp.bfloat16)

    mix = _dot(oa, wo_ref[:DIL_WIDTH, :]) + _dot(od_ref[...], wo_ref[DIL_WIDTH:, :])
    x1 = _layer_norm(DEEPNORM_ALPHA * x_ref[...] + mix, g_ref[...], b_ref[...])
    x1_ref[...] = x1
    for s in range(ROW_SUBLANES):
        x1t_ref[pl.ds(s, x1.shape[0], stride=ROW_SUBLANES), :] = x1[:, s * LANES:(s + 1) * LANES]

    tm = x1.shape[0]
    x_hi = x1.astype(jnp.bfloat16)
    x_lo = (x1 - x_hi.astype(jnp.float32)).astype(jnp.bfloat16)
    lg = (_dot_nt(wr_ref[0], x_hi) + _dot_nt(wr_ref[1], x_hi) + _dot_nt(wr_ref[0], x_lo)
          + br_ref[...])
    eidx = lax.broadcasted_iota(jnp.int32, (N_EXPERTS, tm), 0)
    vals, sels = [], []
    for k in range(TOP_K):
        mx = jnp.max(lg, axis=0, keepdims=True)
        idx = jnp.min(jnp.where(lg == mx, eidx, N_EXPERTS), axis=0, keepdims=True)
        sel = eidx == idx
        vals.append(mx)
        sels.append(sel)
        idx_ref[k:k + 1, :] = idx
        lg = jnp.where(sel, -jnp.inf, lg)
    ex = [jnp.exp(v - vals[0]) for v in vals]
    den = ex[0] + ex[1] + ex[2] + ex[3]
    for k in range(TOP_K):
        gate_ref[k:k + 1, :] = ex[k] / den

    chosen = (sels[0] | sels[1] | sels[2] | sels[3])
    onehot = jnp.where(chosen, 1.0, 0.0)
    before = _dot(onehot.astype(jnp.bfloat16), tri_ref[...]) + carry_scr[...]
    for k in range(TOP_K):
        rank_ref[k:k + 1, :] = jnp.sum(jnp.where(sels[k], before, 0.0), axis=0,
                                       keepdims=True).astype(jnp.int32)
    carry_scr[...] = carry_scr[...] + jnp.sum(onehot, axis=1, keepdims=True)
    pad = jnp.zeros((SUBLANES - TOP_K, tm), jnp.int32)
    idx_ref[TOP_K:, :] = pad
    rank_ref[TOP_K:, :] = pad
    gate_ref[TOP_K:, :] = pad.astype(jnp.float32)
    cnt_ref[...] = jnp.broadcast_to(carry_scr[...], cnt_ref.shape).astype(jnp.int32)


def _post_attn(dil, od, x, wo_bf, g, b, wr_t, br, tri):
    T, D = x.shape
    assert D == ROW_SUBLANES * LANES
    tm = ROW_TILE
    nb = dil[0][0].shape[2] // tm
    row = lambda w: pl.BlockSpec((tm, w), lambda i: (i, 0))
    full = lambda a: pl.BlockSpec(a.shape, lambda i: (0,) * a.ndim)
    col = pl.BlockSpec((SUBLANES, tm), lambda i: (0, i))
    dil_specs, dil_args = [], []
    for o_d, lse_d in dil:
        d = o_d.shape[1]
        spec = pl.BlockSpec((1, d, tm // d, DIL_WIDTH), lambda i: (i // nb, 0, i % nb, 0))
        dil_specs += [spec, spec]
        dil_args += [o_d, lse_d]
    return pl.pallas_call(
        _post_attn_kernel,
        grid=(T // tm,),
        in_specs=dil_specs + [row(DIFF_WIDTH), row(D), full(wo_bf), full(g), full(b), full(wr_t),
                              full(br), full(tri)],
        out_specs=[row(D), pl.BlockSpec((tm * ROW_SUBLANES, LANES), lambda i: (i, 0)), col, col, col,
                   pl.BlockSpec((N_EXPERTS, LANES), lambda i: (0, 0))],
        out_shape=[
            jax.ShapeDtypeStruct((T, D), jnp.float32),
            jax.ShapeDtypeStruct((T * ROW_SUBLANES, LANES), jnp.float32),
            jax.ShapeDtypeStruct((SUBLANES, T), jnp.int32),
            jax.ShapeDtypeStruct((SUBLANES, T), jnp.float32),
            jax.ShapeDtypeStruct((SUBLANES, T), jnp.int32),
            jax.ShapeDtypeStruct((N_EXPERTS, LANES), jnp.int32),
        ],
        scratch_shapes=[pltpu.VMEM((N_EXPERTS, 1), jnp.float32),
                        pltpu.VMEM((4, DIL_WIDTH // LANES, tm, LANES), jnp.float32)],
        compiler_params=_cparams(("arbitrary",)),
        name="post_attn_router",
    )(*dil_args, od, x, wo_bf, g, b, wr_t, br, tri)


def _index_copy(dest_hbm, step, idx_smem, sem):
    per_step = dest_hbm.shape[1]
    half = pl.multiple_of((step % 2) * per_step, per_step)
    return pltpu.make_async_copy(dest_hbm.at[step], idx_smem.at[pl.ds(half, per_step)], sem)


def _stage_index(dest_hbm, step, n_steps, idx_smem, sem):
    @pl.when(step == 0)
    def _():
        _index_copy(dest_hbm, 0, idx_smem, sem).start()

    _index_copy(dest_hbm, step, idx_smem, sem).wait()

    @pl.when(step + 1 < n_steps)
    def _():
        _index_copy(dest_hbm, step + 1, idx_smem, sem).start()

    return (step % 2) * dest_hbm.shape[1]


def _row_tile(ref, row):
    return ref.at[pl.ds(pl.multiple_of(row * ROW_SUBLANES, ROW_SUBLANES), ROW_SUBLANES)]


def _rows_from_tiles(ref, n):
    return jnp.concatenate([ref[pl.ds(s, n, stride=ROW_SUBLANES), :] for s in range(ROW_SUBLANES)], axis=1)


def _rows_to_tiles(ref, rows):
    for s in range(ROW_SUBLANES):
        ref[pl.ds(s, rows.shape[0], stride=ROW_SUBLANES), :] = rows[:, s * LANES:(s + 1) * LANES]


def _dispatch_kernel(pad_start_ref, n_pad_ref, n_used_ref, dest_hbm, x_ref, xs_hbm, idx_smem, idx_sem, row_sem,
                     zero_scr, zero_sem):
    tm = x_ref.shape[0] // ROW_SUBLANES
    tile_rows = MOE_TILE * ROW_SUBLANES
    n_tiles = xs_hbm.shape[0] // tile_rows

    def pad_copies(start_not_wait):
        for j in range(N_EXPERTS):
            tile = n_used_ref[0] + j

            @pl.when(tile < n_tiles)
            def _():
                dst = xs_hbm.at[pl.ds(pl.multiple_of(tile * tile_rows, tile_rows), tile_rows)]
                cp = pltpu.make_async_copy(zero_scr, dst, zero_sem)
                cp.start() if start_not_wait else cp.wait()

        for e in range(N_EXPERTS):
            n_pad = n_pad_ref[e]
            for bit in reversed(range(PAD_BITS)):
                size = 1 << bit

                @pl.when((n_pad & size) != 0)
                def _():
                    row = pad_start_ref[e] + (n_pad & ~(2 * size - 1))
                    dst = xs_hbm.at[pl.ds(pl.multiple_of(row * ROW_SUBLANES, ROW_SUBLANES), size * ROW_SUBLANES)]
                    cp = pltpu.make_async_copy(zero_scr.at[pl.ds(0, size * ROW_SUBLANES)], dst, zero_sem)
                    cp.start() if start_not_wait else cp.wait()

    @pl.when(pl.program_id(0) == 0)
    def _():
        zero_scr[...] = jnp.zeros(zero_scr.shape, zero_scr.dtype)
        pad_copies(True)
        pad_copies(False)

    base = _stage_index(dest_hbm, pl.program_id(0), pl.num_programs(0), idx_smem, idx_sem)

    def issue(t, carry):
        for k in range(TOP_K):
            pltpu.make_async_copy(_row_tile(x_ref, t), _row_tile(xs_hbm, idx_smem[base + k * tm + t]),
                                  row_sem).start(priority=k % 2)
        return carry

    lax.fori_loop(0, tm, issue, 0, unroll=8)
    for k in range(TOP_K):
        pltpu.make_async_copy(x_ref, xs_hbm.at[pl.ds(0, tm * ROW_SUBLANES)], row_sem).wait()


def _dispatch(pad_start, n_pad, n_used, dest_steps, x1t, n_rows):
    n_steps, per_step = dest_steps.shape
    tm = per_step // TOP_K
    assert MOE_TILE == 1 << PAD_BITS
    grid_spec = pltpu.PrefetchScalarGridSpec(
        num_scalar_prefetch=3,
        grid=(n_steps,),
        in_specs=[
            pl.BlockSpec(memory_space=pl.ANY),
            pl.BlockSpec((tm * ROW_SUBLANES, LANES), lambda i, ps, npad, nu: (i, 0)),
        ],
        out_specs=pl.BlockSpec(memory_space=pl.ANY),
        scratch_shapes=[pltpu.SMEM((2 * per_step,), jnp.int32), pltpu.SemaphoreType.DMA(()),
                        pltpu.SemaphoreType.DMA(()),
                        pltpu.VMEM((MOE_TILE * ROW_SUBLANES, LANES), jnp.float32),
                        pltpu.SemaphoreType.DMA(())],
    )
    return pl.pallas_call(
        _dispatch_kernel,
        grid_spec=grid_spec,
        out_shape=jax.ShapeDtypeStruct((n_rows * ROW_SUBLANES, LANES), jnp.float32),
        compiler_params=_cparams(("arbitrary",)),
        name="dispatch_rows",
    )(pad_start, n_pad, n_used, dest_steps, x1t)


def _expert_kernel(tile_e_ref, n_used_ref, xs_ref, wu_ref, bu_ref, wd_ref, bd_ref, y_ref, wu_scr, wd_scr):
    i = pl.program_id(0)
    used = i < n_used_ref[0]
    new_expert = (i == 0) | (tile_e_ref[i] != tile_e_ref[jnp.maximum(i - 1, 0)])
    tm = xs_ref.shape[0] // ROW_SUBLANES

    @pl.when(used & new_expert)
    def _():
        wu_scr[...] = wu_ref[0].astype(jnp.bfloat16)
        wd_scr[...] = wd_ref[0].astype(jnp.bfloat16)

    @pl.when(used)
    def _():
        xs = _rows_from_tiles(xs_ref, tm).astype(jnp.bfloat16)
        hu = _dot(xs, wu_scr[...]) + bu_ref[0]
        g = jnp.minimum(hu[:, :D_FF], SWIGLU_LIMIT)
        u = jnp.clip(hu[:, D_FF:], -SWIGLU_LIMIT, SWIGLU_LIMIT)
        act = g * (1.0 / (1.0 + jnp.exp(-SWIGLU_ALPHA * g))) * (u + 1.0)
        _rows_to_tiles(y_ref, _dot(act.astype(jnp.bfloat16), wd_scr[...]) + bd_ref[0])

    @pl.when(jnp.logical_not(used))
    def _():
        y_ref[...] = jnp.zeros(y_ref.shape, y_ref.dtype)


def _experts(tile_e, n_used, xs, wu, bu, wd, bd):
    D = ROW_SUBLANES * LANES
    n_rows = xs.shape[0] // ROW_SUBLANES
    tm = MOE_TILE
    tile_spec = pl.BlockSpec((tm * ROW_SUBLANES, LANES), lambda i, te, nu: (i, 0))
    grid_spec = pltpu.PrefetchScalarGridSpec(
        num_scalar_prefetch=2,
        grid=(n_rows // tm,),
        in_specs=[
            pl.BlockSpec((tm * ROW_SUBLANES, LANES), lambda i, te, nu: (jnp.minimum(i, nu[0] - 1), 0)),
            pl.BlockSpec((1, D, 2 * D_FF), lambda i, te, nu: (te[i], 0, 0)),
            pl.BlockSpec((1, 1, 2 * D_FF), lambda i, te, nu: (te[i], 0, 0)),
            pl.BlockSpec((1, D_FF, D), lambda i, te, nu: (te[i], 0, 0)),
            pl.BlockSpec((1, 1, D), lambda i, te, nu: (te[i], 0, 0)),
        ],
        out_specs=tile_spec,
        scratch_shapes=[pltpu.VMEM((D, 2 * D_FF), jnp.bfloat16), pltpu.VMEM((D_FF, D), jnp.bfloat16)],
    )
    return pl.pallas_call(
        _expert_kernel,
        grid_spec=grid_spec,
        out_shape=jax.ShapeDtypeStruct(xs.shape, jnp.float32),
        compiler_params=_cparams(("arbitrary",)),
        name="experts",
    )(tile_e, n_used, xs, wu, bu, wd, bd)


def _combine_kernel(dest_hbm, y_hbm, x1_ref, gate_ref, g_ref, b_ref, o_ref, buf, idx_smem, idx_sem, row_sems):
    i = pl.program_id(0)
    n = pl.num_programs(0)
    tm = x1_ref.shape[0]

    def issue(step, slot):
        base = _stage_index(dest_hbm, step, n, idx_smem, idx_sem)

        def body(t, carry):
            for k in range(TOP_K):
                pltpu.make_async_copy(_row_tile(y_hbm, idx_smem[base + k * tm + t]),
                                      _row_tile(buf.at[slot, k], t), row_sems.at[slot]).start(priority=k % 2)
            return carry

        lax.fori_loop(0, tm, body, 0, unroll=8)

    @pl.when(i == 0)
    def _():
        issue(i, 0)

    @pl.when(i + 1 < n)
    def _():
        issue(i + 1, (i + 1) % 2)

    slot = i % 2
    for k in range(TOP_K):
        pltpu.make_async_copy(y_hbm.at[pl.ds(0, tm * ROW_SUBLANES)], buf.at[slot, k], row_sems.at[slot]).wait()
    ffn = _rows_from_tiles(buf.at[slot, 0], tm) * gate_ref[:, 0:1]
    for k in range(1, TOP_K):
        ffn = ffn + _rows_from_tiles(buf.at[slot, k], tm) * gate_ref[:, k:k + 1]
    o_ref[...] = _layer_norm(DEEPNORM_ALPHA * x1_ref[...] + ffn, g_ref[...], b_ref[...])


def _combine(dest_steps, y, x1, gates_t, g, b):
    T, D = x1.shape
    n_steps, per_step = dest_steps.shape
    tm = per_step // TOP_K
    full = lambda a: pl.BlockSpec(a.shape, lambda i: (0, 0))
    return pl.pallas_call(
        _combine_kernel,
        grid=(n_steps,),
        in_specs=[
            pl.BlockSpec(memory_space=pl.ANY),
            pl.BlockSpec(memory_space=pl.ANY),
            pl.BlockSpec((tm, D), lambda i: (i, 0)),
            pl.BlockSpec((tm, SUBLANES), lambda i: (i, 0)),
            full(g), full(b),
        ],
        out_specs=pl.BlockSpec((tm, D), lambda i: (i, 0)),
        out_shape=jax.ShapeDtypeStruct((T, D), jnp.float32),
        scratch_shapes=[pltpu.VMEM((2, TOP_K, tm * ROW_SUBLANES, LANES), jnp.float32),
                        pltpu.SMEM((2 * per_step,), jnp.int32),
                        pltpu.SemaphoreType.DMA(()), pltpu.SemaphoreType.DMA((2,))],
        compiler_params=_cparams(("arbitrary",)),
        name="combine_ln2",
    )(dest_steps, y, x1, gates_t, g, b)


def kernel(x, w_in, w_out, lambda_q1, lambda_k1, lambda_q2, lambda_k2, diff_norm_g, ln1_g, ln1_b,
           w_router, b_router, w_up, b_up, w_down, b_down, ln2_g, ln2_b):
    B, S, D = x.shape
    T = B * S
    assert w_in.shape[0] == 1, "single layer"
    assert S % DIFF_TQ == 0 and DIFF_TK == DIFF_TQ and DIFF_TK % ROW_TILE == 0 and S % (16 * DIL_TQ) == 0

    na = 3 * DIL_WIDTH
    colscale = np.ones((3 * D,), np.float32)
    colscale[:DIL_WIDTH] = HEAD_DIM ** -0.5
    colscale[na:na + DIFF_WIDTH] = HEAD_DIM ** -0.5 * LOG2E
    w_in_bf = (w_in[0] * colscale).astype(jnp.bfloat16)
    nv = na + 2 * DIFF_WIDTH

    kpos, qpos, dabs = _position_columns()
    pa, pa4, pa16, qd, ka, vt, qn, kn = _in_proj(x, w_in_bf[:, :nv], w_in_bf[:, nv:].T, kpos)
    dil = [_dilated_stage(p, d) for p, (_, d) in zip((pa[:, None], pa4, pa16), DIL_PATTERNS)]

    r2 = lambda a: a.reshape(1, -1).astype(jnp.float32)
    od = _diff_attention(_tile_reach(qn, kn, S // DIFF_TK), qd, ka, vt, qpos, dabs, r2(lambda_q1[0]), r2(lambda_k1[0]), r2(lambda_q2[0]),
                         r2(lambda_k2[0]), diff_norm_g[0].reshape(-1, 1).astype(jnp.float32))

    tri = jnp.asarray(np.triu(np.ones((ROW_TILE, ROW_TILE), np.float32), k=1), jnp.bfloat16)
    n_rows = T * TOP_K + N_EXPERTS * MOE_TILE
    n_tiles = n_rows // MOE_TILE
    wr_t = w_router[0].T.astype(jnp.float32)
    wr_hi = wr_t.astype(jnp.bfloat16)
    wr_split = jnp.stack([wr_hi, (wr_t - wr_hi.astype(jnp.float32)).astype(jnp.bfloat16)])
    x1, x1t, top_idx, gates, rank, counts = _post_attn(
        dil, od.reshape(T, DIFF_WIDTH), x.reshape(T, D), w_out[0].astype(jnp.bfloat16),
        r2(ln1_g[0]), r2(ln1_b[0]), wr_split, b_router[0].reshape(N_EXPERTS, 1), tri)

    counts = counts[:, 0]
    pcounts = ((counts + MOE_TILE - 1) // MOE_TILE) * MOE_TILE
    pends = jnp.cumsum(pcounts)
    pstarts = pends - pcounts
    dest = rank[:TOP_K]
    for e in range(N_EXPERTS):
        dest = dest + jnp.where(top_idx[:TOP_K] == e, pstarts[e], 0)
    tile_starts = jnp.arange(n_tiles, dtype=jnp.int32) * MOE_TILE
    tile_e = jnp.minimum(jnp.sum(pends[None, :] <= tile_starts[:, None], axis=1), N_EXPERTS - 1).astype(jnp.int32)
    n_used = (pends[-1] // MOE_TILE).astype(jnp.int32).reshape(1)

    def per_step(tm):
        return dest.reshape(TOP_K, T // tm, tm).transpose(1, 0, 2).reshape(T // tm, TOP_K * tm)

    xs = _dispatch((pstarts + counts).astype(jnp.int32), (pcounts - counts).astype(jnp.int32), n_used,
                   per_step(ROW_TILE), x1t, n_rows)
    y = _experts(tile_e, n_used, xs, w_up[0], b_up[0][:, None, :], w_down[0], b_down[0][:, None, :])
    out = _combine(per_step(COMBINE_TILE), y, x1, gates.T, r2(ln2_g[0]), r2(ln2_b[0]))
    return out.reshape(B, S, D)
```

```python
import functools
import math

import jax
import jax.numpy as jnp
import numpy as np
from jax import lax
from jax.experimental import pallas as pl
from jax.experimental.pallas import tpu as pltpu

D_MODEL = 1024
HEAD_DIM = 64
DIL_WIDTH = 512
N_HEADS_DIL = 8
DIL_PATTERNS = ((128, 1), (512, 4), (2048, 16))
DIL_SIDE = 64
DIFF_WIDTH = 512
N_HEADS_DIFF = 4
DIFF_VDIM = 2 * HEAD_DIM
N_EXPERTS = 32
TOP_K = 4
D_FF = D_MODEL
SWIGLU_ALPHA = 1.702
SWIGLU_LIMIT = 7.0
DEEPNORM_ALPHA = 2.0 ** 0.25
LN_EPS = 1e-5
NEG_INF = -1e30
LAM_INIT = 0.8 - 0.6 * math.exp(-0.3 * 0)
LOG2E = math.log2(math.e)

LANES = 128
SUBLANES = 8
BF16_SUBLANES = 16
VMEM_LIMIT = 56 * 1024 * 1024

ROW_TILE = 512
DIFF_TQ = 1024
DIFF_TK = 1024
V_ROWS = DIFF_VDIM + BF16_SUBLANES
EXP2_HEADROOM = 4.0
UNDERFLOW_LOG2 = 160.0
NORM_SLACK = 1.01
DIL_TQ = 128
DIL_QB = 8
MOE_TILE = 512
COMBINE_TILE = 256
ROW_SUBLANES = D_MODEL // LANES
PAD_BITS = MOE_TILE.bit_length() - 1
POS_SPLIT = 16


def _cparams(sem):
    return pltpu.CompilerParams(dimension_semantics=sem, vmem_limit_bytes=VMEM_LIMIT)


def _dot_nt(a, b, **kw):
    return lax.dot_general(a, b, (((1,), (1,)), ((), ())), preferred_element_type=jnp.float32, **kw)


def _dot(a, b, **kw):
    return jnp.dot(a, b, preferred_element_type=jnp.float32, **kw)


def _layer_norm(y, g, b):
    mu = jnp.mean(y, axis=-1, keepdims=True)
    yc = y - mu
    var = jnp.mean(yc * yc, axis=-1, keepdims=True)
    return yc * lax.rsqrt(var + LN_EPS) * g + b


def _in_proj_kernel(x_ref, w_ref, wvt_ref, kpos_ref, grp_ref, pa_ref, pa4_ref, pa16_ref, qd_ref, ka_ref,
                    vt_ref, qn_ref, kn_ref, pa_scr):
    xb = x_ref[0].astype(jnp.bfloat16)
    na = 3 * DIL_WIDTH
    tm = xb.shape[0]
    paf = _dot(xb, w_ref[:, :na])
    pa_ref[0] = paf.astype(jnp.bfloat16)
    for c in range(na // LANES):
        cols = slice(c * LANES, (c + 1) * LANES)
        pa_scr[c] = paf[:, cols]
        for d, ref in ((DIL_PATTERNS[1][1], pa4_ref), (DIL_PATTERNS[2][1], pa16_ref)):
            for r in range(d):
                ref[0, r, :, cols] = pa_scr[c, pl.ds(r, tm // d, stride=d), :].astype(jnp.bfloat16)
    q = _dot(xb, w_ref[:, na:na + DIFF_WIDTH]).astype(jnp.bfloat16)
    k = _dot(xb, w_ref[:, na + DIFF_WIDTH:]).astype(jnp.bfloat16)
    vt = _dot_nt(wvt_ref[...], xb).astype(jnp.bfloat16)
    ones = jnp.ones((V_ROWS - DIFF_VDIM, tm), jnp.bfloat16)

    @pl.when(pl.program_id(1) == 0)
    def _():
        qn_ref[...] = jnp.zeros(qn_ref.shape, jnp.float32)
        kn_ref[...] = jnp.zeros(kn_ref.shape, jnp.float32)

    for src, dst in ((q, qn_ref), (k, kn_ref)):
        f = src.astype(jnp.float32)
        gs = _dot((f * f).astype(jnp.bfloat16), grp_ref[...])
        dst[0] = jnp.maximum(dst[0], jnp.max(gs, axis=0, keepdims=True))
    for h in range(N_HEADS_DIFF):
        sl = slice(h * LANES, (h + 1) * LANES)
        qd_ref[0, h] = q[:, sl]
        ka_ref[0, h, :, :LANES] = k[:, sl]
        ka_ref[0, h, :, LANES:] = kpos_ref[h]
        vt_ref[0, h, :DIFF_VDIM, :] = vt[sl, :]
        vt_ref[0, h, DIFF_VDIM:, :] = ones


def _in_proj(x, w_bf, wvt_bf, kpos):
    B, S, D = x.shape
    tm = ROW_TILE
    na = 3 * DIL_WIDTH
    grp = np.zeros((DIFF_WIDTH, LANES), np.float32)
    grp[np.arange(DIFF_WIDTH), np.arange(DIFF_WIDTH) // HEAD_DIM] = 1.0
    grp = jnp.asarray(grp, jnp.bfloat16)
    norm_spec = pl.BlockSpec((1, 1, LANES), lambda b, i: (b, 0, 0))
    d4, d16 = DIL_PATTERNS[1][1], DIL_PATTERNS[2][1]
    return pl.pallas_call(
        _in_proj_kernel,
        grid=(B, S // tm),
        in_specs=[
            pl.BlockSpec((1, tm, D), lambda b, i: (b, i, 0)),
            pl.BlockSpec(w_bf.shape, lambda b, i: (0, 0)),
            pl.BlockSpec(wvt_bf.shape, lambda b, i: (0, 0)),
            pl.BlockSpec((N_HEADS_DIFF, tm, LANES), lambda b, i: (0, i % (DIFF_TK // tm), 0)),
            pl.BlockSpec(grp.shape, lambda b, i: (0, 0)),
        ],
        out_specs=[
            pl.BlockSpec((1, tm, na), lambda b, i: (b, i, 0)),
            pl.BlockSpec((1, d4, tm // d4, na), lambda b, i: (b, 0, i, 0)),
            pl.BlockSpec((1, d16, tm // d16, na), lambda b, i: (b, 0, i, 0)),
            pl.BlockSpec((1, N_HEADS_DIFF, tm, LANES), lambda b, i: (b, 0, i, 0)),
            pl.BlockSpec((1, N_HEADS_DIFF, tm, 2 * LANES), lambda b, i: (b, 0, i, 0)),
            pl.BlockSpec((1, N_HEADS_DIFF, V_ROWS, tm), lambda b, i: (b, 0, 0, i)),
            norm_spec, norm_spec,
        ],
        out_shape=[
            jax.ShapeDtypeStruct((B, S, na), jnp.bfloat16),
            jax.ShapeDtypeStruct((B, d4, S // d4, na), jnp.bfloat16),
            jax.ShapeDtypeStruct((B, d16, S // d16, na), jnp.bfloat16),
            jax.ShapeDtypeStruct((B, N_HEADS_DIFF, S, LANES), jnp.bfloat16),
            jax.ShapeDtypeStruct((B, N_HEADS_DIFF, S, 2 * LANES), jnp.bfloat16),
            jax.ShapeDtypeStruct((B, N_HEADS_DIFF, V_ROWS, S), jnp.bfloat16),
            jax.ShapeDtypeStruct((B, 1, LANES), jnp.float32),
            jax.ShapeDtypeStruct((B, 1, LANES), jnp.float32),
        ],
        scratch_shapes=[pltpu.VMEM((na // LANES, tm, LANES), jnp.float32)],
        compiler_params=_cparams(("parallel", "arbitrary")),
        name="in_proj",
    )(x, w_bf, wvt_bf, kpos, grp)


def _dilated_kernel(*refs, seq_len):
    n_halo = 2 * DIL_QB + 2
    q_ref, k_refs, v_refs = refs[0], refs[1:1 + n_halo], refs[1 + n_halo:1 + 2 * n_halo]
    bias_ref, o_ref, lse_ref = refs[1 + 2 * n_halo:]
    tq = DIL_TQ
    nk = tq + 2 * DIL_SIDE
    i = pl.program_id(2)
    q = q_ref[0, 0]
    kwin = jnp.concatenate([r[0, 0] for r in k_refs], axis=0)
    vwin = jnp.concatenate([r[0, 0] for r in v_refs], axis=0)
    low_half = lax.broadcasted_iota(jnp.int32, (tq, LANES), 1) < HEAD_DIM
    ss = []
    for u in range(DIL_QB):
        key_pos = (i * DIL_QB + u) * tq - DIL_SIDE + lax.broadcasted_iota(jnp.int32, (1, nk), 1)
        edge = jnp.where((key_pos >= 0) & (key_pos < seq_len), 0.0, NEG_INF)
        for h in range(N_HEADS_DIL):
            sl = slice((h // 2) * LANES, (h // 2 + 1) * LANES)
            qu = q[u * tq:(u + 1) * tq, sl]
            qm = jnp.where(low_half == (h % 2 == 0), qu, jnp.zeros((tq, LANES), q.dtype))
            ss.append(_dot_nt(qm, kwin[u * tq:u * tq + nk, sl]) + bias_ref[h] + edge)
    s = jnp.concatenate(ss, axis=0)
    m = jnp.max(s, axis=-1, keepdims=True)
    p = jnp.exp(s - m)
    l = jnp.sum(p, axis=-1, keepdims=True)
    lse = m + jnp.log(l)
    inv_l = 1.0 / l
    pb = p.astype(jnp.bfloat16)
    for u in range(DIL_QB):
        for pair in range(N_HEADS_DIL // 2):
            sl = slice(pair * LANES, (pair + 1) * LANES)
            base = (u * N_HEADS_DIL + 2 * pair) * tq
            r0 = slice(base, base + tq)
            r1 = slice(base + tq, base + 2 * tq)
            vu = vwin[u * tq:u * tq + nk, sl]
            a0 = _dot(pb[r0], vu) * inv_l[r0]
            a1 = _dot(pb[r1], vu) * inv_l[r1]
            rows = slice(u * tq, (u + 1) * tq)
            o_ref[0, 0, rows, sl] = jnp.where(low_half, a0, a1).astype(o_ref.dtype)
            lse_ref[0, 0, rows, sl] = jnp.where(low_half, lse[r0], lse[r1])


def _dilated_stage(pa_d, dilation):
    B, d, L, _ = pa_d.shape
    tb = DIL_QB * DIL_TQ
    w = DIL_WIDTH
    halo = DIL_SIDE
    n_halo = L // halo
    per_step = tb // halo + 2
    bias = _dilated_bias(dilation)
    assert L % tb == 0

    def halo_spec(col, j):
        def imap(b, r, i):
            return (b, r, jnp.clip(i * (tb // halo) - 1 + j, 0, n_halo - 1), col)
        return pl.BlockSpec((1, 1, halo, w), imap)

    out_spec = pl.BlockSpec((1, 1, tb, w), lambda b, r, i: (b, r, i, 0))
    in_specs = ([pl.BlockSpec((1, 1, tb, w), lambda b, r, i: (b, r, i, 0))]
                + [halo_spec(1, j) for j in range(per_step)] + [halo_spec(2, j) for j in range(per_step)]
                + [pl.BlockSpec(bias.shape, lambda b, r, i: (0, 0, 0))])
    return pl.pallas_call(
        functools.partial(_dilated_kernel, seq_len=L),
        grid=(B, d, L // tb),
        in_specs=in_specs,
        out_specs=[out_spec, out_spec],
        out_shape=[jax.ShapeDtypeStruct((B, d, L, w), jnp.bfloat16),
                   jax.ShapeDtypeStruct((B, d, L, w), jnp.float32)],
        compiler_params=_cparams(("parallel", "parallel", "parallel")),
        name=f"dilated_d{d}",
    )(*([pa_d] * (1 + 2 * per_step)), bias)


def _dilated_bias(dilation):
    tq = DIL_TQ
    slopes = np.asarray([2.0 ** (-8.0 * (i + 1) / N_HEADS_DIL) for i in range(N_HEADS_DIL)], np.float32)
    rel = (np.arange(tq + 2 * DIL_SIDE)[None, :] - DIL_SIDE) - np.arange(tq)[:, None]
    band = np.abs(rel) <= DIL_SIDE
    pen = -(slopes * dilation)[:, None, None] * np.abs(rel).astype(np.float32)[None]
    return jnp.asarray(np.where(band[None], pen, np.float32(NEG_INF)), jnp.float32)


def _diff_kernel(reach_ref, q_ref, ka_hbm, vt_hbm, qpos_ref, dabs_ref, lq1_ref, lk1_ref, lq2_ref, lk2_ref,
                 g_ref, o_ref, qa_scr, m_scr, acc_scr, k_buf, v_buf, sems, *, slopes, n_key_tiles):
    b = pl.program_id(0)
    h = pl.program_id(1)
    qi = pl.program_id(2)
    tq = q_ref.shape[2]
    tk = k_buf.shape[1]

    slope = jnp.float32(slopes[0])
    for hh in range(1, N_HEADS_DIFF):
        slope = jnp.where(h == hh, jnp.float32(slopes[hh]), slope)

    reach = reach_ref[b * N_HEADS_DIFF + h]
    first = jnp.maximum(qi - reach, 0)
    n_active = jnp.minimum(qi + reach, n_key_tiles - 1) - first + 1

    def key_tile(j):
        t = first + j - 1
        return jnp.where(j == 0, qi, jnp.where(t < qi, t, t + 1))

    def tile_copies(j, slot):
        start = pl.multiple_of(key_tile(j) * tk, tk)
        return (pltpu.make_async_copy(ka_hbm.at[b, h, pl.ds(start, tk), :], k_buf.at[slot], sems.at[0, slot]),
                pltpu.make_async_copy(vt_hbm.at[b, h, :, pl.ds(start, tk)], v_buf.at[slot], sems.at[1, slot]))

    for cp in tile_copies(0, 0):
        cp.start()

    @pl.when(n_active > 1)
    def _():
        for cp in tile_copies(1, 1):
            cp.start()

    q = q_ref[0, 0]
    low_half = lax.broadcasted_iota(jnp.int32, (tq, LANES), 1) < HEAD_DIM
    zero = jnp.zeros_like(q)
    qpos = qpos_ref[0]
    for c in range(2):
        qc = jnp.where(low_half == (c == 0), q, zero)
        for var, pos in enumerate((qpos, -qpos, jnp.zeros_like(qpos))):
            qa_scr[c, var, :, :LANES] = qc
            qa_scr[c, var, :, LANES:] = pos
    m_scr[...] = jnp.full(m_scr.shape, NEG_INF, jnp.float32)
    acc_scr[0] = jnp.zeros(acc_scr.shape[1:], jnp.float32)


    def restabilise(slot, kt, diag, cur):
        var = 2 if diag else jnp.where(kt < qi, 0, 1)
        off = -slope * jnp.abs(qi * tq - kt * tk).astype(jnp.float32)
        ka = k_buf[slot]
        vt = v_buf[slot]
        ss = []
        for c in range(2):
            s = _dot_nt(ka, qa_scr[c, var])
            if diag:
                s = s + dabs_ref[...] * slope
            ss.append(s)
        for c in range(2):
            s = ss[c]
            m_old = m_scr[c]
            m_new = jnp.maximum(m_old, jnp.max(s, axis=0, keepdims=True) + off)
            alpha = jnp.exp2(m_old - m_new)
            p = jnp.exp2((s - (m_new - off)).astype(jnp.bfloat16))
            acc_scr[cur, c] = alpha * acc_scr[cur, c] + _dot(vt, p)
            m_scr[c] = m_new

    for cp in tile_copies(0, 0):
        cp.wait()
    restabilise(0, qi, True, 0)

    def visit(j, cur):
        slot = j % 2

        @pl.when(j + 1 < n_active)
        def _():
            for cp in tile_copies(j + 1, 1 - slot):
                cp.start()

        for cp in tile_copies(j, slot):
            cp.wait()
        kt = key_tile(j)
        var = jnp.where(kt < qi, 0, 1)
        off = -slope * jnp.abs(qi * tq - kt * tk).astype(jnp.float32)
        ka = k_buf[slot]
        vt = v_buf[slot]
        excess = None
        for c in range(2):
            s = _dot_nt(ka, qa_scr[c, var])
            m_eff = m_scr[c] - off
            over = jnp.max(jnp.max(s, axis=0, keepdims=True) - m_eff)
            excess = over if excess is None else jnp.maximum(excess, over)
            p = jnp.exp2((s - m_eff).astype(jnp.bfloat16))
            acc_scr[1 - cur, c] = acc_scr[cur, c] + _dot(vt, p)

        @pl.when(excess > EXP2_HEADROOM)
        def _():
            restabilise(slot, kt, False, cur)

        return jnp.where(excess <= EXP2_HEADROOM, 1 - cur, cur)

    cur = lax.fori_loop(1, n_active, visit, jnp.int32(0))

    lam = (jnp.exp(jnp.sum(lq1_ref[...] * lk1_ref[...], axis=-1, keepdims=True))
           - jnp.exp(jnp.sum(lq2_ref[...] * lk2_ref[...], axis=-1, keepdims=True)) + LAM_INIT)
    a1, a2 = acc_scr[cur, 0], acc_scr[cur, 1]
    o = (a1[:DIFF_VDIM] / a1[DIFF_VDIM:DIFF_VDIM + 1]
         - lam * (a2[:DIFF_VDIM] / a2[DIFF_VDIM:DIFF_VDIM + 1]))
    o = o * lax.rsqrt(jnp.mean(o * o, axis=0, keepdims=True) + LN_EPS) * g_ref[...]
    o_ref[0] = (o * (1.0 - LAM_INIT)).T.astype(o_ref.dtype)


def _tile_reach(qn, kn, n_tiles):
    B = qn.shape[0]
    ng = 2 * N_HEADS_DIFF
    qmax = jnp.sqrt(qn[:, 0, :ng]) * NORM_SLACK
    kmax = jnp.sqrt(kn[:, 0, :ng]) * NORM_SLACK
    bound = jnp.max((2.0 * qmax * kmax).reshape(B, N_HEADS_DIFF, 2), axis=-1) + UNDERFLOW_LOG2
    far = bound / jnp.asarray(_diff_slopes_log2())[None, :]
    reach = jnp.floor((far - 1.0) / DIFF_TK) + 1.0
    reach = jnp.where(jnp.isfinite(reach), reach, n_tiles)
    return jnp.clip(reach, 0, n_tiles).astype(jnp.int32).reshape(-1)


def _diff_attention(reach, qd, ka, vt, qpos, dabs, lq1, lk1, lq2, lk2, g_col):
    B, H, S, _ = qd.shape
    tq, tk = DIFF_TQ, DIFF_TK
    slopes = tuple(float(s) for s in _diff_slopes_log2())
    nk = S // tk
    small = lambda a: pl.BlockSpec(a.shape, lambda b, h, qi, reach: (0, 0))
    grid_spec = pltpu.PrefetchScalarGridSpec(
        num_scalar_prefetch=1,
        grid=(B, H, S // tq),
        in_specs=[
            pl.BlockSpec((1, 1, tq, LANES), lambda b, h, qi, reach: (b, h, qi, 0)),
            pl.BlockSpec(memory_space=pl.ANY),
            pl.BlockSpec(memory_space=pl.ANY),
            pl.BlockSpec((1, tq, LANES), lambda b, h, qi, reach: (h, 0, 0)),
            small(dabs), small(lq1), small(lk1), small(lq2), small(lk2), small(g_col),
        ],
        out_specs=pl.BlockSpec((1, tq, LANES), lambda b, h, qi, reach: (b, qi, h)),
        scratch_shapes=[
            pltpu.VMEM((2, 3, tq, 2 * LANES), jnp.bfloat16),
            pltpu.VMEM((2, 1, tq), jnp.float32),
            pltpu.VMEM((2, 2, V_ROWS, tq), jnp.float32),
            pltpu.VMEM((2, tk, 2 * LANES), jnp.bfloat16),
            pltpu.VMEM((2, V_ROWS, tk), jnp.bfloat16),
            pltpu.SemaphoreType.DMA((2, 2)),
        ],
    )
    return pl.pallas_call(
        functools.partial(_diff_kernel, slopes=slopes, n_key_tiles=nk),
        grid_spec=grid_spec,
        out_shape=jax.ShapeDtypeStruct((B, S, H * LANES), jnp.bfloat16),
        compiler_params=_cparams(("parallel", "parallel", "arbitrary")),
        name="diff_attention",
    )(reach, qd, ka, vt, qpos, dabs, lq1, lk1, lq2, lk2, g_col)


def _diff_slopes_log2():
    return np.asarray([2.0 ** (-8.0 * (i + 1) / N_HEADS_DIFF) for i in range(N_HEADS_DIFF)],
                      np.float32) * np.float32(LOG2E)


def _position_columns():
    def bf16_round(v):
        u = np.asarray(v, np.float32).view(np.uint32)
        return ((u + 0x7FFF + ((u >> 16) & 1)) & np.uint32(0xFFFF0000)).view(np.float32)

    kpos = np.zeros((N_HEADS_DIFF, DIFF_TK, LANES), np.float32)
    qpos = np.zeros((N_HEADS_DIFF, DIFF_TQ, LANES), np.float32)
    pk, pq = np.arange(DIFF_TK), np.arange(DIFF_TQ)
    for h, a in enumerate(_diff_slopes_log2()):
        rest = np.float32(a)
        for n in range(3):
            a_n = np.float32(bf16_round(rest))
            rest = np.float32(rest - a_n)
            c = 4 * n
            kpos[h, :, c + 0] = (pk // POS_SPLIT) * POS_SPLIT
            kpos[h, :, c + 1] = pk % POS_SPLIT
            kpos[h, :, c + 2] = a_n
            kpos[h, :, c + 3] = a_n
            qpos[h, :, c + 0] = a_n
            qpos[h, :, c + 1] = a_n
            qpos[h, :, c + 2] = -((pq // POS_SPLIT) * POS_SPLIT)
            qpos[h, :, c + 3] = -(pq % POS_SPLIT)
    dabs = -np.abs(pq[:, None] - pk[None, :]).astype(np.float32)
    return jnp.asarray(kpos, jnp.bfloat16), jnp.asarray(qpos, jnp.bfloat16), jnp.asarray(dabs)


def _post_attn_kernel(o1_ref, l1_ref, o4_ref, l4_ref, o16_ref, l16_ref, od_ref, x_ref, wo_ref, g_ref, b_ref,
                      wr_ref, br_ref, tri_ref,
                      x1_ref, x1t_ref, idx_ref, gate_ref, rank_ref, cnt_ref, carry_scr, order_scr):
    step = pl.program_id(0)

    @pl.when(step == 0)
    def _():
        carry_scr[...] = jnp.zeros(carry_scr.shape, jnp.float32)

    tm = x_ref.shape[0]
    nc = DIL_WIDTH // LANES
    for n, (o_ref, l_ref) in enumerate(((o4_ref, l4_ref), (o16_ref, l16_ref))):
        d = o_ref.shape[1]
        for r in range(d):
            o_r = o_ref[0, r].astype(jnp.float32)
            l_r = l_ref[0, r]
            for c in range(nc):
                cols = slice(c * LANES, (c + 1) * LANES)
                order_scr[2 * n, c, pl.ds(r, tm // d, stride=d), :] = o_r[:, cols]
                order_scr[2 * n + 1, c, pl.ds(r, tm // d, stride=d), :] = l_r[:, cols]
    in_order = lambda n: jnp.concatenate([order_scr[n, c] for c in range(nc)], axis=1)
    outs = (o1_ref[0, 0].astype(jnp.float32), in_order(0), in_order(2))
    lses = (l1_ref[0, 0], in_order(1), in_order(3))
    top = jnp.maximum(jnp.maximum(lses[0], lses[1]), lses[2])
    wts = [jnp.exp(l - top) for l in lses]
    oa = ((wts[0] * outs[0] + wts[1] * outs[1] + wts[2] * outs[2])
          / (wts[0] + wts[1] + wts[2])).astype(jnp.bfloat16)

    mix = _dot(oa, wo_ref[:DIL_WIDTH, :]) + _dot(od_ref[...], wo_ref[DIL_WIDTH:, :])
    x1 = _layer_norm(DEEPNORM_ALPHA * x_ref[...] + mix, g_ref[...], b_ref[...])
    x1_ref[...] = x1
    for s in range(ROW_SUBLANES):
        x1t_ref[pl.ds(s, x1.shape[0], stride=ROW_SUBLANES), :] = x1[:, s * LANES:(s + 1) * LANES]

    tm = x1.shape[0]
    x_hi = x1.astype(jnp.bfloat16)
    x_lo = (x1 - x_hi.astype(jnp.float32)).astype(jnp.bfloat16)
    lg = (_dot_nt(wr_ref[0], x_hi) + _dot_nt(wr_ref[1], x_hi) + _dot_nt(wr_ref[0], x_lo)
          + br_ref[...])
    eidx = lax.broadcasted_iota(jnp.int32, (N_EXPERTS, tm), 0)
    vals, sels = [], []
    for k in range(TOP_K):
        mx = jnp.max(lg, axis=0, keepdims=True)
        idx = jnp.min(jnp.where(lg == mx, eidx, N_EXPERTS), axis=0, keepdims=True)
        sel = eidx == idx
        vals.append(mx)
        sels.append(sel)
        idx_ref[k:k + 1, :] = idx
        lg = jnp.where(sel, -jnp.inf, lg)
    ex = [jnp.exp(v - vals[0]) for v in vals]
    den = ex[0] + ex[1] + ex[2] + ex[3]
    for k in range(TOP_K):
        gate_ref[k:k + 1, :] = ex[k] / den

    chosen = (sels[0] | sels[1] | sels[2] | sels[3])
    onehot = jnp.where(chosen, 1.0, 0.0)
    before = _dot(onehot.astype(jnp.bfloat16), tri_ref[...]) + carry_scr[...]
    for k in range(TOP_K):
        rank_ref[k:k + 1, :] = jnp.sum(jnp.where(sels[k], before, 0.0), axis=0,
                                       keepdims=True).astype(jnp.int32)
    carry_scr[...] = carry_scr[...] + jnp.sum(onehot, axis=1, keepdims=True)
    pad = jnp.zeros((SUBLANES - TOP_K, tm), jnp.int32)
    idx_ref[TOP_K:, :] = pad
    rank_ref[TOP_K:, :] = pad
    gate_ref[TOP_K:, :] = pad.astype(jnp.float32)
    cnt_ref[...] = jnp.broadcast_to(carry_scr[...], cnt_ref.shape).astype(jnp.int32)


def _post_attn(dil, od, x, wo_bf, g, b, wr_t, br, tri):
    T, D = x.shape
    assert D == ROW_SUBLANES * LANES
    tm = ROW_TILE
    nb = dil[0][0].shape[2] // tm
    row = lambda w: pl.BlockSpec((tm, w), lambda i: (i, 0))
    full = lambda a: pl.BlockSpec(a.shape, lambda i: (0,) * a.ndim)
    col = pl.BlockSpec((SUBLANES, tm), lambda i: (0, i))
    dil_specs, dil_args = [], []
    for o_d, lse_d in dil:
        d = o_d.shape[1]
        spec = pl.BlockSpec((1, d, tm // d, DIL_WIDTH), lambda i: (i // nb, 0, i % nb, 0))
        dil_specs += [spec, spec]
        dil_args += [o_d, lse_d]
    return pl.pallas_call(
        _post_attn_kernel,
        grid=(T // tm,),
        in_specs=dil_specs + [row(DIFF_WIDTH), row(D), full(wo_bf), full(g), full(b), full(wr_t),
                              full(br), full(tri)],
        out_specs=[row(D), pl.BlockSpec((tm * ROW_SUBLANES, LANES), lambda i: (i, 0)), col, col, col,
                   pl.BlockSpec((N_EXPERTS, LANES), lambda i: (0, 0))],
        out_shape=[
            jax.ShapeDtypeStruct((T, D), jnp.float32),
            jax.ShapeDtypeStruct((T * ROW_SUBLANES, LANES), jnp.float32),
            jax.ShapeDtypeStruct((SUBLANES, T), jnp.int32),
            jax.ShapeDtypeStruct((SUBLANES, T), jnp.float32),
            jax.ShapeDtypeStruct((SUBLANES, T), jnp.int32),
            jax.ShapeDtypeStruct((N_EXPERTS, LANES), jnp.int32),
        ],
        scratch_shapes=[pltpu.VMEM((N_EXPERTS, 1), jnp.float32),
                        pltpu.VMEM((4, DIL_WIDTH // LANES, tm, LANES), jnp.float32)],
        compiler_params=_cparams(("arbitrary",)),
        name="post_attn_router",
    )(*dil_args, od, x, wo_bf, g, b, wr_t, br, tri)


def _index_copy(dest_hbm, step, idx_smem, sem):
    per_step = dest_hbm.shape[1]
    half = pl.multiple_of((step % 2) * per_step, per_step)
    return pltpu.make_async_copy(dest_hbm.at[step], idx_smem.at[pl.ds(half, per_step)], sem)


def _stage_index(dest_hbm, step, n_steps, idx_smem, sem):
    @pl.when(step == 0)
    def _():
        _index_copy(dest_hbm, 0, idx_smem, sem).start()

    _index_copy(dest_hbm, step, idx_smem, sem).wait()

    @pl.when(step + 1 < n_steps)
    def _():
        _index_copy(dest_hbm, step + 1, idx_smem, sem).start()

    return (step % 2) * dest_hbm.shape[1]


def _row_tile(ref, row):
    return ref.at[pl.ds(pl.multiple_of(row * ROW_SUBLANES, ROW_SUBLANES), ROW_SUBLANES)]


def _rows_from_tiles(ref, n):
    return jnp.concatenate([ref[pl.ds(s, n, stride=ROW_SUBLANES), :] for s in range(ROW_SUBLANES)], axis=1)


def _rows_to_tiles(ref, rows):
    for s in range(ROW_SUBLANES):
        ref[pl.ds(s, rows.shape[0], stride=ROW_SUBLANES), :] = rows[:, s * LANES:(s + 1) * LANES]


def _dispatch_kernel(pad_start_ref, n_pad_ref, n_used_ref, dest_hbm, x_ref, xs_hbm, idx_smem, idx_sem, row_sem,
                     zero_scr, zero_sem):
    tm = x_ref.shape[0] // ROW_SUBLANES
    tile_rows = MOE_TILE * ROW_SUBLANES
    n_tiles = xs_hbm.shape[0] // tile_rows

    def pad_copies(start_not_wait):
        for j in range(N_EXPERTS):
            tile = n_used_ref[0] + j

            @pl.when(tile < n_tiles)
            def _():
                dst = xs_hbm.at[pl.ds(pl.multiple_of(tile * tile_rows, tile_rows), tile_rows)]
                cp = pltpu.make_async_copy(zero_scr, dst, zero_sem)
                cp.start() if start_not_wait else cp.wait()

        for e in range(N_EXPERTS):
            n_pad = n_pad_ref[e]
            for bit in reversed(range(PAD_BITS)):
                size = 1 << bit

                @pl.when((n_pad & size) != 0)
                def _():
                    row = pad_start_ref[e] + (n_pad & ~(2 * size - 1))
                    dst = xs_hbm.at[pl.ds(pl.multiple_of(row * ROW_SUBLANES, ROW_SUBLANES), size * ROW_SUBLANES)]
                    cp = pltpu.make_async_copy(zero_scr.at[pl.ds(0, size * ROW_SUBLANES)], dst, zero_sem)
                    cp.start() if start_not_wait else cp.wait()

    @pl.when(pl.program_id(0) == 0)
    def _():
        zero_scr[...] = jnp.zeros(zero_scr.shape, zero_scr.dtype)
        pad_copies(True)
        pad_copies(False)

    base = _stage_index(dest_hbm, pl.program_id(0), pl.num_programs(0), idx_smem, idx_sem)

    def issue(t, carry):
        for k in range(TOP_K):
            pltpu.make_async_copy(_row_tile(x_ref, t), _row_tile(xs_hbm, idx_smem[base + k * tm + t]),
                                  row_sem).start(priority=k % 2)
        return carry

    lax.fori_loop(0, tm, issue, 0, unroll=8)
    for k in range(TOP_K):
        pltpu.make_async_copy(x_ref, xs_hbm.at[pl.ds(0, tm * ROW_SUBLANES)], row_sem).wait()


def _dispatch(pad_start, n_pad, n_used, dest_steps, x1t, n_rows):
    n_steps, per_step = dest_steps.shape
    tm = per_step // TOP_K
    assert MOE_TILE == 1 << PAD_BITS
    grid_spec = pltpu.PrefetchScalarGridSpec(
        num_scalar_prefetch=3,
        grid=(n_steps,),
        in_specs=[
            pl.BlockSpec(memory_space=pl.ANY),
            pl.BlockSpec((tm * ROW_SUBLANES, LANES), lambda i, ps, npad, nu: (i, 0)),
        ],
        out_specs=pl.BlockSpec(memory_space=pl.ANY),
        scratch_shapes=[pltpu.SMEM((2 * per_step,), jnp.int32), pltpu.SemaphoreType.DMA(()),
                        pltpu.SemaphoreType.DMA(()),
                        pltpu.VMEM((MOE_TILE * ROW_SUBLANES, LANES), jnp.float32),
                        pltpu.SemaphoreType.DMA(())],
    )
    return pl.pallas_call(
        _dispatch_kernel,
        grid_spec=grid_spec,
        out_shape=jax.ShapeDtypeStruct((n_rows * ROW_SUBLANES, LANES), jnp.float32),
        compiler_params=_cparams(("arbitrary",)),
        name="dispatch_rows",
    )(pad_start, n_pad, n_used, dest_steps, x1t)


def _expert_kernel(tile_e_ref, n_used_ref, xs_ref, wu_ref, bu_ref, wd_ref, bd_ref, y_ref, wu_scr, wd_scr):
    i = pl.program_id(0)
    used = i < n_used_ref[0]
    new_expert = (i == 0) | (tile_e_ref[i] != tile_e_ref[jnp.maximum(i - 1, 0)])
    tm = xs_ref.shape[0] // ROW_SUBLANES

    @pl.when(used & new_expert)
    def _():
        wu_scr[...] = wu_ref[0].astype(jnp.bfloat16)
        wd_scr[...] = wd_ref[0].astype(jnp.bfloat16)

    @pl.when(used)
    def _():
        xs = _rows_from_tiles(xs_ref, tm).astype(jnp.bfloat16)
        hu = _dot(xs, wu_scr[...]) + bu_ref[0]
        g = jnp.minimum(hu[:, :D_FF], SWIGLU_LIMIT)
        u = jnp.clip(hu[:, D_FF:], -SWIGLU_LIMIT, SWIGLU_LIMIT)
        act = g * (1.0 / (1.0 + jnp.exp(-SWIGLU_ALPHA * g))) * (u + 1.0)
        _rows_to_tiles(y_ref, _dot(act.astype(jnp.bfloat16), wd_scr[...]) + bd_ref[0])

    @pl.when(jnp.logical_not(used))
    def _():
        y_ref[...] = jnp.zeros(y_ref.shape, y_ref.dtype)


def _experts(tile_e, n_used, xs, wu, bu, wd, bd):
    D = ROW_SUBLANES * LANES
    n_rows = xs.shape[0] // ROW_SUBLANES
    tm = MOE_TILE
    tile_spec = pl.BlockSpec((tm * ROW_SUBLANES, LANES), lambda i, te, nu: (i, 0))
    grid_spec = pltpu.PrefetchScalarGridSpec(
        num_scalar_prefetch=2,
        grid=(n_rows // tm,),
        in_specs=[
            pl.BlockSpec((tm * ROW_SUBLANES, LANES), lambda i, te, nu: (jnp.minimum(i, nu[0] - 1), 0)),
            pl.BlockSpec((1, D, 2 * D_FF), lambda i, te, nu: (te[i], 0, 0)),
            pl.BlockSpec((1, 1, 2 * D_FF), lambda i, te, nu: (te[i], 0, 0)),
            pl.BlockSpec((1, D_FF, D), lambda i, te, nu: (te[i], 0, 0)),
            pl.BlockSpec((1, 1, D), lambda i, te, nu: (te[i], 0, 0)),
        ],
        out_specs=tile_spec,
        scratch_shapes=[pltpu.VMEM((D, 2 * D_FF), jnp.bfloat16), pltpu.VMEM((D_FF, D), jnp.bfloat16)],
    )
    return pl.pallas_call(
        _expert_kernel,
        grid_spec=grid_spec,
        out_shape=jax.ShapeDtypeStruct(xs.shape, jnp.float32),
        compiler_params=_cparams(("arbitrary",)),
        name="experts",
    )(tile_e, n_used, xs, wu, bu, wd, bd)


def _combine_kernel(dest_hbm, y_hbm, x1_ref, gate_ref, g_ref, b_ref, o_ref, buf, idx_smem, idx_sem, row_sems):
    i = pl.program_id(0)
    n = pl.num_programs(0)
    tm = x1_ref.shape[0]

    def issue(step, slot):
        base = _stage_index(dest_hbm, step, n, idx_smem, idx_sem)

        def body(t, carry):
            for k in range(TOP_K):
                pltpu.make_async_copy(_row_tile(y_hbm, idx_smem[base + k * tm + t]),
                                      _row_tile(buf.at[slot, k], t), row_sems.at[slot]).start(priority=k % 2)
            return carry

        lax.fori_loop(0, tm, body, 0, unroll=8)

    @pl.when(i == 0)
    def _():
        issue(i, 0)

    @pl.when(i + 1 < n)
    def _():
        issue(i + 1, (i + 1) % 2)

    slot = i % 2
    for k in range(TOP_K):
        pltpu.make_async_copy(y_hbm.at[pl.ds(0, tm * ROW_SUBLANES)], buf.at[slot, k], row_sems.at[slot]).wait()
    ffn = _rows_from_tiles(buf.at[slot, 0], tm) * gate_ref[:, 0:1]
    for k in range(1, TOP_K):
        ffn = ffn + _rows_from_tiles(buf.at[slot, k], tm) * gate_ref[:, k:k + 1]
    o_ref[...] = _layer_norm(DEEPNORM_ALPHA * x1_ref[...] + ffn, g_ref[...], b_ref[...])


def _combine(dest_steps, y, x1, gates_t, g, b):
    T, D = x1.shape
    n_steps, per_step = dest_steps.shape
    tm = per_step // TOP_K
    full = lambda a: pl.BlockSpec(a.shape, lambda i: (0, 0))
    return pl.pallas_call(
        _combine_kernel,
        grid=(n_steps,),
        in_specs=[
            pl.BlockSpec(memory_space=pl.ANY),
            pl.BlockSpec(memory_space=pl.ANY),
            pl.BlockSpec((tm, D), lambda i: (i, 0)),
            pl.BlockSpec((tm, SUBLANES), lambda i: (i, 0)),
            full(g), full(b),
        ],
        out_specs=pl.BlockSpec((tm, D), lambda i: (i, 0)),
        out_shape=jax.ShapeDtypeStruct((T, D), jnp.float32),
        scratch_shapes=[pltpu.VMEM((2, TOP_K, tm * ROW_SUBLANES, LANES), jnp.float32),
                        pltpu.SMEM((2 * per_step,), jnp.int32),
                        pltpu.SemaphoreType.DMA(()), pltpu.SemaphoreType.DMA((2,))],
        compiler_params=_cparams(("arbitrary",)),
        name="combine_ln2",
    )(dest_steps, y, x1, gates_t, g, b)


def kernel(x, w_in, w_out, lambda_q1, lambda_k1, lambda_q2, lambda_k2, diff_norm_g, ln1_g, ln1_b,
           w_router, b_router, w_up, b_up, w_down, b_down, ln2_g, ln2_b):
    B, S, D = x.shape
    T = B * S
    assert w_in.shape[0] == 1, "single layer"
    assert S % DIFF_TQ == 0 and DIFF_TK == DIFF_TQ and DIFF_TK % ROW_TILE == 0 and S % (16 * DIL_TQ) == 0

    na = 3 * DIL_WIDTH
    colscale = np.ones((3 * D,), np.float32)
    colscale[:DIL_WIDTH] = HEAD_DIM ** -0.5
    colscale[na:na + DIFF_WIDTH] = HEAD_DIM ** -0.5 * LOG2E
    w_in_bf = (w_in[0] * colscale).astype(jnp.bfloat16)
    nv = na + 2 * DIFF_WIDTH

    kpos, qpos, dabs = _position_columns()
    pa, pa4, pa16, qd, ka, vt, qn, kn = _in_proj(x, w_in_bf[:, :nv], w_in_bf[:, nv:].T, kpos)
    dil = [_dilated_stage(p, d) for p, (_, d) in zip((pa[:, None], pa4, pa16), DIL_PATTERNS)]

    r2 = lambda a: a.reshape(1, -1).astype(jnp.float32)
    od = _diff_attention(_tile_reach(qn, kn, S // DIFF_TK), qd, ka, vt, qpos, dabs, r2(lambda_q1[0]), r2(lambda_k1[0]), r2(lambda_q2[0]),
                         r2(lambda_k2[0]), diff_norm_g[0].reshape(-1, 1).astype(jnp.float32))

    tri = jnp.asarray(np.triu(np.ones((ROW_TILE, ROW_TILE), np.float32), k=1), jnp.bfloat16)
    n_rows = T * TOP_K + N_EXPERTS * MOE_TILE
    n_tiles = n_rows // MOE_TILE
    wr_t = w_router[0].T.astype(jnp.float32)
    wr_hi = wr_t.astype(jnp.bfloat16)
    wr_split = jnp.stack([wr_hi, (wr_t - wr_hi.astype(jnp.float32)).astype(jnp.bfloat16)])
    x1, x1t, top_idx, gates, rank, counts = _post_attn(
        dil, od.reshape(T, DIFF_WIDTH), x.reshape(T, D), w_out[0].astype(jnp.bfloat16),
        r2(ln1_g[0]), r2(ln1_b[0]), wr_split, b_router[0].reshape(N_EXPERTS, 1), tri)

    counts = counts[:, 0]
    pcounts = ((counts + MOE_TILE - 1) // MOE_TILE) * MOE_TILE
    pends = jnp.cumsum(pcounts)
    pstarts = pends - pcounts
    dest = rank[:TOP_K]
    for e in range(N_EXPERTS):
        dest = dest + jnp.where(top_idx[:TOP_K] == e, pstarts[e], 0)
    tile_starts = jnp.arange(n_tiles, dtype=jnp.int32) * MOE_TILE
    tile_e = jnp.minimum(jnp.sum(pends[None, :] <= tile_starts[:, None], axis=1), N_EXPERTS - 1).astype(jnp.int32)
    n_used = (pends[-1] // MOE_TILE).astype(jnp.int32).reshape(1)

    def per_step(tm):
        return dest.reshape(TOP_K, T // tm, tm).transpose(1, 0, 2).reshape(T // tm, TOP_K * tm)

    xs = _dispatch((pstarts + counts).astype(jnp.int32), (pcounts - counts).astype(jnp.int32), n_used,
                   per_step(ROW_TILE), x1t, n_rows)
    y = _experts(tile_e, n_used, xs, w_up[0], b_up[0][:, None, :], w_down[0], b_down[0][:, None, :])
    out = _combine(per_step(COMBINE_TILE), y, x1, gates.T, r2(ln2_g[0]), r2(ln2_b[0]))
    return out.reshape(B, S, D)
```

```python
import functools
import math

import jax
import jax.numpy as jnp
import numpy as np
from jax import lax
from jax.experimental import pallas as pl
from jax.experimental.pallas import tpu as pltpu

D_MODEL = 1024
HEAD_DIM = 64
DIL_WIDTH = 512
N_HEADS_DIL = 8
DIL_PATTERNS = ((128, 1), (512, 4), (2048, 16))
DIL_SIDE = 64
DIFF_WIDTH = 512
N_HEADS_DIFF = 4
DIFF_VDIM = 2 * HEAD_DIM
N_EXPERTS = 32
TOP_K = 4
D_FF = D_MODEL
SWIGLU_ALPHA = 1.702
SWIGLU_LIMIT = 7.0
DEEPNORM_ALPHA = 2.0 ** 0.25
LN_EPS = 1e-5
NEG_INF = -1e30
LAM_INIT = 0.8 - 0.6 * math.exp(-0.3 * 0)
LOG2E = math.log2(math.e)

LANES = 128
SUBLANES = 8
BF16_SUBLANES = 16
VMEM_LIMIT = 56 * 1024 * 1024

ROW_TILE = 512
DIFF_TQ = 1024
DIFF_TK = 1024
V_ROWS = DIFF_VDIM + BF16_SUBLANES
EXP2_HEADROOM = 4.0
UNDERFLOW_LOG2 = 160.0
NORM_SLACK = 1.01
DIL_TQ = 128
DIL_QB = 8
MOE_TILE = 512
COMBINE_TILE = 256
ROW_SUBLANES = D_MODEL // LANES
PAD_BITS = MOE_TILE.bit_length() - 1
POS_SPLIT = 16


def _cparams(sem):
    return pltpu.CompilerParams(dimension_semantics=sem, vmem_limit_bytes=VMEM_LIMIT)


def _dot_nt(a, b, **kw):
    return lax.dot_general(a, b, (((1,), (1,)), ((), ())), preferred_element_type=jnp.float32, **kw)


def _dot(a, b, **kw):
    return jnp.dot(a, b, preferred_element_type=jnp.float32, **kw)


def _layer_norm(y, g, b):
    mu = jnp.mean(y, axis=-1, keepdims=True)
    yc = y - mu
    var = jnp.mean(yc * yc, axis=-1, keepdims=True)
    return yc * lax.rsqrt(var + LN_EPS) * g + b


def _in_proj_kernel(x_ref, w_ref, wvt_ref, kpos_ref, grp_ref, pa_ref, pa4_ref, pa16_ref, qd_ref, ka_ref,
                    vt_ref, qn_ref, kn_ref, pa_scr):
    xb = x_ref[0].astype(jnp.bfloat16)
    na = 3 * DIL_WIDTH
    tm = xb.shape[0]
    paf = _dot(xb, w_ref[:, :na])
    pa_ref[0] = paf.astype(jnp.bfloat16)
    for c in range(na // LANES):
        cols = slice(c * LANES, (c + 1) * LANES)
        pa_scr[c] = paf[:, cols]
        for d, ref in ((DIL_PATTERNS[1][1], pa4_ref), (DIL_PATTERNS[2][1], pa16_ref)):
            for r in range(d):
                ref[0, r, :, cols] = pa_scr[c, pl.ds(r, tm // d, stride=d), :].astype(jnp.bfloat16)
    q = _dot(xb, w_ref[:, na:na + DIFF_WIDTH]).astype(jnp.bfloat16)
    k = _dot(xb, w_ref[:, na + DIFF_WIDTH:]).astype(jnp.bfloat16)
    vt = _dot_nt(wvt_ref[...], xb).astype(jnp.bfloat16)
    ones = jnp.ones((V_ROWS - DIFF_VDIM, tm), jnp.bfloat16)

    @pl.when(pl.program_id(1) == 0)
    def _():
        qn_ref[...] = jnp.zeros(qn_ref.shape, jnp.float32)
        kn_ref[...] = jnp.zeros(kn_ref.shape, jnp.float32)

    for src, dst in ((q, qn_ref), (k, kn_ref)):
        f = src.astype(jnp.float32)
        gs = _dot((f * f).astype(jnp.bfloat16), grp_ref[...])
        dst[0] = jnp.maximum(dst[0], jnp.max(gs, axis=0, keepdims=True))
    for h in range(N_HEADS_DIFF):
        sl = slice(h * LANES, (h + 1) * LANES)
        qd_ref[0, h] = q[:, sl]
        ka_ref[0, h, :, :LANES] = k[:, sl]
        ka_ref[0, h, :, LANES:] = kpos_ref[h]
        vt_ref[0, h, :DIFF_VDIM, :] = vt[sl, :]
        vt_ref[0, h, DIFF_VDIM:, :] = ones


def _in_proj(x, w_bf, wvt_bf, kpos):
    B, S, D = x.shape
    tm = ROW_TILE
    na = 3 * DIL_WIDTH
    grp = np.zeros((DIFF_WIDTH, LANES), np.float32)
    grp[np.arange(DIFF_WIDTH), np.arange(DIFF_WIDTH) // HEAD_DIM] = 1.0
    grp = jnp.asarray(grp, jnp.bfloat16)
    norm_spec = pl.BlockSpec((1, 1, LANES), lambda b, i: (b, 0, 0))
    d4, d16 = DIL_PATTERNS[1][1], DIL_PATTERNS[2][1]
    return pl.pallas_call(
        _in_proj_kernel,
        grid=(B, S // tm),
        in_specs=[
            pl.BlockSpec((1, tm, D), lambda b, i: (b, i, 0)),
            pl.BlockSpec(w_bf.shape, lambda b, i: (0, 0)),
            pl.BlockSpec(wvt_bf.shape, lambda b, i: (0, 0)),
            pl.BlockSpec((N_HEADS_DIFF, tm, LANES), lambda b, i: (0, i % (DIFF_TK // tm), 0)),
            pl.BlockSpec(grp.shape, lambda b, i: (0, 0)),
        ],
        out_specs=[
            pl.BlockSpec((1, tm, na), lambda b, i: (b, i, 0)),
            pl.BlockSpec((1, d4, tm // d4, na), lambda b, i: (b, 0, i, 0)),
            pl.BlockSpec((1, d16, tm // d16, na), lambda b, i: (b, 0, i, 0)),
            pl.BlockSpec((1, N_HEADS_DIFF, tm, LANES), lambda b, i: (b, 0, i, 0)),
            pl.BlockSpec((1, N_HEADS_DIFF, tm, 2 * LANES), lambda b, i: (b, 0, i, 0)),
            pl.BlockSpec((1, N_HEADS_DIFF, V_ROWS, tm), lambda b, i: (b, 0, 0, i)),
            norm_spec, norm_spec,
        ],
        out_shape=[
            jax.ShapeDtypeStruct((B, S, na), jnp.bfloat16),
            jax.ShapeDtypeStruct((B, d4, S // d4, na), jnp.bfloat16),
            jax.ShapeDtypeStruct((B, d16, S // d16, na), jnp.bfloat16),
            jax.ShapeDtypeStruct((B, N_HEADS_DIFF, S, LANES), jnp.bfloat16),
            jax.ShapeDtypeStruct((B, N_HEADS_DIFF, S, 2 * LANES), jnp.bfloat16),
            jax.ShapeDtypeStruct((B, N_HEADS_DIFF, V_ROWS, S), jnp.bfloat16),
            jax.ShapeDtypeStruct((B, 1, LANES), jnp.float32),
            jax.ShapeDtypeStruct((B, 1, LANES), jnp.float32),
        ],
        scratch_shapes=[pltpu.VMEM((na // LANES, tm, LANES), jnp.float32)],
        compiler_params=_cparams(("parallel", "arbitrary")),
        name="in_proj",
    )(x, w_bf, wvt_bf, kpos, grp)


def _dilated_kernel(*refs, seq_len):
    n_halo = 2 * DIL_QB + 2
    q_ref, k_refs, v_refs = refs[0], refs[1:1 + n_halo], refs[1 + n_halo:1 + 2 * n_halo]
    bias_ref, o_ref, lse_ref = refs[1 + 2 * n_halo:]
    tq = DIL_TQ
    nk = tq + 2 * DIL_SIDE
    i = pl.program_id(2)
    q = q_ref[0, 0]
    kwin = jnp.concatenate([r[0, 0] for r in k_refs], axis=0)
    vwin = jnp.concatenate([r[0, 0] for r in v_refs], axis=0)
    low_half = lax.broadcasted_iota(jnp.int32, (tq, LANES), 1) < HEAD_DIM
    ss = []
    for u in range(DIL_QB):
        key_pos = (i * DIL_QB + u) * tq - DIL_SIDE + lax.broadcasted_iota(jnp.int32, (1, nk), 1)
        edge = jnp.where((key_pos >= 0) & (key_pos < seq_len), 0.0, NEG_INF)
        for h in range(N_HEADS_DIL):
            sl = slice((h // 2) * LANES, (h // 2 + 1) * LANES)
            qu = q[u * tq:(u + 1) * tq, sl]
            qm = jnp.where(low_half == (h % 2 == 0), qu, jnp.zeros((tq, LANES), q.dtype))
            ss.append(_dot_nt(qm, kwin[u * tq:u * tq + nk, sl]) + bias_ref[h] + edge)
    s = jnp.concatenate(ss, axis=0)
    m = jnp.max(s, axis=-1, keepdims=True)
    p = jnp.exp(s - m)
    l = jnp.sum(p, axis=-1, keepdims=True)
    lse = m + jnp.log(l)
    inv_l = 1.0 / l
    pb = p.astype(jnp.bfloat16)
    for u in range(DIL_QB):
        for pair in range(N_HEADS_DIL // 2):
            sl = slice(pair * LANES, (pair + 1) * LANES)
            base = (u * N_HEADS_DIL + 2 * pair) * tq
            r0 = slice(base, base + tq)
            r1 = slice(base + tq, base + 2 * tq)
            vu = vwin[u * tq:u * tq + nk, sl]
            a0 = _dot(pb[r0], vu) * inv_l[r0]
            a1 = _dot(pb[r1], vu) * inv_l[r1]
            rows = slice(u * tq, (u + 1) * tq)
            o_ref[0, 0, rows, sl] = jnp.where(low_half, a0, a1).astype(o_ref.dtype)
            lse_ref[0, 0, rows, sl] = jnp.where(low_half, lse[r0], lse[r1])


def _dilated_stage(pa_d, dilation):
    B, d, L, _ = pa_d.shape
    tb = DIL_QB * DIL_TQ
    w = DIL_WIDTH
    halo = DIL_SIDE
    n_halo = L // halo
    per_step = tb // halo + 2
    bias = _dilated_bias(dilation)
    assert L % tb == 0

    def halo_spec(col, j):
        def imap(b, r, i):
            return (b, r, jnp.clip(i * (tb // halo) - 1 + j, 0, n_halo - 1), col)
        return pl.BlockSpec((1, 1, halo, w), imap)

    out_spec = pl.BlockSpec((1, 1, tb, w), lambda b, r, i: (b, r, i, 0))
    in_specs = ([pl.BlockSpec((1, 1, tb, w), lambda b, r, i: (b, r, i, 0))]
                + [halo_spec(1, j) for j in range(per_step)] + [halo_spec(2, j) for j in range(per_step)]
                + [pl.BlockSpec(bias.shape, lambda b, r, i: (0, 0, 0))])
    return pl.pallas_call(
        functools.partial(_dilated_kernel, seq_len=L),
        grid=(B, d, L // tb),
        in_specs=in_specs,
        out_specs=[out_spec, out_spec],
        out_shape=[jax.ShapeDtypeStruct((B, d, L, w), jnp.bfloat16),
                   jax.ShapeDtypeStruct((B, d, L, w), jnp.float32)],
        compiler_params=_cparams(("parallel", "parallel", "parallel")),
        name=f"dilated_d{d}",
    )(*([pa_d] * (1 + 2 * per_step)), bias)


def _dilated_bias(dilation):
    tq = DIL_TQ
    slopes = np.asarray([2.0 ** (-8.0 * (i + 1) / N_HEADS_DIL) for i in range(N_HEADS_DIL)], np.float32)
    rel = (np.arange(tq + 2 * DIL_SIDE)[None, :] - DIL_SIDE) - np.arange(tq)[:, None]
    band = np.abs(rel) <= DIL_SIDE
    pen = -(slopes * dilation)[:, None, None] * np.abs(rel).astype(np.float32)[None]
    return jnp.asarray(np.where(band[None], pen, np.float32(NEG_INF)), jnp.float32)


def _diff_kernel(reach_ref, q_ref, ka_hbm, vt_hbm, qpos_ref, dabs_ref, lq1_ref, lk1_ref, lq2_ref, lk2_ref,
                 g_ref, o_ref, qa_scr, m_scr, acc_scr, k_buf, v_buf, sems, *, slopes, n_key_tiles):
    b = pl.program_id(0)
    h = pl.program_id(1)
    qi = pl.program_id(2)
    tq = q_ref.shape[2]
    tk = k_buf.shape[1]

    slope = jnp.float32(slopes[0])
    for hh in range(1, N_HEADS_DIFF):
        slope = jnp.where(h == hh, jnp.float32(slopes[hh]), slope)

    reach = reach_ref[b * N_HEADS_DIFF + h]
    first = jnp.maximum(qi - reach, 0)
    n_active = jnp.minimum(qi + reach, n_key_tiles - 1) - first + 1

    def key_tile(j):
        t = first + j - 1
        return jnp.where(j == 0, qi, jnp.where(t < qi, t, t + 1))

    def tile_copies(j, slot):
        start = pl.multiple_of(key_tile(j) * tk, tk)
        return (pltpu.make_async_copy(ka_hbm.at[b, h, pl.ds(start, tk), :], k_buf.at[slot], sems.at[0, slot]),
                pltpu.make_async_copy(vt_hbm.at[b, h, :, pl.ds(start, tk)], v_buf.at[slot], sems.at[1, slot]))

    for cp in tile_copies(0, 0):
        cp.start()

    @pl.when(n_active > 1)
    def _():
        for cp in tile_copies(1, 1):
            cp.start()

    q = q_ref[0, 0]
    low_half = lax.broadcasted_iota(jnp.int32, (tq, LANES), 1) < HEAD_DIM
    zero = jnp.zeros_like(q)
    qpos = qpos_ref[0]
    for c in range(2):
        qc = jnp.where(low_half == (c == 0), q, zero)
        for var, pos in enumerate((qpos, -qpos, jnp.zeros_like(qpos))):
            qa_scr[c, var, :, :LANES] = qc
            qa_scr[c, var, :, LANES:] = pos
    m_scr[...] = jnp.full(m_scr.shape, NEG_INF, jnp.float32)
    acc_scr[0] = jnp.zeros(acc_scr.shape[1:], jnp.float32)


    def restabilise(slot, kt, diag, cur):
        var = 2 if diag else jnp.where(kt < qi, 0, 1)
        off = -slope * jnp.abs(qi * tq - kt * tk).astype(jnp.float32)
        ka = k_buf[slot]
        vt = v_buf[slot]
        ss = []
        for c in range(2):
            s = _dot_nt(ka, qa_scr[c, var])
            if diag:
                s = s + dabs_ref[0]
            ss.append(s)
        for c in range(2):
            s = ss[c]
            m_old = m_scr[c]
            m_new = jnp.maximum(m_old, jnp.max(s, axis=0, keepdims=True) + off)
            alpha = jnp.exp2(m_old - m_new)
            p = jnp.exp2((s - (m_new - off)).astype(jnp.bfloat16))
            acc_scr[cur, c] = alpha * acc_scr[cur, c] + _dot(vt, p)
            m_scr[c] = m_new

    for cp in tile_copies(0, 0):
        cp.wait()
    restabilise(0, qi, True, 0)

    def visit(j, cur):
        slot = j % 2

        @pl.when(j + 1 < n_active)
        def _():
            for cp in tile_copies(j + 1, 1 - slot):
                cp.start()

        for cp in tile_copies(j, slot):
            cp.wait()
        kt = key_tile(j)
        var = jnp.where(kt < qi, 0, 1)
        off = -slope * jnp.abs(qi * tq - kt * tk).astype(jnp.float32)
        ka = k_buf[slot]
        vt = v_buf[slot]
        excess = None
        for c in range(2):
            s = _dot_nt(ka, qa_scr[c, var])
            m_eff = m_scr[c] - off
            over = jnp.max(jnp.max(s, axis=0, keepdims=True) - m_eff)
            excess = over if excess is None else jnp.maximum(excess, over)
            p = jnp.exp2((s - m_eff).astype(jnp.bfloat16))
            acc_scr[1 - cur, c] = acc_scr[cur, c] + _dot(vt, p)

        @pl.when(excess > EXP2_HEADROOM)
        def _():
            restabilise(slot, kt, False, cur)

        return jnp.where(excess <= EXP2_HEADROOM, 1 - cur, cur)

    cur = lax.fori_loop(1, n_active, visit, jnp.int32(0))

    lam = (jnp.exp(jnp.sum(lq1_ref[...] * lk1_ref[...], axis=-1, keepdims=True))
           - jnp.exp(jnp.sum(lq2_ref[...] * lk2_ref[...], axis=-1, keepdims=True)) + LAM_INIT)
    a1, a2 = acc_scr[cur, 0], acc_scr[cur, 1]
    o = (a1[:DIFF_VDIM] / a1[DIFF_VDIM:DIFF_VDIM + 1]
         - lam * (a2[:DIFF_VDIM] / a2[DIFF_VDIM:DIFF_VDIM + 1]))
    o = o * lax.rsqrt(jnp.mean(o * o, axis=0, keepdims=True) + LN_EPS) * g_ref[...]
    o_ref[0] = (o * (1.0 - LAM_INIT)).T.astype(o_ref.dtype)


def _tile_reach(qn, kn, n_tiles):
    B = qn.shape[0]
    ng = 2 * N_HEADS_DIFF
    qmax = jnp.sqrt(qn[:, 0, :ng]) * NORM_SLACK
    kmax = jnp.sqrt(kn[:, 0, :ng]) * NORM_SLACK
    bound = jnp.max((2.0 * qmax * kmax).reshape(B, N_HEADS_DIFF, 2), axis=-1) + UNDERFLOW_LOG2
    far = bound / jnp.asarray(_diff_slopes_log2())[None, :]
    reach = jnp.floor((far - 1.0) / DIFF_TK) + 1.0
    reach = jnp.where(jnp.isfinite(reach), reach, n_tiles)
    return jnp.clip(reach, 0, n_tiles).astype(jnp.int32).reshape(-1)


def _diff_attention(reach, qd, ka, vt, qpos, dabs, lq1, lk1, lq2, lk2, g_col):
    B, H, S, _ = qd.shape
    tq, tk = DIFF_TQ, DIFF_TK
    slopes = tuple(float(s) for s in _diff_slopes_log2())
    nk = S // tk
    small = lambda a: pl.BlockSpec(a.shape, lambda b, h, qi, reach: (0, 0))
    grid_spec = pltpu.PrefetchScalarGridSpec(
        num_scalar_prefetch=1,
        grid=(B, H, S // tq),
        in_specs=[
            pl.BlockSpec((1, 1, tq, LANES), lambda b, h, qi, reach: (b, h, qi, 0)),
            pl.BlockSpec(memory_space=pl.ANY),
            pl.BlockSpec(memory_space=pl.ANY),
            pl.BlockSpec((1, tq, LANES), lambda b, h, qi, reach: (h, 0, 0)),
            pl.BlockSpec((1, tq, tk), lambda b, h, qi, reach: (h, 0, 0)),
            small(lq1), small(lk1), small(lq2), small(lk2), small(g_col),
        ],
        out_specs=pl.BlockSpec((1, tq, LANES), lambda b, h, qi, reach: (b, qi, h)),
        scratch_shapes=[
            pltpu.VMEM((2, 3, tq, 2 * LANES), jnp.bfloat16),
            pltpu.VMEM((2, 1, tq), jnp.float32),
            pltpu.VMEM((2, 2, V_ROWS, tq), jnp.float32),
            pltpu.VMEM((2, tk, 2 * LANES), jnp.bfloat16),
            pltpu.VMEM((2, V_ROWS, tk), jnp.bfloat16),
            pltpu.SemaphoreType.DMA((2, 2)),
        ],
    )
    return pl.pallas_call(
        functools.partial(_diff_kernel, slopes=slopes, n_key_tiles=nk),
        grid_spec=grid_spec,
        out_shape=jax.ShapeDtypeStruct((B, S, H * LANES), jnp.bfloat16),
        compiler_params=_cparams(("parallel", "parallel", "arbitrary")),
        name="diff_attention",
    )(reach, qd, ka, vt, qpos, dabs, lq1, lk1, lq2, lk2, g_col)


def _diff_slopes_log2():
    return np.asarray([2.0 ** (-8.0 * (i + 1) / N_HEADS_DIFF) for i in range(N_HEADS_DIFF)],
                      np.float32) * np.float32(LOG2E)


def _position_columns():
    def bf16_round(v):
        u = np.asarray(v, np.float32).view(np.uint32)
        return ((u + 0x7FFF + ((u >> 16) & 1)) & np.uint32(0xFFFF0000)).view(np.float32)

    kpos = np.zeros((N_HEADS_DIFF, DIFF_TK, LANES), np.float32)
    qpos = np.zeros((N_HEADS_DIFF, DIFF_TQ, LANES), np.float32)
    pk, pq = np.arange(DIFF_TK), np.arange(DIFF_TQ)
    for h, a in enumerate(_diff_slopes_log2()):
        rest = np.float32(a)
        for n in range(3):
            a_n = np.float32(bf16_round(rest))
            rest = np.float32(rest - a_n)
            c = 4 * n
            kpos[h, :, c + 0] = (pk // POS_SPLIT) * POS_SPLIT
            kpos[h, :, c + 1] = pk % POS_SPLIT
            kpos[h, :, c + 2] = a_n
            kpos[h, :, c + 3] = a_n
            qpos[h, :, c + 0] = a_n
            qpos[h, :, c + 1] = a_n
            qpos[h, :, c + 2] = -((pq // POS_SPLIT) * POS_SPLIT)
            qpos[h, :, c + 3] = -(pq % POS_SPLIT)
    dist = np.abs(pq[:, None] - pk[None, :]).astype(np.float32)
    dabs = -_diff_slopes_log2()[:, None, None] * dist[None]
    return jnp.asarray(kpos, jnp.bfloat16), jnp.asarray(qpos, jnp.bfloat16), jnp.asarray(dabs)


def _post_attn_kernel(o1_ref, l1_ref, o4_ref, l4_ref, o16_ref, l16_ref, od_ref, x_ref, wo_ref, g_ref, b_ref,
                      wr_ref, br_ref, tri_ref,
                      x1_ref, x1t_ref, idx_ref, gate_ref, rank_ref, cnt_ref, carry_scr, order_scr):
    step = pl.program_id(0)

    @pl.when(step == 0)
    def _():
        carry_scr[...] = jnp.zeros(carry_scr.shape, jnp.float32)

    tm = x_ref.shape[0]
    nc = DIL_WIDTH // LANES
    for n, (o_ref, l_ref) in enumerate(((o4_ref, l4_ref), (o16_ref, l16_ref))):
        d = o_ref.shape[1]
        for r in range(d):
            o_r = o_ref[0, r].astype(jnp.float32)
            l_r = l_ref[0, r]
            for c in range(nc):
                cols = slice(c * LANES, (c + 1) * LANES)
                order_scr[2 * n, c, pl.ds(r, tm // d, stride=d), :] = o_r[:, cols]
                order_scr[2 * n + 1, c, pl.ds(r, tm // d, stride=d), :] = l_r[:, cols]
    in_order = lambda n: jnp.concatenate([order_scr[n, c] for c in range(nc)], axis=1)
    outs = (o1_ref[0, 0].astype(jnp.float32), in_order(0), in_order(2))
    lses = (l1_ref[0, 0], in_order(1), in_order(3))
    top = jnp.maximum(jnp.maximum(lses[0], lses[1]), lses[2])
    wts = [jnp.exp(l - top) for l in lses]
    oa = ((wts[0] * outs[0] + wts[1] * outs[1] + wts[2] * outs[2])
          / (wts[0] + wts[1] + wts[2])).astype(jnp.bfloat16)

    mix = _dot(oa, wo_ref[:DIL_WIDTH, :]) + _dot(od_ref[...], wo_ref[DIL_WIDTH:, :])
    x1 = _layer_norm(DEEPNORM_ALPHA * x_ref[...] + mix, g_ref[...], b_ref[...])
    x1_ref[...] = x1
    for s in range(ROW_SUBLANES):
        x1t_ref[pl.ds(s, x1.shape[0], stride=ROW_SUBLANES), :] = x1[:, s * LANES:(s + 1) * LANES]

    tm = x1.shape[0]
    x_hi = x1.astype(jnp.bfloat16)
    x_lo = (x1 - x_hi.astype(jnp.float32)).astype(jnp.bfloat16)
    lg = (_dot_nt(wr_ref[0], x_hi) + _dot_nt(wr_ref[1], x_hi) + _dot_nt(wr_ref[0], x_lo)
          + br_ref[...])
    eidx = lax.broadcasted_iota(jnp.int32, (N_EXPERTS, tm), 0)
    vals, sels = [], []
    for k in range(TOP_K):
        mx = jnp.max(lg, axis=0, keepdims=True)
        idx = jnp.min(jnp.where(lg == mx, eidx, N_EXPERTS), axis=0, keepdims=True)
        sel = eidx == idx
        vals.append(mx)
        sels.append(sel)
        idx_ref[k:k + 1, :] = idx
        lg = jnp.where(sel, -jnp.inf, lg)
    ex = [jnp.exp(v - vals[0]) for v in vals]
    den = ex[0] + ex[1] + ex[2] + ex[3]
    for k in range(TOP_K):
        gate_ref[k:k + 1, :] = ex[k] / den

    chosen = (sels[0] | sels[1] | sels[2] | sels[3])
    onehot = jnp.where(chosen, 1.0, 0.0)
    before = _dot(onehot.astype(jnp.bfloat16), tri_ref[...]) + carry_scr[...]
    for k in range(TOP_K):
        rank_ref[k:k + 1, :] = jnp.sum(jnp.where(sels[k], before, 0.0), axis=0,
                                       keepdims=True).astype(jnp.int32)
    carry_scr[...] = carry_scr[...] + jnp.sum(onehot, axis=1, keepdims=True)
    pad = jnp.zeros((SUBLANES - TOP_K, tm), jnp.int32)
    idx_ref[TOP_K:, :] = pad
    rank_ref[TOP_K:, :] = pad
    gate_ref[TOP_K:, :] = pad.astype(jnp.float32)
    cnt_ref[...] = jnp.broadcast_to(carry_scr[...], cnt_ref.shape).astype(jnp.int32)


def _post_attn(dil, od, x, wo_bf, g, b, wr_t, br, tri):
    T, D = x.shape
    assert D == ROW_SUBLANES * LANES
    tm = ROW_TILE
    nb = dil[0][0].shape[2] // tm
    row = lambda w: pl.BlockSpec((tm, w), lambda i: (i, 0))
    full = lambda a: pl.BlockSpec(a.shape, lambda i: (0,) * a.ndim)
    col = pl.BlockSpec((SUBLANES, tm), lambda i: (0, i))
    dil_specs, dil_args = [], []
    for o_d, lse_d in dil:
        d = o_d.shape[1]
        spec = pl.BlockSpec((1, d, tm // d, DIL_WIDTH), lambda i: (i // nb, 0, i % nb, 0))
        dil_specs += [spec, spec]
        dil_args += [o_d, lse_d]
    return pl.pallas_call(
        _post_attn_kernel,
        grid=(T // tm,),
        in_specs=dil_specs + [row(DIFF_WIDTH), row(D), full(wo_bf), full(g), full(b), full(wr_t),
                              full(br), full(tri)],
        out_specs=[row(D), pl.BlockSpec((tm * ROW_SUBLANES, LANES), lambda i: (i, 0)), col, col, col,
                   pl.BlockSpec((N_EXPERTS, LANES), lambda i: (0, 0))],
        out_shape=[
            jax.ShapeDtypeStruct((T, D), jnp.float32),
            jax.ShapeDtypeStruct((T * ROW_SUBLANES, LANES), jnp.float32),
            jax.ShapeDtypeStruct((SUBLANES, T), jnp.int32),
            jax.ShapeDtypeStruct((SUBLANES, T), jnp.float32),
            jax.ShapeDtypeStruct((SUBLANES, T), jnp.int32),
            jax.ShapeDtypeStruct((N_EXPERTS, LANES), jnp.int32),
        ],
        scratch_shapes=[pltpu.VMEM((N_EXPERTS, 1), jnp.float32),
                        pltpu.VMEM((4, DIL_WIDTH // LANES, tm, LANES), jnp.float32)],
        compiler_params=_cparams(("arbitrary",)),
        name="post_attn_router",
    )(*dil_args, od, x, wo_bf, g, b, wr_t, br, tri)


def _index_copy(dest_hbm, step, idx_smem, sem):
    per_step = dest_hbm.shape[1]
    half = pl.multiple_of((step % 2) * per_step, per_step)
    return pltpu.make_async_copy(dest_hbm.at[step], idx_smem.at[pl.ds(half, per_step)], sem)


def _stage_index(dest_hbm, step, n_steps, idx_smem, sem):
    @pl.when(step == 0)
    def _():
        _index_copy(dest_hbm, 0, idx_smem, sem).start()

    _index_copy(dest_hbm, step, idx_smem, sem).wait()

    @pl.when(step + 1 < n_steps)
    def _():
        _index_copy(dest_hbm, step + 1, idx_smem, sem).start()

    return (step % 2) * dest_hbm.shape[1]


def _row_tile(ref, row):
    return ref.at[pl.ds(pl.multiple_of(row * ROW_SUBLANES, ROW_SUBLANES), ROW_SUBLANES)]


def _rows_from_tiles(ref, n):
    return jnp.concatenate([ref[pl.ds(s, n, stride=ROW_SUBLANES), :] for s in range(ROW_SUBLANES)], axis=1)


def _rows_to_tiles(ref, rows):
    for s in range(ROW_SUBLANES):
        ref[pl.ds(s, rows.shape[0], stride=ROW_SUBLANES), :] = rows[:, s * LANES:(s + 1) * LANES]


def _dispatch_kernel(pad_start_ref, n_pad_ref, n_used_ref, dest_hbm, x_ref, xs_hbm, idx_smem, idx_sem, row_sem,
                     zero_scr, zero_sem):
    tm = x_ref.shape[0] // ROW_SUBLANES
    tile_rows = MOE_TILE * ROW_SUBLANES
    n_tiles = xs_hbm.shape[0] // tile_rows

    def pad_copies(start_not_wait):
        for j in range(N_EXPERTS):
            tile = n_used_ref[0] + j

            @pl.when(tile < n_tiles)
            def _():
                dst = xs_hbm.at[pl.ds(pl.multiple_of(tile * tile_rows, tile_rows), tile_rows)]
                cp = pltpu.make_async_copy(zero_scr, dst, zero_sem)
                cp.start() if start_not_wait else cp.wait()

        for e in range(N_EXPERTS):
            n_pad = n_pad_ref[e]
            for bit in reversed(range(PAD_BITS)):
                size = 1 << bit

                @pl.when((n_pad & size) != 0)
                def _():
                    row = pad_start_ref[e] + (n_pad & ~(2 * size - 1))
                    dst = xs_hbm.at[pl.ds(pl.multiple_of(row * ROW_SUBLANES, ROW_SUBLANES), size * ROW_SUBLANES)]
                    cp = pltpu.make_async_copy(zero_scr.at[pl.ds(0, size * ROW_SUBLANES)], dst, zero_sem)
                    cp.start() if start_not_wait else cp.wait()

    @pl.when(pl.program_id(0) == 0)
    def _():
        zero_scr[...] = jnp.zeros(zero_scr.shape, zero_scr.dtype)
        pad_copies(True)
        pad_copies(False)

    base = _stage_index(dest_hbm, pl.program_id(0), pl.num_programs(0), idx_smem, idx_sem)

    def issue(t, carry):
        for k in range(TOP_K):
            pltpu.make_async_copy(_row_tile(x_ref, t), _row_tile(xs_hbm, idx_smem[base + k * tm + t]),
                                  row_sem).start(priority=k % 2)
        return carry

    lax.fori_loop(0, tm, issue, 0, unroll=8)
    for k in range(TOP_K):
        pltpu.make_async_copy(x_ref, xs_hbm.at[pl.ds(0, tm * ROW_SUBLANES)], row_sem).wait()


def _dispatch(pad_start, n_pad, n_used, dest_steps, x1t, n_rows):
    n_steps, per_step = dest_steps.shape
    tm = per_step // TOP_K
    assert MOE_TILE == 1 << PAD_BITS
    grid_spec = pltpu.PrefetchScalarGridSpec(
        num_scalar_prefetch=3,
        grid=(n_steps,),
        in_specs=[
            pl.BlockSpec(memory_space=pl.ANY),
            pl.BlockSpec((tm * ROW_SUBLANES, LANES), lambda i, ps, npad, nu: (i, 0)),
        ],
        out_specs=pl.BlockSpec(memory_space=pl.ANY),
        scratch_shapes=[pltpu.SMEM((2 * per_step,), jnp.int32), pltpu.SemaphoreType.DMA(()),
                        pltpu.SemaphoreType.DMA(()),
                        pltpu.VMEM((MOE_TILE * ROW_SUBLANES, LANES), jnp.float32),
                        pltpu.SemaphoreType.DMA(())],
    )
    return pl.pallas_call(
        _dispatch_kernel,
        grid_spec=grid_spec,
        out_shape=jax.ShapeDtypeStruct((n_rows * ROW_SUBLANES, LANES), jnp.float32),
        compiler_params=_cparams(("arbitrary",)),
        name="dispatch_rows",
    )(pad_start, n_pad, n_used, dest_steps, x1t)


def _expert_kernel(tile_e_ref, n_used_ref, xs_ref, wu_ref, bu_ref, wd_ref, bd_ref, y_ref, wu_scr, wd_scr):
    i = pl.program_id(0)
    used = i < n_used_ref[0]
    new_expert = (i == 0) | (tile_e_ref[i] != tile_e_ref[jnp.maximum(i - 1, 0)])
    tm = xs_ref.shape[0] // ROW_SUBLANES

    @pl.when(used & new_expert)
    def _():
        wu_scr[...] = wu_ref[0].astype(jnp.bfloat16)
        wd_scr[...] = wd_ref[0].astype(jnp.bfloat16)

    @pl.when(used)
    def _():
        xs = _rows_from_tiles(xs_ref, tm).astype(jnp.bfloat16)
        hu = _dot(xs, wu_scr[...]) + bu_ref[0]
        g = jnp.minimum(hu[:, :D_FF], SWIGLU_LIMIT)
        u = jnp.clip(hu[:, D_FF:], -SWIGLU_LIMIT, SWIGLU_LIMIT)
        act = g * (1.0 / (1.0 + jnp.exp(-SWIGLU_ALPHA * g))) * (u + 1.0)
        _rows_to_tiles(y_ref, _dot(act.astype(jnp.bfloat16), wd_scr[...]) + bd_ref[0])

    @pl.when(jnp.logical_not(used))
    def _():
        y_ref[...] = jnp.zeros(y_ref.shape, y_ref.dtype)


def _experts(tile_e, n_used, xs, wu, bu, wd, bd):
    D = ROW_SUBLANES * LANES
    n_rows = xs.shape[0] // ROW_SUBLANES
    tm = MOE_TILE
    tile_spec = pl.BlockSpec((tm * ROW_SUBLANES, LANES), lambda i, te, nu: (i, 0))
    grid_spec = pltpu.PrefetchScalarGridSpec(
        num_scalar_prefetch=2,
        grid=(n_rows // tm,),
        in_specs=[
            pl.BlockSpec((tm * ROW_SUBLANES, LANES), lambda i, te, nu: (jnp.minimum(i, nu[0] - 1), 0)),
            pl.BlockSpec((1, D, 2 * D_FF), lambda i, te, nu: (te[i], 0, 0)),
            pl.BlockSpec((1, 1, 2 * D_FF), lambda i, te, nu: (te[i], 0, 0)),
            pl.BlockSpec((1, D_FF, D), lambda i, te, nu: (te[i], 0, 0)),
            pl.BlockSpec((1, 1, D), lambda i, te, nu: (te[i], 0, 0)),
        ],
        out_specs=tile_spec,
        scratch_shapes=[pltpu.VMEM((D, 2 * D_FF), jnp.bfloat16), pltpu.VMEM((D_FF, D), jnp.bfloat16)],
    )
    return pl.pallas_call(
        _expert_kernel,
        grid_spec=grid_spec,
        out_shape=jax.ShapeDtypeStruct(xs.shape, jnp.float32),
        compiler_params=_cparams(("arbitrary",)),
        name="experts",
    )(tile_e, n_used, xs, wu, bu, wd, bd)


def _combine_kernel(dest_hbm, y_hbm, x1_ref, gate_ref, g_ref, b_ref, o_ref, buf, idx_smem, idx_sem, row_sems):
    i = pl.program_id(0)
    n = pl.num_programs(0)
    tm = x1_ref.shape[0]

    def issue(step, slot):
        base = _stage_index(dest_hbm, step, n, idx_smem, idx_sem)

        def body(t, carry):
            for k in range(TOP_K):
                pltpu.make_async_copy(_row_tile(y_hbm, idx_smem[base + k * tm + t]),
                                      _row_tile(buf.at[slot, k], t), row_sems.at[slot]).start(priority=k % 2)
            return carry

        lax.fori_loop(0, tm, body, 0, unroll=8)

    @pl.when(i == 0)
    def _():
        issue(i, 0)

    @pl.when(i + 1 < n)
    def _():
        issue(i + 1, (i + 1) % 2)

    slot = i % 2
    for k in range(TOP_K):
        pltpu.make_async_copy(y_hbm.at[pl.ds(0, tm * ROW_SUBLANES)], buf.at[slot, k], row_sems.at[slot]).wait()
    ffn = _rows_from_tiles(buf.at[slot, 0], tm) * gate_ref[:, 0:1]
    for k in range(1, TOP_K):
        ffn = ffn + _rows_from_tiles(buf.at[slot, k], tm) * gate_ref[:, k:k + 1]
    o_ref[...] = _layer_norm(DEEPNORM_ALPHA * x1_ref[...] + ffn, g_ref[...], b_ref[...])


def _combine(dest_steps, y, x1, gates_t, g, b):
    T, D = x1.shape
    n_steps, per_step = dest_steps.shape
    tm = per_step // TOP_K
    full = lambda a: pl.BlockSpec(a.shape, lambda i: (0, 0))
    return pl.pallas_call(
        _combine_kernel,
        grid=(n_steps,),
        in_specs=[
            pl.BlockSpec(memory_space=pl.ANY),
            pl.BlockSpec(memory_space=pl.ANY),
            pl.BlockSpec((tm, D), lambda i: (i, 0)),
            pl.BlockSpec((tm, SUBLANES), lambda i: (i, 0)),
            full(g), full(b),
        ],
        out_specs=pl.BlockSpec((tm, D), lambda i: (i, 0)),
        out_shape=jax.ShapeDtypeStruct((T, D), jnp.float32),
        scratch_shapes=[pltpu.VMEM((2, TOP_K, tm * ROW_SUBLANES, LANES), jnp.float32),
                        pltpu.SMEM((2 * per_step,), jnp.int32),
                        pltpu.SemaphoreType.DMA(()), pltpu.SemaphoreType.DMA((2,))],
        compiler_params=_cparams(("arbitrary",)),
        name="combine_ln2",
    )(dest_steps, y, x1, gates_t, g, b)


def kernel(x, w_in, w_out, lambda_q1, lambda_k1, lambda_q2, lambda_k2, diff_norm_g, ln1_g, ln1_b,
           w_router, b_router, w_up, b_up, w_down, b_down, ln2_g, ln2_b):
    B, S, D = x.shape
    T = B * S
    assert w_in.shape[0] == 1, "single layer"
    assert S % DIFF_TQ == 0 and DIFF_TK == DIFF_TQ and DIFF_TK % ROW_TILE == 0 and S % (16 * DIL_TQ) == 0

    na = 3 * DIL_WIDTH
    colscale = np.ones((3 * D,), np.float32)
    colscale[:DIL_WIDTH] = HEAD_DIM ** -0.5
    colscale[na:na + DIFF_WIDTH] = HEAD_DIM ** -0.5 * LOG2E
    w_in_bf = (w_in[0] * colscale).astype(jnp.bfloat16)
    nv = na + 2 * DIFF_WIDTH

    kpos, qpos, dabs = _position_columns()
    pa, pa4, pa16, qd, ka, vt, qn, kn = _in_proj(x, w_in_bf[:, :nv], w_in_bf[:, nv:].T, kpos)
    dil = [_dilated_stage(p, d) for p, (_, d) in zip((pa[:, None], pa4, pa16), DIL_PATTERNS)]

    r2 = lambda a: a.reshape(1, -1).astype(jnp.float32)
    od = _diff_attention(_tile_reach(qn, kn, S // DIFF_TK), qd, ka, vt, qpos, dabs, r2(lambda_q1[0]), r2(lambda_k1[0]), r2(lambda_q2[0]),
                         r2(lambda_k2[0]), diff_norm_g[0].reshape(-1, 1).astype(jnp.float32))

    tri = jnp.asarray(np.triu(np.ones((ROW_TILE, ROW_TILE), np.float32), k=1), jnp.bfloat16)
    n_rows = T * TOP_K + N_EXPERTS * MOE_TILE
    n_tiles = n_rows // MOE_TILE
    wr_t = w_router[0].T.astype(jnp.float32)
    wr_hi = wr_t.astype(jnp.bfloat16)
    wr_split = jnp.stack([wr_hi, (wr_t - wr_hi.astype(jnp.float32)).astype(jnp.bfloat16)])
    x1, x1t, top_idx, gates, rank, counts = _post_attn(
        dil, od.reshape(T, DIFF_WIDTH), x.reshape(T, D), w_out[0].astype(jnp.bfloat16),
        r2(ln1_g[0]), r2(ln1_b[0]), wr_split, b_router[0].reshape(N_EXPERTS, 1), tri)

    counts = counts[:, 0]
    pcounts = ((counts + MOE_TILE - 1) // MOE_TILE) * MOE_TILE
    pends = jnp.cumsum(pcounts)
    pstarts = pends - pcounts
    dest = rank[:TOP_K]
    for e in range(N_EXPERTS):
        dest = dest + jnp.where(top_idx[:TOP_K] == e, pstarts[e], 0)
    tile_starts = jnp.arange(n_tiles, dtype=jnp.int32) * MOE_TILE
    tile_e = jnp.minimum(jnp.sum(pends[None, :] <= tile_starts[:, None], axis=1), N_EXPERTS - 1).astype(jnp.int32)
    n_used = (pends[-1] // MOE_TILE).astype(jnp.int32).reshape(1)

    def per_step(tm):
        return dest.reshape(TOP_K, T // tm, tm).transpose(1, 0, 2).reshape(T // tm, TOP_K * tm)

    xs = _dispatch((pstarts + counts).astype(jnp.int32), (pcounts - counts).astype(jnp.int32), n_used,
                   per_step(ROW_TILE), x1t, n_rows)
    y = _experts(tile_e, n_used, xs, w_up[0], b_up[0][:, None, :], w_down[0], b_down[0][:, None, :])
    out = _combine(per_step(COMBINE_TILE), y, x1, gates.T, r2(ln2_g[0]), r2(ln2_b[0]))
    return out.reshape(B, S, D)
```
